```python
import math
import jax
import jax.numpy as jnp
from jax import lax
import numpy as np

D_MODEL = 1024
BATCH = 4
SEQ = 4096
DEPTH = 2
DEC_BATCH = 8
DEC_SEQ = 32
PAST_LEN = 1024

CHUNK = 64
N_MEM = 256
EPS = 1e-6
D_INNER = 2 * D_MODEL
SSD_HEADDIM = 64
SSD_HEADS = D_INNER // SSD_HEADDIM
SSD_GROUPS = 4
SSD_HPG = SSD_HEADS // SSD_GROUPS
D_STATE = 128
CONV_W = 4
GN = SSD_GROUPS * D_STATE
CONV_DIM = D_INNER + 2 * GN
SSD_BLOCK = CHUNK
SSD_MIX_IN = D_INNER + CONV_DIM + SSD_HEADS
ATT_HEADS = 16
ATT_HD = D_MODEL // ATT_HEADS
KV_HEADS = 4
ATT_GROUPS = ATT_HEADS // KV_HEADS
ATT_Q = ATT_HEADS * ATT_HD
ATT_KV = KV_HEADS * ATT_HD
IDX_HEADS = 8
IDX_HD = 64
IDX_Q = IDX_HEADS * IDX_HD
IDX_SCALE = (IDX_HEADS * IDX_HD) ** -0.5
TOPK_MAX = 256
Q_BLOCK = 128
ROPE_THETA = 10000.0
O_K = ATT_Q
O_V = O_K + ATT_KV
O_QI = O_V + ATT_KV
O_WI = O_QI + IDX_Q
O_KI = O_WI + IDX_HEADS
ATT_MIX_IN = O_KI + IDX_HD
MEM_HEADS = 4
MEM_HD = 256
MEM_WIDTH = MEM_HEADS * MEM_HD
SSD_IN = SSD_MIX_IN + MEM_WIDTH
ATT_IN = ATT_MIX_IN + MEM_WIDTH
D_FF = 3584
N_EXPERTS = 8
TOP_K_EXPERTS = 2
N_SSD = (DEPTH + 1) // 2
N_ATT = DEPTH // 2

kernel_name = "hybrid_ssd_dsa_streaming_step"


def rmsnorm(x, g):
    xf = x.astype(jnp.float32)
    y = xf * lax.rsqrt(jnp.mean(xf * xf, axis=-1, keepdims=True) + EPS)
    return (y * g.astype(jnp.float32)).astype(x.dtype)


def rope(x, pos):
    half = x.shape[-1] // 2
    inv = ROPE_THETA ** (-jnp.arange(half, dtype=jnp.float32) / half)
    ang = pos.astype(jnp.float32)[:, None] * inv[None, :]
    cos = jnp.cos(ang)[:, None, :]
    sin = jnp.sin(ang)[:, None, :]
    xf = x.astype(jnp.float32)
    x1, x2 = xf[..., :half], xf[..., half:]
    return jnp.concatenate([x1 * cos - x2 * sin, x2 * cos + x1 * sin], axis=-1).astype(x.dtype)


def swiglu(h, wg, wu, wd):
    return (jax.nn.silu(h @ wg) * (h @ wu)) @ wd


def moe_swiglu(h, router, wg, wu, wd):
    probs = jax.nn.softmax((h @ router).astype(jnp.float32), axis=-1)
    top_v, top_i = lax.top_k(probs, TOP_K_EXPERTS)
    top_v = top_v / jnp.sum(top_v, axis=-1, keepdims=True)
    gates = jnp.sum(jax.nn.one_hot(top_i, N_EXPERTS, dtype=jnp.float32) * top_v[..., None], axis=-2)
    gates = gates.astype(h.dtype)
    out = jnp.zeros_like(h)
    for e in range(N_EXPERTS):
        out = out + gates[..., e:e + 1] * swiglu(h, wg[e], wu[e], wd[e])
    return out


def ssd_scan(xs, dt, A, Bg, Cg, h0):
    b, T = xs.shape[:2]
    blk = min(SSD_BLOCK, T)
    c = T // blk
    f32 = jnp.float32
    X = (xs.astype(f32) * dt[..., None]).reshape(b, c, blk, SSD_GROUPS, SSD_HPG, SSD_HEADDIM)
    a = (dt * A).reshape(b, c, blk, SSD_GROUPS, SSD_HPG)
    Bc = Bg.astype(f32).reshape(b, c, blk, SSD_GROUPS, D_STATE)
    Cc = Cg.astype(f32).reshape(b, c, blk, SSD_GROUPS, D_STATE)
    acum = jnp.cumsum(a, axis=2)
    seg = acum[:, :, :, None] - acum[:, :, None, :]
    causal = jnp.tril(jnp.ones((blk, blk), bool))[None, None, :, :, None, None]
    Lmat = jnp.where(causal, jnp.exp(jnp.where(causal, seg, 0.0)), 0.0)
    CB = jnp.einsum('bclgn,bcsgn->bclsg', Cc, Bc)
    y_diag = jnp.einsum('bclsge,bcsgep->bclgep', CB[..., None] * Lmat, X)
    decay_to_end = jnp.exp(acum[:, :, -1:] - acum)
    S_blk = jnp.einsum('bclgn,bclge,bclgep->bcgepn', Bc, decay_to_end, X)
    blk_decay = jnp.exp(acum[:, :, -1])

    def step(h, inp):
        s, d = inp
        return d[..., None, None] * h + s, h

    h0g = h0.reshape(b, SSD_GROUPS, SSD_HPG, SSD_HEADDIM, D_STATE)
    hT, h_in = lax.scan(step, h0g, (jnp.moveaxis(S_blk, 1, 0), jnp.moveaxis(blk_decay, 1, 0)))
    h_in = jnp.moveaxis(h_in, 0, 1)
    y_off = jnp.einsum('bclgn,bcgepn,bclge->bclgep', Cc, h_in, jnp.exp(acum))
    y = (y_diag + y_off).reshape(b, T, SSD_HEADS, SSD_HEADDIM)
    return y, hT.reshape(b, SSD_HEADS, SSD_HEADDIM, D_STATE)


def ssd_mixer(u, conv_state, ssm_state, conv_w, conv_b, dt_bias, A_log, D_skip, out_gain):
    b, T, _ = u.shape
    z = u[..., :D_INNER]
    xbc = u[..., D_INNER:D_INNER + CONV_DIM]
    dt_raw = u[..., D_INNER + CONV_DIM:]
    xpad = jnp.concatenate([conv_state.astype(xbc.dtype), xbc], axis=1)
    conv = conv_b + sum(xpad[:, w:w + T] * conv_w[w] for w in range(CONV_W))
    new_conv = xpad[:, T:]
    xbc = jax.nn.silu(conv)
    xs = xbc[..., :D_INNER].reshape(b, T, SSD_HEADS, SSD_HEADDIM)
    Bg = xbc[..., D_INNER:D_INNER + GN].reshape(b, T, SSD_GROUPS, D_STATE)
    Cg = xbc[..., D_INNER + GN:].reshape(b, T, SSD_GROUPS, D_STATE)
    dt = jax.nn.softplus(dt_raw.astype(jnp.float32) + dt_bias.astype(jnp.float32))
    A = -jnp.exp(A_log.astype(jnp.float32))
    y, h_new = ssd_scan(xs, dt, A, Bg, Cg, ssm_state.astype(jnp.float32))
    y = y + D_skip.astype(jnp.float32)[:, None] * xs.astype(jnp.float32)
    g = (y.reshape(b, T, D_INNER) * jax.nn.silu(z.astype(jnp.float32)))
    g = g.reshape(b, T, SSD_GROUPS, D_INNER // SSD_GROUPS)
    g = g * lax.rsqrt(jnp.mean(g * g, axis=-1, keepdims=True) + EPS)
    y = (g.reshape(b, T, D_INNER) * out_gain.astype(jnp.float32)).astype(u.dtype)
    return y, new_conv, h_new


def mem_kv(mem, g_mem, w_kv, k_gain):
    b = mem.shape[0]
    kv = rmsnorm(mem, g_mem) @ w_kv
    k = kv[..., :MEM_WIDTH].reshape(b, N_MEM, MEM_HEADS, MEM_HD)
    v = kv[..., MEM_WIDTH:].reshape(b, N_MEM, MEM_HEADS, MEM_HD)
    return rmsnorm(k, k_gain), v


def mem_attend(uq, q_gain, mk, mv):
    b, T, _ = uq.shape
    q = rmsnorm(uq.reshape(b, T, MEM_HEADS, MEM_HD), q_gain)
    s = jnp.einsum('bthd,bmhd->bhtm', q, mk.astype(q.dtype)).astype(jnp.float32) * MEM_HD ** -0.5
    p = jax.nn.softmax(s, axis=-1).astype(q.dtype)
    o = jnp.einsum('bhtm,bmhd->bthd', p, mv.astype(q.dtype))
    return o.reshape(b, T, MEM_WIDTH)


def dsa_project(u, pos, q_gain, k_gain):
    b, T, _ = u.shape
    q = rope(rmsnorm(u[..., :O_K].reshape(b, T, ATT_HEADS, ATT_HD), q_gain), pos)
    k = rope(rmsnorm(u[..., O_K:O_V].reshape(b, T, KV_HEADS, ATT_HD), k_gain), pos)
    v = u[..., O_V:O_QI].reshape(b, T, KV_HEADS, ATT_HD)
    qi = rope(u[..., O_QI:O_WI].reshape(b, T, IDX_HEADS, IDX_HD), pos)
    wi = u[..., O_WI:O_KI] * IDX_SCALE
    ki = rope(u[..., O_KI:ATT_MIX_IN][:, :, None, :], pos)[:, :, 0]
    return q, k, v, qi, wi, ki


def sparse_attention(q, qi, wi, K, V, KI, pos):
    b, T = q.shape[:2]
    L = K.shape[1]
    n_sel = min(TOPK_MAX, L // 4)
    limit = jnp.minimum((pos // CHUNK + 1) * CHUNK, L)
    qb = min(Q_BLOCK, T)
    nb = T // qb
    key_pos = jnp.arange(L)
    take = jax.vmap(lambda arr, idx: arr[idx])

    def blocks(a):
        return jnp.moveaxis(a.reshape((b, nb, qb) + a.shape[2:]), 1, 0)

    def one_block(args):
        q_b, qi_b, wi_b, lim_b = args
        dots = jnp.einsum('bqhd,bld->bqhl', qi_b.astype(jnp.float32), KI.astype(jnp.float32))
        score = jnp.einsum('bqhl,bqh->bql', jax.nn.relu(dots), wi_b.astype(jnp.float32))
        admissible = key_pos[None, :] < lim_b[:, None]
        score = jnp.where(admissible[None], score, -jnp.inf)
        _, sel = lax.top_k(score, n_sel)
        valid = sel < lim_b[None, :, None]
        Ks = take(K, sel).astype(q_b.dtype)
        Vs = take(V, sel).astype(q_b.dtype)
        qg = q_b.reshape(b, qb, KV_HEADS, ATT_GROUPS, ATT_HD)
        s = jnp.einsum('bqhgd,bqkhd->bqhgk', qg, Ks).astype(jnp.float32) * ATT_HD ** -0.5
        s = jnp.where(valid[:, :, None, None, :], s, -jnp.inf)
        p = jax.nn.softmax(s, axis=-1).astype(q_b.dtype)
        o = jnp.einsum('bqhgk,bqkhd->bqhgd', p, Vs)
        return o.reshape(b, qb, ATT_Q)

    out = lax.map(one_block, (blocks(q), blocks(qi), blocks(wi), limit.reshape(nb, qb)))
    return jnp.moveaxis(out, 0, 1).reshape(b, T, ATT_Q)


def run_trunk(x, start, mem_k, mem_v, conv_in, ssm_in, kv_in, W):
    T = x.shape[1]
    pos = start + jnp.arange(T)
    new_conv, new_ssm, new_k, new_v, new_ki = [], [], [], [], []
    for i in range(DEPTH):
        j = i // 2
        if i % 2 == 0:
            u = rmsnorm(x, W['ssd_norm'][j]) @ W['ssd_w_in'][j]
            y_mix, cs, hs = ssd_mixer(u[..., :SSD_MIX_IN], conv_in[j], ssm_in[j], W['ssd_conv_w'][j],
                                      W['ssd_conv_b'][j], W['ssd_dt_bias'][j], W['ssd_A_log'][j],
                                      W['ssd_D'][j], W['ssd_out_norm'][j])
            y_mem = mem_attend(u[..., SSD_MIX_IN:], W['mem_q_norm'][i], mem_k[i], mem_v[i])
            x = x + jnp.concatenate([y_mix, y_mem], axis=-1) @ W['ssd_w_out'][j]
            x = x + swiglu(rmsnorm(x, W['ffn_norm'][i]), W['dense_w_gate'][j], W['dense_w_up'][j],
                           W['dense_w_down'][j])
            new_conv.append(cs)
            new_ssm.append(hs)
        else:
            u = rmsnorm(x, W['att_norm'][j]) @ W['att_w_in'][j]
            q, k, v, qi, wi, ki = dsa_project(u[..., :ATT_MIX_IN], pos, W['att_q_norm'][j], W['att_k_norm'][j])
            if kv_in is None:
                K, V, KI = k, v, ki
            else:
                K = jnp.concatenate([kv_in[0][j].astype(k.dtype), k], axis=1)
                V = jnp.concatenate([kv_in[1][j].astype(v.dtype), v], axis=1)
                KI = jnp.concatenate([kv_in[2][j].astype(ki.dtype), ki], axis=1)
            y_mix = sparse_attention(q, qi, wi, K, V, KI, pos)
            y_mem = mem_attend(u[..., ATT_MIX_IN:], W['mem_q_norm'][i], mem_k[i], mem_v[i])
            x = x + jnp.concatenate([y_mix, y_mem], axis=-1) @ W['att_w_out'][j]
            x = x + moe_swiglu(rmsnorm(x, W['ffn_norm'][i]), W['moe_router'][j], W['moe_w_gate'][j],
                               W['moe_w_up'][j], W['moe_w_down'][j])
            new_k.append(k)
            new_v.append(v)
            new_ki.append(ki)
    return x, new_conv, new_ssm, new_k, new_v, new_ki


def setup_inputs(seed: int = 0) -> dict:
    key = jax.random.key(seed)
    keys = list(jax.random.split(key, 48))

    def nrm(shape, scale=1.0):
        return jax.random.normal(keys.pop(), shape, jnp.float32) * scale

    def gain(shape):
        return 1.0 + 0.05 * nrm(shape)

    dt0 = jnp.exp(jax.random.uniform(keys.pop(), (N_SSD, SSD_HEADS), jnp.float32,
                                     math.log(1e-3), math.log(1e-1)))
    dt_bias = dt0 + jnp.log(-jnp.expm1(-dt0))
    A_log = jnp.log(jax.random.uniform(keys.pop(), (N_SSD, SSD_HEADS), jnp.float32, 1.0, 16.0))
    return {
        'x_prompt': nrm((BATCH, SEQ, D_MODEL)),
        'x_sample': nrm((DEC_BATCH, DEC_SEQ, D_MODEL)),
        'mem_prompt': nrm((BATCH, N_MEM, D_MODEL)),
        'cache_conv': nrm((N_SSD, DEC_BATCH, CONV_W - 1, CONV_DIM)),
        'state_ssm': nrm((N_SSD, DEC_BATCH, SSD_HEADS, SSD_HEADDIM, D_STATE), 0.5),
        'cache_k': nrm((N_ATT, DEC_BATCH, PAST_LEN, KV_HEADS, ATT_HD)),
        'cache_v': nrm((N_ATT, DEC_BATCH, PAST_LEN, KV_HEADS, ATT_HD)),
        'cache_idx_k': nrm((N_ATT, DEC_BATCH, PAST_LEN, IDX_HD)),
        'cache_mem_k': nrm((DEPTH, DEC_BATCH, N_MEM, MEM_HEADS, MEM_HD)),
        'cache_mem_v': nrm((DEPTH, DEC_BATCH, N_MEM, MEM_HEADS, MEM_HD)),
        'ssd_norm': gain((N_SSD, D_MODEL)),
        'ssd_w_in': nrm((N_SSD, D_MODEL, SSD_IN), D_MODEL ** -0.5),
        'ssd_conv_w': nrm((N_SSD, CONV_W, CONV_DIM), CONV_W ** -0.5),
        'ssd_conv_b': nrm((N_SSD, CONV_DIM), 0.01),
        'ssd_dt_bias': dt_bias,
        'ssd_A_log': A_log,
        'ssd_D': gain((N_SSD, SSD_HEADS)),
        'ssd_out_norm': gain((N_SSD, D_INNER)),
        'ssd_w_out': nrm((N_SSD, D_INNER + MEM_WIDTH, D_MODEL), (D_INNER + MEM_WIDTH) ** -0.5),
        'att_norm': gain((N_ATT, D_MODEL)),
        'att_w_in': nrm((N_ATT, D_MODEL, ATT_IN), D_MODEL ** -0.5),
        'att_q_norm': gain((N_ATT, ATT_HD)),
        'att_k_norm': gain((N_ATT, ATT_HD)),
        'att_w_out': nrm((N_ATT, ATT_Q + MEM_WIDTH, D_MODEL), (ATT_Q + MEM_WIDTH) ** -0.5),
        'mem_norm': gain((DEPTH, D_MODEL)),
        'mem_w_kv': nrm((DEPTH, D_MODEL, 2 * MEM_WIDTH), D_MODEL ** -0.5),
        'mem_q_norm': gain((DEPTH, MEM_HD)),
        'mem_k_norm': gain((DEPTH, MEM_HD)),
        'ffn_norm': gain((DEPTH, D_MODEL)),
        'dense_w_gate': nrm((N_SSD, D_MODEL, D_FF), D_MODEL ** -0.5),
        'dense_w_up': nrm((N_SSD, D_MODEL, D_FF), D_MODEL ** -0.5),
        'dense_w_down': nrm((N_SSD, D_FF, D_MODEL), D_FF ** -0.5),
        'moe_router': nrm((N_ATT, D_MODEL, N_EXPERTS), D_MODEL ** -0.5),
        'moe_w_gate': nrm((N_ATT, N_EXPERTS, D_MODEL, D_FF), D_MODEL ** -0.5),
        'moe_w_up': nrm((N_ATT, N_EXPERTS, D_MODEL, D_FF), D_MODEL ** -0.5),
        'moe_w_down': nrm((N_ATT, N_EXPERTS, D_FF, D_MODEL), D_FF ** -0.5),
    }


def reference(x_prompt, x_sample, mem_prompt, cache_conv, state_ssm, cache_k, cache_v, cache_idx_k,
              cache_mem_k, cache_mem_v, ssd_norm, ssd_w_in, ssd_conv_w, ssd_conv_b, ssd_dt_bias, ssd_A_log,
              ssd_D, ssd_out_norm, ssd_w_out, att_norm, att_w_in, att_q_norm, att_k_norm, att_w_out,
              mem_norm, mem_w_kv, mem_q_norm, mem_k_norm, ffn_norm, dense_w_gate, dense_w_up, dense_w_down,
              moe_router, moe_w_gate, moe_w_up, moe_w_down):
    W = dict(ssd_norm=ssd_norm, ssd_w_in=ssd_w_in, ssd_conv_w=ssd_conv_w, ssd_conv_b=ssd_conv_b,
             ssd_dt_bias=ssd_dt_bias, ssd_A_log=ssd_A_log, ssd_D=ssd_D, ssd_out_norm=ssd_out_norm,
             ssd_w_out=ssd_w_out, att_norm=att_norm, att_w_in=att_w_in, att_q_norm=att_q_norm,
             att_k_norm=att_k_norm, att_w_out=att_w_out, mem_q_norm=mem_q_norm, ffn_norm=ffn_norm,
             dense_w_gate=dense_w_gate, dense_w_up=dense_w_up, dense_w_down=dense_w_down,
             moe_router=moe_router, moe_w_gate=moe_w_gate, moe_w_up=moe_w_up, moe_w_down=moe_w_down)
    b_p = x_prompt.shape[0]
    p_mem = [mem_kv(mem_prompt, mem_norm[i], mem_w_kv[i], mem_k_norm[i]) for i in range(DEPTH)]
    p_mem_k = jnp.stack([m[0] for m in p_mem])
    p_mem_v = jnp.stack([m[1] for m in p_mem])
    conv0 = jnp.zeros((N_SSD, b_p, CONV_W - 1, CONV_DIM), x_prompt.dtype)
    ssm0 = jnp.zeros((N_SSD, b_p, SSD_HEADS, SSD_HEADDIM, D_STATE), jnp.float32)
    y_prompt, p_conv, p_ssm, p_k, p_v, p_ki = run_trunk(x_prompt, 0, p_mem_k, p_mem_v, conv0, ssm0, None, W)
    y_sample, s_conv, s_ssm, s_k, s_v, s_ki = run_trunk(
        x_sample, cache_k.shape[2], cache_mem_k, cache_mem_v, cache_conv, state_ssm,
        (cache_k, cache_v, cache_idx_k), W)
    return (y_prompt, y_sample, jnp.stack(p_conv), jnp.stack(p_ssm), jnp.stack(p_k), jnp.stack(p_v),
            jnp.stack(p_ki), p_mem_k, p_mem_v, jnp.stack(s_conv), jnp.stack(s_ssm), jnp.stack(s_k),
            jnp.stack(s_v), jnp.stack(s_ki))
```

```python
import functools
import math

import jax
import jax.numpy as jnp
from jax import lax
from jax.experimental import pallas as pl
from jax.experimental.pallas import tpu as pltpu

F32 = jnp.float32
BF16 = jnp.bfloat16
I32 = jnp.int32

D_MODEL = 1024
CHUNK = 64
N_MEM = 256
EPS = 1e-6
D_INNER = 2048
SSD_HEADDIM = 64
SSD_HEADS = 32
SSD_GROUPS = 4
SSD_HPG = 8
D_STATE = 128
CONV_W = 4
GN = SSD_GROUPS * D_STATE
CONV_DIM = D_INNER + 2 * GN
ATT_HEADS = 16
ATT_HD = 64
KV_HEADS = 4
ATT_GROUPS = 4
ATT_Q = 1024
ATT_KV = 256
IDX_HEADS = 8
IDX_HD = 64
IDX_Q = 512
IDX_SCALE = (IDX_HEADS * IDX_HD) ** -0.5
TOPK_MAX = 256
ROPE_THETA = 10000.0
MEM_HEADS = 4
MEM_HD = 256
MEM_WIDTH = 1024
D_FF = 3584
N_EXPERTS = 8

LANES = 128
INT_MIN = -(2 ** 31)
NEG_BIG = -1e30
VMEM_LIMIT = 56 * 1024 * 1024


def _cparams(sem):
    return pltpu.CompilerParams(dimension_semantics=sem, vmem_limit_bytes=VMEM_LIMIT)


def _dot(a, b):
    return jnp.dot(a, b, preferred_element_type=F32)


def _split3(v):
    hi = v.astype(BF16)
    r1 = v - hi.astype(F32)
    mid = r1.astype(BF16)
    lo = (r1 - mid.astype(F32)).astype(BF16)
    return hi, mid, lo


def _silu(x):
    return x / (1.0 + jnp.exp(-x))


def _norm_matmul_kernel(x_ref, g_ref, w_ref, o_ref, h_ref):
    @pl.when(pl.program_id(1) == 0)
    def _():
        x = x_ref[...]
        ms = jnp.mean(x * x, axis=-1, keepdims=True)
        h_ref[...] = (x * lax.rsqrt(ms + EPS) * g_ref[...]).astype(BF16)

    o_ref[...] = _dot(h_ref[...], w_ref[...]).astype(o_ref.dtype)


def _norm_matmul(x, g, w, tn, out_dtype=F32):
    n, k = x.shape
    m = w.shape[1]
    tm = min(512, n)
    return pl.pallas_call(
        _norm_matmul_kernel,
        grid=(n // tm, m // tn),
        in_specs=[
            pl.BlockSpec((tm, k), lambda i, j: (i, 0)),
            pl.BlockSpec((1, k), lambda i, j: (0, 0)),
            pl.BlockSpec((k, tn), lambda i, j: (0, j)),
        ],
        out_specs=pl.BlockSpec((tm, tn), lambda i, j: (i, j)),
        out_shape=jax.ShapeDtypeStruct((n, m), out_dtype),
        scratch_shapes=[pltpu.VMEM((tm, k), BF16)],
        compiler_params=_cparams(("parallel", "arbitrary")),
        name="norm_matmul",
    )(x, g.reshape(1, k), w)


def _head_norm_kernel(x_ref, g_ref, o_ref):
    for h in range(MEM_HEADS):
        x = x_ref[:, h * MEM_HD:(h + 1) * MEM_HD]
        ms = jnp.mean(x * x, axis=-1, keepdims=True)
        o_ref[:, h * MEM_HD:(h + 1) * MEM_HD] = x * lax.rsqrt(ms + EPS) * g_ref[...]


def _head_norm(x, g):
    n = x.shape[0]
    tm = min(512, n)
    return pl.pallas_call(
        _head_norm_kernel,
        grid=(n // tm,),
        in_specs=[pl.BlockSpec((tm, MEM_WIDTH), lambda i: (i, 0)),
                  pl.BlockSpec((1, MEM_HD), lambda i: (0, 0))],
        out_specs=pl.BlockSpec((tm, MEM_WIDTH), lambda i: (i, 0)),
        out_shape=jax.ShapeDtypeStruct((n, MEM_WIDTH), F32),
        compiler_params=_cparams(("parallel",)),
        name="head_norm",
    )(x, g.reshape(1, MEM_HD))


def _mem_attn_kernel(q_ref, k_ref, v_ref, g_ref, o_ref):
    for h in range(MEM_HEADS):
        sl = slice(h * MEM_HD, (h + 1) * MEM_HD)
        q = q_ref[0, :, sl]
        ms = jnp.mean(q * q, axis=-1, keepdims=True)
        qn = (q * lax.rsqrt(ms + EPS) * g_ref[...]).astype(BF16)
        k = k_ref[0, :, sl].astype(BF16)
        s = lax.dot_general(qn, k, (((1,), (1,)), ((), ())), preferred_element_type=F32)
        s = s * (MEM_HD ** -0.5)
        m = jnp.max(s, axis=-1, keepdims=True)
        p = jnp.exp(s - m)
        p = p / jnp.sum(p, axis=-1, keepdims=True)
        o = _dot(p.astype(BF16), v_ref[0, :, sl].astype(BF16))
        o_ref[0, :, sl] = o.astype(o_ref.dtype)


def _mem_attn(u, col_block, mk, mv, g):
    b, t, _ = u.shape
    tq = min(512, t)
    return pl.pallas_call(
        _mem_attn_kernel,
        grid=(b, t // tq),
        in_specs=[
            pl.BlockSpec((1, tq, MEM_WIDTH), lambda i, j: (i, j, col_block)),
            pl.BlockSpec((1, N_MEM, MEM_WIDTH), lambda i, j: (i, 0, 0)),
            pl.BlockSpec((1, N_MEM, MEM_WIDTH), lambda i, j: (i, 0, 0)),
            pl.BlockSpec((1, MEM_HD), lambda i, j: (0, 0)),
        ],
        out_specs=pl.BlockSpec((1, tq, MEM_WIDTH), lambda i, j: (i, j, 0)),
        out_shape=jax.ShapeDtypeStruct((b, t, MEM_WIDTH), BF16),
        compiler_params=_cparams(("parallel", "parallel")),
        name="mem_attn",
    )(u, mk, mv, g.reshape(1, MEM_HD))


def _proj_res_kernel(a_ref, b_ref, wa_ref, wb_ref, x_ref, o_ref):
    o_ref[...] = x_ref[...] + _dot(a_ref[...], wa_ref[...]) + _dot(b_ref[...], wb_ref[...])


def _proj_res(a, b, wa, wb, x):
    n = x.shape[0]
    tm = min(512, n)
    ka, kb = a.shape[1], b.shape[1]
    return pl.pallas_call(
        _proj_res_kernel,
        grid=(n // tm,),
        in_specs=[
            pl.BlockSpec((tm, ka), lambda i: (i, 0)),
            pl.BlockSpec((tm, kb), lambda i: (i, 0)),
            pl.BlockSpec((ka, D_MODEL), lambda i: (0, 0)),
            pl.BlockSpec((kb, D_MODEL), lambda i: (0, 0)),
            pl.BlockSpec((tm, D_MODEL), lambda i: (i, 0)),
        ],
        out_specs=pl.BlockSpec((tm, D_MODEL), lambda i: (i, 0)),
        out_shape=jax.ShapeDtypeStruct((n, D_MODEL), F32),
        compiler_params=_cparams(("parallel",)),
        name="proj_res",
    )(a, b, wa, wb, x)


def _ffn_kernel(x_ref, g_ref, wg_ref, wu_ref, wd_ref, o_ref, h_ref, acc_ref):
    f = pl.program_id(1)

    @pl.when(f == 0)
    def _():
        x = x_ref[...]
        ms = jnp.mean(x * x, axis=-1, keepdims=True)
        h_ref[...] = (x * lax.rsqrt(ms + EPS) * g_ref[...]).astype(BF16)
        acc_ref[...] = x

    h = h_ref[...]
    a = _silu(_dot(h, wg_ref[...])) * _dot(h, wu_ref[...])
    acc_ref[...] += _dot(a.astype(BF16), wd_ref[...])

    @pl.when(f == pl.num_programs(1) - 1)
    def _():
        o_ref[...] = acc_ref[...]


def _ffn(x, g, wg, wu, wd, tf=512):
    n = x.shape[0]
    tm = min(512, n)
    return pl.pallas_call(
        _ffn_kernel,
        grid=(n // tm, D_FF // tf),
        in_specs=[
            pl.BlockSpec((tm, D_MODEL), lambda i, f: (i, 0)),
            pl.BlockSpec((1, D_MODEL), lambda i, f: (0, 0)),
            pl.BlockSpec((D_MODEL, tf), lambda i, f: (0, f)),
            pl.BlockSpec((D_MODEL, tf), lambda i, f: (0, f)),
            pl.BlockSpec((tf, D_MODEL), lambda i, f: (f, 0)),
        ],
        out_specs=pl.BlockSpec((tm, D_MODEL), lambda i, f: (i, 0)),
        out_shape=jax.ShapeDtypeStruct((n, D_MODEL), F32),
        scratch_shapes=[pltpu.VMEM((tm, D_MODEL), BF16), pltpu.VMEM((tm, D_MODEL), F32)],
        compiler_params=_cparams(("parallel", "arbitrary")),
        name="ffn",
    )(x, g.reshape(1, D_MODEL), wg, wu, wd)


def _moe_kernel(x_ref, g_ref, rh_ref, rl_ref, wg_ref, wu_ref, wd_ref, o_ref,
                h_ref, gate_ref, acc_e_ref, acc_ref):
    e = pl.program_id(1)
    f = pl.program_id(2)
    nf = pl.num_programs(2)
    lane = lax.broadcasted_iota(I32, (1, LANES), 1)

    @pl.when((e == 0) & (f == 0))
    def _():
        x = x_ref[...]
        ms = jnp.mean(x * x, axis=-1, keepdims=True)
        hf = x * lax.rsqrt(ms + EPS) * g_ref[...]
        hb = hf.astype(BF16)
        h_ref[...] = hb
        hl = (hf - hb.astype(F32)).astype(BF16)
        logits = _dot(hb, rh_ref[...]) + _dot(hl, rh_ref[...]) + _dot(hb, rl_ref[...])
        valid = lane < N_EXPERTS
        logits = jnp.where(valid, logits, NEG_BIG)
        m = jnp.max(logits, axis=-1, keepdims=True)
        p = jnp.exp(logits - m)
        p = p / jnp.sum(p, axis=-1, keepdims=True)
        p = jnp.where(valid, p, -1.0)
        v1 = jnp.max(p, axis=-1, keepdims=True)
        i1 = jnp.min(jnp.where(p == v1, lane, LANES), axis=-1, keepdims=True)
        p2 = jnp.where(lane == i1, -1.0, p)
        v2 = jnp.max(p2, axis=-1, keepdims=True)
        i2 = jnp.min(jnp.where(p2 == v2, lane, LANES), axis=-1, keepdims=True)
        den = v1 + v2
        gate_ref[...] = jnp.where(lane == i1, v1 / den, jnp.where(lane == i2, v2 / den, 0.0))
        acc_ref[...] = x

    h = h_ref[...]
    a = _silu(_dot(h, wg_ref[0])) * _dot(h, wu_ref[0])
    y = _dot(a.astype(BF16), wd_ref[0])

    @pl.when(f == 0)
    def _():
        acc_e_ref[...] = y

    @pl.when(f > 0)
    def _():
        acc_e_ref[...] += y

    @pl.when(f == nf - 1)
    def _():
        gcol = jnp.sum(jnp.where(lane == e, gate_ref[...], 0.0), axis=-1, keepdims=True)
        acc_ref[...] += gcol * acc_e_ref[...]

    @pl.when((f == nf - 1) & (e == N_EXPERTS - 1))
    def _():
        o_ref[...] = acc_ref[...]


def _moe(x, g, r_hi, r_lo, wg, wu, wd, tf=512):
    n = x.shape[0]
    tm = min(512, n)
    return pl.pallas_call(
        _moe_kernel,
        grid=(n // tm, N_EXPERTS, D_FF // tf),
        in_specs=[
            pl.BlockSpec((tm, D_MODEL), lambda i, e, f: (i, 0)),
            pl.BlockSpec((1, D_MODEL), lambda i, e, f: (0, 0)),
            pl.BlockSpec((D_MODEL, LANES), lambda i, e, f: (0, 0)),
            pl.BlockSpec((D_MODEL, LANES), lambda i, e, f: (0, 0)),
            pl.BlockSpec((1, D_MODEL, tf), lambda i, e, f: (e, 0, f)),
            pl.BlockSpec((1, D_MODEL, tf), lambda i, e, f: (e, 0, f)),
            pl.BlockSpec((1, tf, D_MODEL), lambda i, e, f: (e, f, 0)),
        ],
        out_specs=pl.BlockSpec((tm, D_MODEL), lambda i, e, f: (i, 0)),
        out_shape=jax.ShapeDtypeStruct((n, D_MODEL), F32),
        scratch_shapes=[pltpu.VMEM((tm, D_MODEL), BF16), pltpu.VMEM((tm, LANES), F32),
                        pltpu.VMEM((tm, D_MODEL), F32), pltpu.VMEM((tm, D_MODEL), F32)],
        compiler_params=_cparams(("parallel", "arbitrary", "arbitrary")),
        name="moe",
    )(x, g.reshape(1, D_MODEL), r_hi, r_lo, wg, wu, wd)


def _ssd_kernel(xbc_ref, z_ref, dt_ref, cs_ref, h0_ref, cw_ref, cb_ref, dtb_ref, alog_ref,
                dexp_ref, og_ref, e_ref, y_ref, hout_ref, xpad_ref, h_ref, *, q):
    c = pl.program_id(1)

    @pl.when(c == 0)
    def _():
        xpad_ref[0:8, :] = cs_ref[0]
        h_ref[...] = h0_ref[0]

    xbc = xbc_ref[0]
    xpad_ref[8:8 + q, :] = xbc
    cw = cw_ref[...]
    conv = cb_ref[...] + (xpad_ref[5:5 + q, :] * cw[0:1] + xpad_ref[6:6 + q, :] * cw[1:2]
                          + xpad_ref[7:7 + q, :] * cw[2:3] + xbc * cw[3:4])
    xpad_ref[0:8, :] = xpad_ref[q:q + 8, :]
    act = _silu(conv)

    lane = lax.broadcasted_iota(I32, (1, LANES), 1)
    xdt = dt_ref[0] + dtb_ref[...]
    sp = jnp.maximum(xdt, 0.0) + jnp.log1p(jnp.exp(-jnp.abs(xdt)))
    dt = jnp.where(lane < SSD_HEADS, sp, 0.0)
    a = dt * (-jnp.exp(alog_ref[...]))

    rows = lax.broadcasted_iota(I32, (q, q), 0)
    cols = lax.broadcasted_iota(I32, (q, q), 1)
    causal = rows >= cols
    tril = jnp.where(causal, 1.0, 0.0).astype(BF16)
    a3 = _split3(a)
    acum = _dot(tril, a3[0]) + _dot(tril, a3[1]) + _dot(tril, a3[2])
    acum_t = jnp.transpose(acum)

    e_mat = e_ref[...]
    ac3 = _split3(acum)
    acum_x = _dot(ac3[0], e_mat) + _dot(ac3[1], e_mat) + _dot(ac3[2], e_mat)
    dt3 = _split3(dt)
    dt_x = _dot(dt3[0], e_mat) + _dot(dt3[1], e_mat) + _dot(dt3[2], e_mat)
    last = acum_x[q - 1:q, :]
    eac_x = jnp.exp(acum_x)
    dte_x = jnp.exp(last - acum_x)
    blkdec = jnp.exp(last)

    xs = act[:, :D_INNER]
    x_dt = xs * dt_x
    xb = x_dt.astype(BF16)
    xdb = (x_dt * dte_x).astype(BF16)
    z = z_ref[0]
    lane_lo = lane < SSD_HEADDIM

    gw = SSD_HPG * SSD_HEADDIM
    for g in range(SSD_GROUPS):
        bg = act[:, D_INNER + g * D_STATE:D_INNER + (g + 1) * D_STATE]
        cg = act[:, D_INNER + GN + g * D_STATE:D_INNER + GN + (g + 1) * D_STATE]
        cgb = cg.astype(BF16)
        bgt = jnp.transpose(bg).astype(BF16)
        cb = _dot(cgb, bgt)
        gs = slice(g * gw, (g + 1) * gw)
        h_in = h_ref[g]
        y_off = _dot(cgb, h_in.astype(BF16)) * eac_x[:, gs]
        h_ref[g] = h_in * blkdec[:, gs] + _dot(bgt, xdb[:, gs])
        parts = []
        for pr in range(SSD_HPG // 2):
            h0 = g * SSD_HPG + 2 * pr
            xp = xb[:, (h0 // 2) * LANES:(h0 // 2 + 1) * LANES]
            ys = []
            for hh in (h0, h0 + 1):
                seg = acum[:, hh:hh + 1] - acum_t[hh:hh + 1, :]
                lm = jnp.exp(jnp.where(causal, seg, NEG_BIG))
                ys.append(_dot((cb * lm).astype(BF16), xp))
            parts.append(jnp.where(lane_lo, ys[0], ys[1]))
        y_g = jnp.concatenate(parts, axis=1) + y_off + dexp_ref[:, gs] * xs[:, gs]
        gt = y_g * _silu(z[:, gs])
        gn = gt * lax.rsqrt(jnp.mean(gt * gt, axis=-1, keepdims=True) + EPS) * og_ref[:, gs]
        y_ref[0, :, gs] = gn.astype(y_ref.dtype)

    @pl.when(c == pl.num_programs(1) - 1)
    def _():
        hout_ref[0] = h_ref[...]


def _ssd(u, u_small, conv_state8, h0_t, cw8, cb, dtb, alog, dexp, og, e_mat, q):
    b, t, _ = u.shape
    kern = functools.partial(_ssd_kernel, q=q)
    full2 = lambda i, c: (0, 0)
    return pl.pallas_call(
        kern,
        grid=(b, t // q),
        in_specs=[
            pl.BlockSpec((1, q, CONV_DIM), lambda i, c: (i, c, 0)),
            pl.BlockSpec((1, q, D_INNER), lambda i, c: (i, c, 2)),
            pl.BlockSpec((1, q, LANES), lambda i, c: (i, c, 0)),
            pl.BlockSpec((1, 8, CONV_DIM), lambda i, c: (i, 0, 0)),
            pl.BlockSpec((1, SSD_GROUPS, D_STATE, SSD_HPG * SSD_HEADDIM), lambda i, c: (i, 0, 0, 0)),
            pl.BlockSpec((8, CONV_DIM), full2),
            pl.BlockSpec((1, CONV_DIM), full2),
            pl.BlockSpec((1, LANES), full2),
            pl.BlockSpec((1, LANES), full2),
            pl.BlockSpec((1, D_INNER), full2),
            pl.BlockSpec((1, D_INNER), full2),
            pl.BlockSpec((LANES, D_INNER), full2),
        ],
        out_specs=[
            pl.BlockSpec((1, q, D_INNER), lambda i, c: (i, c, 0)),
            pl.BlockSpec((1, SSD_GROUPS, D_STATE, SSD_HPG * SSD_HEADDIM), lambda i, c: (i, 0, 0, 0)),
        ],
        out_shape=[
            jax.ShapeDtypeStruct((b, t, D_INNER), BF16),
            jax.ShapeDtypeStruct((b, SSD_GROUPS, D_STATE, SSD_HPG * SSD_HEADDIM), F32),
        ],
        scratch_shapes=[pltpu.VMEM((q + 8, CONV_DIM), F32),
                        pltpu.VMEM((SSD_GROUPS, D_STATE, SSD_HPG * SSD_HEADDIM), F32)],
        compiler_params=_cparams(("parallel", "arbitrary")),
        name="ssd",
    )(u, u, u_small, conv_state8, h0_t, cw8, cb, dtb, alog, dexp, og, e_mat)


def _dsa_prep_kernel(q_ref, k_ref, qi_ref, sm_ref, cos_ref, sin_ref, qg_ref, kg_ref,
                     qo_ref, ko_ref, qio_ref, kio_ref):
    cos = cos_ref[...]
    sin = sin_ref[...]
    lane = lax.broadcasted_iota(I32, (1, LANES), 1)

    def head_norm(x):
        s = x * x
        for sh in (1, 2, 4, 8, 16, 32):
            s = s + jnp.where((lane & sh) == 0, pltpu.roll(s, LANES - sh, 1), pltpu.roll(s, sh, 1))
        return x * lax.rsqrt(s * (1.0 / ATT_HD) + EPS)

    def rope(x):
        partner = jnp.where((lane & 32) == 0, pltpu.roll(x, LANES - 32, 1), pltpu.roll(x, 32, 1))
        return x * cos + partner * sin

    for c in range(ATT_Q // LANES):
        sl = slice(c * LANES, (c + 1) * LANES)
        qo_ref[0, :, sl] = rope(head_norm(q_ref[0, :, sl]) * qg_ref[...]).astype(qo_ref.dtype)
    for c in range(ATT_KV // LANES):
        sl = slice(c * LANES, (c + 1) * LANES)
        ko_ref[0, :, sl] = rope(head_norm(k_ref[0, :, sl]) * kg_ref[...])
    for c in range(IDX_Q // LANES):
        sl = slice(c * LANES, (c + 1) * LANES)
        qio_ref[0, :, sl] = rope(qi_ref[0, :, sl]).astype(qio_ref.dtype)
    kio_ref[0] = rope(sm_ref[0])


def _dsa_prep(u, u_small, cos, sin, qg, kg):
    b, t, _ = u.shape
    tr = min(512, t)
    full2 = lambda i, j: (0, 0)
    return pl.pallas_call(
        _dsa_prep_kernel,
        grid=(b, t // tr),
        in_specs=[
            pl.BlockSpec((1, tr, ATT_Q), lambda i, j: (i, j, 0)),
            pl.BlockSpec((1, tr, ATT_KV), lambda i, j: (i, j, 4)),
            pl.BlockSpec((1, tr, IDX_Q), lambda i, j: (i, j, 3)),
            pl.BlockSpec((1, tr, LANES), lambda i, j: (i, j, 0)),
            pl.BlockSpec((tr, LANES), lambda i, j: (j, 0)),
            pl.BlockSpec((tr, LANES), lambda i, j: (j, 0)),
            pl.BlockSpec((1, LANES), full2),
            pl.BlockSpec((1, LANES), full2),
        ],
        out_specs=[
            pl.BlockSpec((1, tr, ATT_Q), lambda i, j: (i, j, 0)),
            pl.BlockSpec((1, tr, ATT_KV), lambda i, j: (i, j, 0)),
            pl.BlockSpec((1, tr, IDX_Q), lambda i, j: (i, j, 0)),
            pl.BlockSpec((1, tr, LANES), lambda i, j: (i, j, 0)),
        ],
        out_shape=[
            jax.ShapeDtypeStruct((b, t, ATT_Q), BF16),
            jax.ShapeDtypeStruct((b, t, ATT_KV), F32),
            jax.ShapeDtypeStruct((b, t, IDX_Q), BF16),
            jax.ShapeDtypeStruct((b, t, LANES), F32),
        ],
        compiler_params=_cparams(("parallel", "parallel")),
        name="dsa_prep",
    )(u, u, u, u_small, cos, sin, qg, kg)


def _dsa_kernel(q_ref, qi_ref, wi_ref, kt_ref, v_ref, kit_ref, o_ref, key_ref, bias_ref, mm_ref,
                *, tq, lp, l_keys, start, nsel):
    i = pl.program_id(1)
    kit = kit_ref[0]
    score = None
    for h in range(IDX_HEADS):
        d = _dot(qi_ref[0, h], kit)
        t = jnp.maximum(d, 0.0) * (wi_ref[0, :, IDX_HD + h:IDX_HD + h + 1] * IDX_SCALE)
        score = t if score is None else score + t

    row = lax.broadcasted_iota(I32, (tq, 1), 0)
    pos = start + i * tq + row
    limit = jnp.minimum((lax.shift_right_logical(pos, 6) + 1) * CHUNK, l_keys)
    kpos = lax.broadcasted_iota(I32, (tq, lp), 1)
    adm = kpos < limit

    bits = pltpu.bitcast(score, I32)
    key = jnp.where(bits < 0, bits ^ 0x7FFFFFFF, bits)
    key = jnp.where(score == 0.0, 0, key)
    key_ref[...] = jnp.where(adm, key, INT_MIN)

    def search(it, lo):
        cand = lo + lax.shift_left(jnp.int32(1), 31 - it)
        cnt = jnp.sum(jnp.where(key_ref[...] >= cand, 1.0, 0.0), axis=1, keepdims=True)
        return jnp.where(cnt >= nsel, cand, lo)

    thr = lax.fori_loop(0, 32, search, jnp.full((tq, 1), INT_MIN, I32))

    key = key_ref[...]
    c_gt = jnp.sum(jnp.where(key > thr, 1.0, 0.0), axis=1, keepdims=True)
    n_eq = jnp.sum(jnp.where(key == thr, 1.0, 0.0), axis=1, keepdims=True)
    need = nsel - c_gt
    excess = jnp.where((n_eq > need) & (thr > INT_MIN), 1.0, 0.0)
    mm_ref[...] = jnp.full((tq, 1), lp, I32)

    @pl.when(jnp.max(excess) > 0.0)
    def _():
        def tie_search(it, m):
            cand = m + lax.shift_left(jnp.int32(1), 13 - it)
            tied = (key_ref[...] == thr) & (kpos < cand)
            f = jnp.sum(jnp.where(tied, 1.0, 0.0), axis=1, keepdims=True)
            return jnp.where(f < need, cand, m)

        mm_ref[...] = lax.fori_loop(0, 14, tie_search, jnp.zeros((tq, 1), I32))

    mm = mm_ref[...]
    sel = ((key > thr) | ((key == thr) & (kpos <= mm))) & adm
    bias_ref[...] = jnp.where(sel, 0.0, NEG_BIG)

    for j in range(KV_HEADS):
        qj = q_ref[0, ATT_GROUPS * j:ATT_GROUPS * (j + 1)].reshape(ATT_GROUPS * tq, ATT_HD)
        s = _dot(qj, kt_ref[0, j * ATT_HD:(j + 1) * ATT_HD, :]) * (ATT_HD ** -0.5)
        s = s.reshape(ATT_GROUPS, tq, lp) + bias_ref[...][None]
        m = jnp.max(s, axis=-1, keepdims=True)
        p = jnp.exp(s - m)
        den = jnp.sum(p, axis=-1, keepdims=True)
        o = _dot(p.reshape(ATT_GROUPS * tq, lp).astype(BF16), v_ref[0])
        o = o[:, j * ATT_HD:(j + 1) * ATT_HD].reshape(ATT_GROUPS, tq, ATT_HD) / den
        o_ref[0, ATT_GROUPS * j:ATT_GROUPS * (j + 1)] = o.astype(o_ref.dtype)


def _dsa(q_h, qi_h, wi, kt, v, kit, *, tq, l_keys, start):
    b, _, t, _ = q_h.shape
    lp = kt.shape[2]
    nsel = min(TOPK_MAX, l_keys // 4)
    kern = functools.partial(_dsa_kernel, tq=tq, lp=lp, l_keys=l_keys, start=start, nsel=float(nsel))
    return pl.pallas_call(
        kern,
        grid=(b, t // tq),
        in_specs=[
            pl.BlockSpec((1, ATT_HEADS, tq, ATT_HD), lambda i, j: (i, 0, j, 0)),
            pl.BlockSpec((1, IDX_HEADS, tq, IDX_HD), lambda i, j: (i, 0, j, 0)),
            pl.BlockSpec((1, tq, LANES), lambda i, j: (i, j, 0)),
            pl.BlockSpec((1, ATT_KV, lp), lambda i, j: (i, 0, 0)),
            pl.BlockSpec((1, lp, ATT_KV), lambda i, j: (i, 0, 0)),
            pl.BlockSpec((1, IDX_HD, lp), lambda i, j: (i, 0, 0)),
        ],
        out_specs=pl.BlockSpec((1, ATT_HEADS, tq, ATT_HD), lambda i, j: (i, 0, j, 0)),
        out_shape=jax.ShapeDtypeStruct((b, ATT_HEADS, t, ATT_HD), BF16),
        scratch_shapes=[pltpu.VMEM((tq, lp), I32), pltpu.VMEM((tq, lp), F32), pltpu.VMEM((tq, 1), I32)],
        compiler_params=_cparams(("parallel", "arbitrary")),
        name="dsa",
    )(q_h, qi_h, wi, kt, v, kit)


def _prep_weights(ssd_w_in, att_w_in, ssd_w_out, att_w_out, mem_w_kv, dense_w_gate, dense_w_up,
                  dense_w_down, moe_router, moe_w_gate, moe_w_up, moe_w_down):
    bf = lambda w: w.astype(BF16)
    w = ssd_w_in[0]
    o_xbc, o_dt, o_mq = D_INNER, D_INNER + CONV_DIM, D_INNER + CONV_DIM + SSD_HEADS
    ssd_main = bf(jnp.concatenate([w[:, o_xbc:o_dt], w[:, o_mq:], w[:, :D_INNER]], axis=1))
    ssd_small = bf(jnp.pad(w[:, o_dt:o_mq], ((0, 0), (0, LANES - SSD_HEADS))))
    w = att_w_in[0]
    o_wi = ATT_Q + 2 * ATT_KV + IDX_Q
    o_ki = o_wi + IDX_HEADS
    o_mq = o_ki + IDX_HD
    att_main = bf(jnp.concatenate([w[:, :o_wi], w[:, o_mq:]], axis=1))
    att_small = bf(jnp.pad(jnp.concatenate([w[:, o_ki:o_mq], w[:, o_wi:o_ki]], axis=1),
                           ((0, 0), (0, LANES - IDX_HD - IDX_HEADS))))
    r = jnp.pad(moe_router[0], ((0, 0), (0, LANES - N_EXPERTS)))
    r_hi = bf(r)
    r_lo = bf(r - r_hi.astype(F32))
    return dict(
        ssd_main=ssd_main, ssd_small=ssd_small, att_main=att_main, att_small=att_small,
        ssd_out_a=bf(ssd_w_out[0, :D_INNER]), ssd_out_b=bf(ssd_w_out[0, D_INNER:]),
        att_out_a=bf(att_w_out[0, :ATT_Q]), att_out_b=bf(att_w_out[0, ATT_Q:]),
        mem_w_kv=bf(mem_w_kv), dense_g=bf(dense_w_gate[0]), dense_u=bf(dense_w_up[0]),
        dense_d=bf(dense_w_down[0]), r_hi=r_hi, r_lo=r_lo,
        moe_g=bf(moe_w_gate[0]), moe_u=bf(moe_w_up[0]), moe_d=bf(moe_w_down[0]))


def _rope_tables(pos):
    half = ATT_HD // 2
    inv = ROPE_THETA ** (-jnp.arange(half, dtype=F32) / half)
    ang = pos.astype(F32)[:, None] * inv[None, :]
    cos = jnp.cos(ang)
    sin = jnp.sin(ang)
    cos_t = jnp.concatenate([cos, cos, cos, cos], axis=1)
    sin_t = jnp.concatenate([-sin, sin, -sin, sin], axis=1)
    return cos_t, sin_t


def _trunk(x, start, mem_k, mem_v, conv_in, ssm_in, kv_in, P, W, q_ssd, tq_dsa):
    b, t, _ = x.shape
    n = b * t
    x2 = x.reshape(n, D_MODEL)

    u = _norm_matmul(x2, W['ssd_norm'][0], P['ssd_main'], tn=1024).reshape(b, t, -1)
    u_small = _norm_matmul(x2, W['ssd_norm'][0], P['ssd_small'], tn=LANES).reshape(b, t, LANES)
    conv8 = jnp.pad(conv_in, ((0, 0), (8 - (CONV_W - 1), 0), (0, 0)))
    h0_t = ssm_in.reshape(b, SSD_GROUPS, SSD_HPG, SSD_HEADDIM, D_STATE).transpose(0, 1, 4, 2, 3)
    h0_t = h0_t.reshape(b, SSD_GROUPS, D_STATE, SSD_HPG * SSD_HEADDIM)
    cw8 = jnp.pad(W['ssd_conv_w'][0], ((0, 8 - CONV_W), (0, 0)))
    pad_h = lambda v: jnp.pad(v.astype(F32), (0, LANES - SSD_HEADS)).reshape(1, LANES)
    e_mat = (jnp.arange(LANES)[:, None] == (jnp.arange(D_INNER)[None, :] // SSD_HEADDIM)).astype(BF16)
    y_mix, h_t = _ssd(u, u_small, conv8, h0_t, cw8, W['ssd_conv_b'][0].reshape(1, CONV_DIM),
                      pad_h(W['ssd_dt_bias'][0]), pad_h(W['ssd_A_log'][0]),
                      jnp.repeat(W['ssd_D'][0].astype(F32), SSD_HEADDIM).reshape(1, D_INNER),
                      W['ssd_out_norm'][0].reshape(1, D_INNER), e_mat, q_ssd)
    new_conv = u[:, t - (CONV_W - 1):, :CONV_DIM]
    new_ssm = h_t.reshape(b, SSD_GROUPS, D_STATE, SSD_HPG, SSD_HEADDIM).transpose(0, 1, 3, 4, 2)
    new_ssm = new_ssm.reshape(b, SSD_HEADS, SSD_HEADDIM, D_STATE)
    y_mem = _mem_attn(u, 3, mem_k[0], mem_v[0], W['mem_q_norm'][0])
    x2 = _proj_res(y_mix.reshape(n, D_INNER), y_mem.reshape(n, MEM_WIDTH), P['ssd_out_a'], P['ssd_out_b'], x2)
    x2 = _ffn(x2, W['ffn_norm'][0], P['dense_g'], P['dense_u'], P['dense_d'])

    u = _norm_matmul(x2, W['att_norm'][0], P['att_main'], tn=1024).reshape(b, t, -1)
    u_small = _norm_matmul(x2, W['att_norm'][0], P['att_small'], tn=LANES).reshape(b, t, LANES)
    pos = start + jnp.arange(t)
    cos_t, sin_t = _rope_tables(pos)
    tile2 = lambda v: jnp.tile(v.astype(F32), 2).reshape(1, LANES)
    q_rot, k_rot, qi_rot, ki_rot = _dsa_prep(u, u_small, cos_t, sin_t, tile2(W['att_q_norm'][0]),
                                             tile2(W['att_k_norm'][0]))
    v_new = u[:, :, ATT_Q + ATT_KV:ATT_Q + 2 * ATT_KV]
    ki_new = ki_rot[:, :, :IDX_HD]
    if kv_in is None:
        k_all, v_all, ki_all = k_rot, v_new, ki_new
    else:
        k_all = jnp.concatenate([kv_in[0], k_rot], axis=1)
        v_all = jnp.concatenate([kv_in[1], v_new], axis=1)
        ki_all = jnp.concatenate([kv_in[2], ki_new], axis=1)
    l_keys = k_all.shape[1]
    lp = -(-l_keys // LANES) * LANES
    padl = lambda a: jnp.pad(a, ((0, 0), (0, lp - l_keys), (0, 0)))
    kt = padl(k_all).astype(BF16).transpose(0, 2, 1)
    kit = padl(ki_all).astype(BF16).transpose(0, 2, 1)
    v_b = padl(v_all).astype(BF16)
    q_h = q_rot.reshape(b, t, ATT_HEADS, ATT_HD).transpose(0, 2, 1, 3)
    qi_h = qi_rot.reshape(b, t, IDX_HEADS, IDX_HD).transpose(0, 2, 1, 3)
    o_h = _dsa(q_h, qi_h, u_small, kt, v_b, kit, tq=tq_dsa, l_keys=l_keys, start=start)
    y_mix = o_h.transpose(0, 2, 1, 3).reshape(n, ATT_Q)
    y_mem = _mem_attn(u, 2, mem_k[1], mem_v[1], W['mem_q_norm'][1])
    x2 = _proj_res(y_mix, y_mem.reshape(n, MEM_WIDTH), P['att_out_a'], P['att_out_b'], x2)
    x2 = _moe(x2, W['ffn_norm'][1], P['r_hi'], P['r_lo'], P['moe_g'], P['moe_u'], P['moe_d'])

    return (x2.reshape(b, t, D_MODEL), new_conv, new_ssm, k_rot.reshape(b, t, KV_HEADS, ATT_HD),
            v_new.reshape(b, t, KV_HEADS, ATT_HD), ki_new)


def kernel(x_prompt, x_sample, mem_prompt, cache_conv, state_ssm, cache_k, cache_v, cache_idx_k, cache_mem_k, cache_mem_v, ssd_norm, ssd_w_in, ssd_conv_w, ssd_conv_b, ssd_dt_bias, ssd_A_log, ssd_D, ssd_out_norm, ssd_w_out, att_norm, att_w_in, att_q_norm, att_k_norm, att_w_out, mem_norm, mem_w_kv, mem_q_norm, mem_k_norm, ffn_norm, dense_w_gate, dense_w_up, dense_w_down, moe_router, moe_w_gate, moe_w_up, moe_w_down):
    W = dict(ssd_norm=ssd_norm, ssd_conv_w=ssd_conv_w, ssd_conv_b=ssd_conv_b, ssd_dt_bias=ssd_dt_bias,
             ssd_A_log=ssd_A_log, ssd_D=ssd_D, ssd_out_norm=ssd_out_norm, att_norm=att_norm,
             att_q_norm=att_q_norm, att_k_norm=att_k_norm, mem_q_norm=mem_q_norm, ffn_norm=ffn_norm)
    P = _prep_weights(ssd_w_in, att_w_in, ssd_w_out, att_w_out, mem_w_kv, dense_w_gate, dense_w_up,
                      dense_w_down, moe_router, moe_w_gate, moe_w_up, moe_w_down)
    bp, sp = x_prompt.shape[0], x_prompt.shape[1]
    bs = x_sample.shape[0]

    mem2 = mem_prompt.reshape(bp * N_MEM, D_MODEL)
    pk, pv = [], []
    for i in range(2):
        kv = _norm_matmul(mem2, mem_norm[i], P['mem_w_kv'][i], tn=1024)
        pk.append(_head_norm(kv[:, :MEM_WIDTH], mem_k_norm[i]).reshape(bp, N_MEM, MEM_WIDTH))
        pv.append(kv[:, MEM_WIDTH:].reshape(bp, N_MEM, MEM_WIDTH))
    p_mem_k = jnp.stack(pk)
    p_mem_v = jnp.stack(pv)

    conv0 = jnp.zeros((bp, CONV_W - 1, CONV_DIM), F32)
    ssm0 = jnp.zeros((bp, SSD_HEADS, SSD_HEADDIM, D_STATE), F32)
    y_p, p_conv, p_ssm, p_k, p_v, p_ki = _trunk(x_prompt, 0, p_mem_k, p_mem_v, conv0, ssm0, None, P, W,
                                                q_ssd=128, tq_dsa=128)
    past = cache_k.shape[2]
    kv_in = (cache_k[0].reshape(bs, past, ATT_KV), cache_v[0].reshape(bs, past, ATT_KV), cache_idx_k[0])
    y_s, s_conv, s_ssm, s_k, s_v, s_ki = _trunk(
        x_sample, past, cache_mem_k.reshape(2, bs, N_MEM, MEM_WIDTH), cache_mem_v.reshape(2, bs, N_MEM, MEM_WIDTH),
        cache_conv[0], state_ssm[0], kv_in, P, W, q_ssd=x_sample.shape[1], tq_dsa=x_sample.shape[1])

    shp = (bp, N_MEM, MEM_HEADS, MEM_HD)
    return (y_p, y_s, p_conv[None], p_ssm[None], p_k[None], p_v[None], p_ki[None],
            p_mem_k.reshape((2,) + shp), p_mem_v.reshape((2,) + shp),
            s_conv[None], s_ssm[None], s_k[None], s_v[None], s_ki[None])
```

```python
import functools
import math

import jax
import jax.numpy as jnp
from jax import lax
from jax.experimental import pallas as pl
from jax.experimental.pallas import tpu as pltpu

F32 = jnp.float32
BF16 = jnp.bfloat16
I32 = jnp.int32

D_MODEL = 1024
CHUNK = 64
N_MEM = 256
EPS = 1e-6
D_INNER = 2048
SSD_HEADDIM = 64
SSD_HEADS = 32
SSD_GROUPS = 4
SSD_HPG = 8
D_STATE = 128
CONV_W = 4
GN = SSD_GROUPS * D_STATE
CONV_DIM = D_INNER + 2 * GN
ATT_HEADS = 16
ATT_HD = 64
KV_HEADS = 4
ATT_GROUPS = 4
ATT_Q = 1024
ATT_KV = 256
IDX_HEADS = 8
IDX_HD = 64
IDX_Q = 512
IDX_SCALE = (IDX_HEADS * IDX_HD) ** -0.5
TOPK_MAX = 256
ROPE_THETA = 10000.0
MEM_HEADS = 4
MEM_HD = 256
MEM_WIDTH = 1024
D_FF = 3584
N_EXPERTS = 8

LANES = 128
INT_MIN = -(2 ** 31)
NEG_BIG = -1e30
VMEM_LIMIT = 56 * 1024 * 1024


def _cparams(sem):
    return pltpu.CompilerParams(dimension_semantics=sem, vmem_limit_bytes=VMEM_LIMIT)


def _dot(a, b):
    return jnp.dot(a, b, preferred_element_type=F32)


def _split3(v):
    hi = v.astype(BF16)
    r1 = v - hi.astype(F32)
    mid = r1.astype(BF16)
    lo = (r1 - mid.astype(F32)).astype(BF16)
    return hi, mid, lo


def _silu(x):
    return x / (1.0 + jnp.exp(-x))


def _norm_matmul_kernel(x_ref, g_ref, w_ref, o_ref, h_ref):
    @pl.when(pl.program_id(1) == 0)
    def _():
        x = x_ref[...]
        ms = jnp.mean(x * x, axis=-1, keepdims=True)
        h_ref[...] = (x * lax.rsqrt(ms + EPS) * g_ref[...]).astype(BF16)

    o_ref[...] = _dot(h_ref[...], w_ref[...]).astype(o_ref.dtype)


def _norm_matmul(x, g, w, tn, out_dtype=F32):
    n, k = x.shape
    m = w.shape[1]
    tm = min(512, n)
    return pl.pallas_call(
        _norm_matmul_kernel,
        grid=(n // tm, m // tn),
        in_specs=[
            pl.BlockSpec((tm, k), lambda i, j: (i, 0)),
            pl.BlockSpec((1, k), lambda i, j: (0, 0)),
            pl.BlockSpec((k, tn), lambda i, j: (0, j)),
        ],
        out_specs=pl.BlockSpec((tm, tn), lambda i, j: (i, j)),
        out_shape=jax.ShapeDtypeStruct((n, m), out_dtype),
        scratch_shapes=[pltpu.VMEM((tm, k), BF16)],
        compiler_params=_cparams(("parallel", "arbitrary")),
        name="norm_matmul",
    )(x, g.reshape(1, k), w)


def _head_norm_kernel(x_ref, g_ref, o_ref):
    for h in range(MEM_HEADS):
        x = x_ref[:, h * MEM_HD:(h + 1) * MEM_HD]
        ms = jnp.mean(x * x, axis=-1, keepdims=True)
        o_ref[:, h * MEM_HD:(h + 1) * MEM_HD] = x * lax.rsqrt(ms + EPS) * g_ref[...]


def _head_norm(x, g):
    n = x.shape[0]
    tm = min(512, n)
    return pl.pallas_call(
        _head_norm_kernel,
        grid=(n // tm,),
        in_specs=[pl.BlockSpec((tm, MEM_WIDTH), lambda i: (i, 0)),
                  pl.BlockSpec((1, MEM_HD), lambda i: (0, 0))],
        out_specs=pl.BlockSpec((tm, MEM_WIDTH), lambda i: (i, 0)),
        out_shape=jax.ShapeDtypeStruct((n, MEM_WIDTH), F32),
        compiler_params=_cparams(("parallel",)),
        name="head_norm",
    )(x, g.reshape(1, MEM_HD))


def _mem_attn_kernel(q_ref, k_ref, v_ref, g_ref, o_ref):
    for h in range(MEM_HEADS):
        sl = slice(h * MEM_HD, (h + 1) * MEM_HD)
        q = q_ref[0, :, sl]
        ms = jnp.mean(q * q, axis=-1, keepdims=True)
        qn = (q * lax.rsqrt(ms + EPS) * g_ref[...]).astype(BF16)
        k = k_ref[0, :, sl].astype(BF16)
        s = lax.dot_general(qn, k, (((1,), (1,)), ((), ())), preferred_element_type=F32)
        s = s * (MEM_HD ** -0.5)
        m = jnp.max(s, axis=-1, keepdims=True)
        p = jnp.exp(s - m)
        p = p / jnp.sum(p, axis=-1, keepdims=True)
        o = _dot(p.astype(BF16), v_ref[0, :, sl].astype(BF16))
        o_ref[0, :, sl] = o.astype(o_ref.dtype)


def _mem_attn(u, col_block, mk, mv, g):
    b, t, _ = u.shape
    tq = min(512, t)
    return pl.pallas_call(
        _mem_attn_kernel,
        grid=(b, t // tq),
        in_specs=[
            pl.BlockSpec((1, tq, MEM_WIDTH), lambda i, j: (i, j, col_block)),
            pl.BlockSpec((1, N_MEM, MEM_WIDTH), lambda i, j: (i, 0, 0)),
            pl.BlockSpec((1, N_MEM, MEM_WIDTH), lambda i, j: (i, 0, 0)),
            pl.BlockSpec((1, MEM_HD), lambda i, j: (0, 0)),
        ],
        out_specs=pl.BlockSpec((1, tq, MEM_WIDTH), lambda i, j: (i, j, 0)),
        out_shape=jax.ShapeDtypeStruct((b, t, MEM_WIDTH), BF16),
        compiler_params=_cparams(("parallel", "parallel")),
        name="mem_attn",
    )(u, mk, mv, g.reshape(1, MEM_HD))


def _proj_res_kernel(a_ref, b_ref, wa_ref, wb_ref, x_ref, o_ref):
    o_ref[...] = x_ref[...] + _dot(a_ref[...], wa_ref[...]) + _dot(b_ref[...], wb_ref[...])


def _proj_res(a, b, wa, wb, x):
    n = x.shape[0]
    tm = min(512, n)
    ka, kb = a.shape[1], b.shape[1]
    return pl.pallas_call(
        _proj_res_kernel,
        grid=(n // tm,),
        in_specs=[
            pl.BlockSpec((tm, ka), lambda i: (i, 0)),
            pl.BlockSpec((tm, kb), lambda i: (i, 0)),
            pl.BlockSpec((ka, D_MODEL), lambda i: (0, 0)),
            pl.BlockSpec((kb, D_MODEL), lambda i: (0, 0)),
            pl.BlockSpec((tm, D_MODEL), lambda i: (i, 0)),
        ],
        out_specs=pl.BlockSpec((tm, D_MODEL), lambda i: (i, 0)),
        out_shape=jax.ShapeDtypeStruct((n, D_MODEL), F32),
        compiler_params=_cparams(("parallel",)),
        name="proj_res",
    )(a, b, wa, wb, x)


def _ffn_kernel(x_ref, g_ref, wg_ref, wu_ref, wd_ref, o_ref, h_ref, acc_ref):
    f = pl.program_id(1)

    @pl.when(f == 0)
    def _():
        x = x_ref[...]
        ms = jnp.mean(x * x, axis=-1, keepdims=True)
        h_ref[...] = (x * lax.rsqrt(ms + EPS) * g_ref[...]).astype(BF16)
        acc_ref[...] = x

    h = h_ref[...]
    a = _silu(_dot(h, wg_ref[...])) * _dot(h, wu_ref[...])
    acc_ref[...] += _dot(a.astype(BF16), wd_ref[...])

    @pl.when(f == pl.num_programs(1) - 1)
    def _():
        o_ref[...] = acc_ref[...]


def _ffn(x, g, wg, wu, wd, tf=512):
    n = x.shape[0]
    tm = min(512, n)
    return pl.pallas_call(
        _ffn_kernel,
        grid=(n // tm, D_FF // tf),
        in_specs=[
            pl.BlockSpec((tm, D_MODEL), lambda i, f: (i, 0)),
            pl.BlockSpec((1, D_MODEL), lambda i, f: (0, 0)),
            pl.BlockSpec((D_MODEL, tf), lambda i, f: (0, f)),
            pl.BlockSpec((D_MODEL, tf), lambda i, f: (0, f)),
            pl.BlockSpec((tf, D_MODEL), lambda i, f: (f, 0)),
        ],
        out_specs=pl.BlockSpec((tm, D_MODEL), lambda i, f: (i, 0)),
        out_shape=jax.ShapeDtypeStruct((n, D_MODEL), F32),
        scratch_shapes=[pltpu.VMEM((tm, D_MODEL), BF16), pltpu.VMEM((tm, D_MODEL), F32)],
        compiler_params=_cparams(("parallel", "arbitrary")),
        name="ffn",
    )(x, g.reshape(1, D_MODEL), wg, wu, wd)


MOE_ROWS = 256


def _split2(v):
    hi = v.astype(BF16)
    return hi, (v - hi.astype(F32)).astype(BF16)


def _moe_route_kernel(x_ref, g_ref, rh_ref, rl_ref, h_ref, gate_ref):
    lane = lax.broadcasted_iota(I32, (1, LANES), 1)
    x = x_ref[...]
    ms = jnp.mean(x * x, axis=-1, keepdims=True)
    hf = x * lax.rsqrt(ms + EPS) * g_ref[...]
    hb, hl = _split2(hf)
    h_ref[...] = hb
    logits = _dot(hb, rh_ref[...]) + _dot(hl, rh_ref[...]) + _dot(hb, rl_ref[...])
    valid = lane < N_EXPERTS
    logits = jnp.where(valid, logits, NEG_BIG)
    m = jnp.max(logits, axis=-1, keepdims=True)
    p = jnp.exp(logits - m)
    p = p / jnp.sum(p, axis=-1, keepdims=True)
    p = jnp.where(valid, p, -1.0)
    v1 = jnp.max(p, axis=-1, keepdims=True)
    i1 = jnp.min(jnp.where(p == v1, lane, LANES), axis=-1, keepdims=True)
    p2 = jnp.where(lane == i1, -1.0, p)
    v2 = jnp.max(p2, axis=-1, keepdims=True)
    i2 = jnp.min(jnp.where(p2 == v2, lane, LANES), axis=-1, keepdims=True)
    den = v1 + v2
    gate_ref[...] = jnp.where(lane == i1, v1 / den, jnp.where(lane == i2, v2 / den, 0.0))


def _moe_route(x, g, r_hi, r_lo):
    n = x.shape[0]
    tm = min(512, n)
    return pl.pallas_call(
        _moe_route_kernel,
        grid=(n // tm,),
        in_specs=[pl.BlockSpec((tm, D_MODEL), lambda i: (i, 0)),
                  pl.BlockSpec((1, D_MODEL), lambda i: (0, 0)),
                  pl.BlockSpec((D_MODEL, LANES), lambda i: (0, 0)),
                  pl.BlockSpec((D_MODEL, LANES), lambda i: (0, 0))],
        out_specs=[pl.BlockSpec((tm, D_MODEL), lambda i: (i, 0)),
                   pl.BlockSpec((tm, LANES), lambda i: (i, 0))],
        out_shape=[jax.ShapeDtypeStruct((n, D_MODEL), BF16), jax.ShapeDtypeStruct((n, LANES), F32)],
        compiler_params=_cparams(("parallel",)),
        name="moe_route",
    )(x, g.reshape(1, D_MODEL), r_hi, r_lo)


def _moe_ffn_kernel(h_ref, gate_ref, x_ref, wg_ref, wu_ref, wd_ref, o_ref,
                    rank_ref, rank_t_ref, sel_t_ref, gate_t_ref, xe_ref, acc_ref, nb_ref, *, tm):
    e = pl.program_id(1)
    f = pl.program_id(2)
    nf = pl.num_programs(2)
    big, small = MOE_ROWS, MOE_ROWS // 2
    lane = lax.broadcasted_iota(I32, (1, LANES), 1)

    @pl.when((e == 0) & (f == 0))
    def _():
        o_ref[...] = x_ref[...]
        gates = gate_ref[...]
        sel = jnp.where(gates > 0.0, 1.0, 0.0)
        r = lax.broadcasted_iota(I32, (tm, tm), 0)
        c = lax.broadcasted_iota(I32, (tm, tm), 1)
        below = jnp.where(c < r, 1.0, 0.0).astype(BF16)
        rank = _dot(below, sel.astype(BF16))
        rank_ref[...] = rank
        rank_t_ref[...] = jnp.transpose(rank)
        sel_t_ref[...] = jnp.transpose(sel)
        gate_t_ref[...] = jnp.transpose(gates)

    @pl.when(f == 0)
    def _():
        cnt = jnp.sum(jnp.where(lane == e, jnp.sum(jnp.where(gate_ref[...] > 0.0, 1.0, 0.0), axis=0, keepdims=True),
                                0.0)).astype(I32)
        n_big = lax.div(cnt, big)
        rem = cnt - n_big * big
        nb_ref[0] = n_big + jnp.where(rem > small, 1, 0)
        nb_ref[1] = jnp.where((rem > 0) & (rem <= small), 1, 0)

    n_big = nb_ref[0]
    has_small = nb_ref[1] > 0
    r_small = pl.multiple_of(n_big * big, big)

    def pick(r0, rows):
        want = (lax.broadcasted_iota(I32, (rows, 1), 0) + r0).astype(F32)
        return (rank_t_ref[pl.ds(e, 1), :] == want) & (sel_t_ref[pl.ds(e, 1), :] > 0.0)

    def for_blocks(fn):
        def body(rb, carry):
            fn(pl.multiple_of(rb * big, big), big)
            return carry

        lax.fori_loop(0, n_big, body, 0)

        @pl.when(has_small)
        def _():
            fn(r_small, small)

    def gather(r0, rows):
        p = jnp.where(pick(r0, rows), 1.0, 0.0).astype(BF16)
        xe_ref[pl.ds(r0, rows), :] = _dot(p, h_ref[...]).astype(BF16)

    def ffn(r0, rows):
        xe = xe_ref[pl.ds(r0, rows), :]
        a = _silu(_dot(xe, wg_ref[0])) * _dot(xe, wu_ref[0])
        y = _dot(a.astype(BF16), wd_ref[0])

        @pl.when(f == 0)
        def _():
            acc_ref[pl.ds(r0, rows), :] = y

        @pl.when(f > 0)
        def _():
            acc_ref[pl.ds(r0, rows), :] += y

    @pl.when(f == 0)
    def _():
        for_blocks(gather)

    for_blocks(ffn)

    @pl.when(f == nf - 1)
    def _():
        rank_col = jnp.sum(jnp.where(lane == e, rank_ref[...], 0.0), axis=1, keepdims=True)
        sel_col = jnp.sum(jnp.where(lane == e, gate_ref[...], 0.0), axis=1, keepdims=True) > 0.0

        def scatter(r0, rows):
            g_rows = jnp.sum(jnp.where(pick(r0, rows), gate_t_ref[pl.ds(e, 1), :], 0.0), axis=1, keepdims=True)
            y_hi, y_lo = _split2(acc_ref[pl.ds(r0, rows), :] * g_rows)
            col = (lax.broadcasted_iota(I32, (1, rows), 1) + r0).astype(F32)
            put = jnp.where((rank_col == col) & sel_col, 1.0, 0.0).astype(BF16)
            o_ref[...] += _dot(put, y_hi) + _dot(put, y_lo)

        for_blocks(scatter)


def _moe_ffn(h, gates, x, wg, wu, wd, *, tm=1024, tf=512):
    n = x.shape[0]
    tm = min(tm, n)
    kern = functools.partial(_moe_ffn_kernel, tm=tm)
    return pl.pallas_call(
        kern,
        grid=(n // tm, N_EXPERTS, D_FF // tf),
        in_specs=[
            pl.BlockSpec((tm, D_MODEL), lambda i, e, f: (i, 0)),
            pl.BlockSpec((tm, LANES), lambda i, e, f: (i, 0)),
            pl.BlockSpec((tm, D_MODEL), lambda i, e, f: (i, 0)),
            pl.BlockSpec((1, D_MODEL, tf), lambda i, e, f: (e, 0, f)),
            pl.BlockSpec((1, D_MODEL, tf), lambda i, e, f: (e, 0, f)),
            pl.BlockSpec((1, tf, D_MODEL), lambda i, e, f: (e, f, 0)),
        ],
        out_specs=pl.BlockSpec((tm, D_MODEL), lambda i, e, f: (i, 0)),
        out_shape=jax.ShapeDtypeStruct((n, D_MODEL), F32),
        scratch_shapes=[
            pltpu.VMEM((tm, LANES), F32),
            pltpu.VMEM((LANES, tm), F32),
            pltpu.VMEM((LANES, tm), F32),
            pltpu.VMEM((LANES, tm), F32),
            pltpu.VMEM((tm, D_MODEL), BF16),
            pltpu.VMEM((tm, D_MODEL), F32),
            pltpu.SMEM((2,), I32),
        ],
        compiler_params=_cparams(("parallel", "arbitrary", "arbitrary")),
        name="moe_ffn",
    )(h, gates, x, wg, wu, wd)


def _moe(x, g, r_hi, r_lo, wg, wu, wd):
    h, gates = _moe_route(x, g, r_hi, r_lo)
    return _moe_ffn(h, gates, x, wg, wu, wd)


def _ssd_kernel(xbc_ref, z_ref, dt_ref, cs_ref, h0_ref, cw_ref, cb_ref, dtb_ref, alog_ref,
                dexp_ref, og_ref, e_ref, y_ref, hout_ref, xpad_ref, h_ref, *, q):
    c = pl.program_id(1)

    @pl.when(c == 0)
    def _():
        xpad_ref[0:8, :] = cs_ref[0]
        h_ref[...] = h0_ref[0]

    xbc = xbc_ref[0]
    xpad_ref[8:8 + q, :] = xbc
    cw = cw_ref[...]
    conv = cb_ref[...] + (xpad_ref[5:5 + q, :] * cw[0:1] + xpad_ref[6:6 + q, :] * cw[1:2]
                          + xpad_ref[7:7 + q, :] * cw[2:3] + xbc * cw[3:4])
    xpad_ref[0:8, :] = xpad_ref[q:q + 8, :]
    act = _silu(conv)

    lane = lax.broadcasted_iota(I32, (1, LANES), 1)
    xdt = dt_ref[0] + dtb_ref[...]
    sp = jnp.maximum(xdt, 0.0) + jnp.log1p(jnp.exp(-jnp.abs(xdt)))
    dt = jnp.where(lane < SSD_HEADS, sp, 0.0)
    a = dt * (-jnp.exp(alog_ref[...]))

    rows = lax.broadcasted_iota(I32, (q, q), 0)
    cols = lax.broadcasted_iota(I32, (q, q), 1)
    causal = rows >= cols
    tril = jnp.where(causal, 1.0, 0.0).astype(BF16)
    a3 = _split3(a)
    acum = _dot(tril, a3[0]) + _dot(tril, a3[1]) + _dot(tril, a3[2])
    acum_t = jnp.transpose(acum)

    e_mat = e_ref[...]
    ac3 = _split3(acum)
    acum_x = _dot(ac3[0], e_mat) + _dot(ac3[1], e_mat) + _dot(ac3[2], e_mat)
    dt3 = _split3(dt)
    dt_x = _dot(dt3[0], e_mat) + _dot(dt3[1], e_mat) + _dot(dt3[2], e_mat)
    last = acum_x[q - 1:q, :]
    eac_x = jnp.exp(acum_x)
    dte_x = jnp.exp(last - acum_x)
    blkdec = jnp.exp(last)

    xs = act[:, :D_INNER]
    x_dt = xs * dt_x
    xb = x_dt.astype(BF16)
    xdb = (x_dt * dte_x).astype(BF16)
    z = z_ref[0]
    lane_lo = lane < SSD_HEADDIM

    gw = SSD_HPG * SSD_HEADDIM
    for g in range(SSD_GROUPS):
        bg = act[:, D_INNER + g * D_STATE:D_INNER + (g + 1) * D_STATE]
        cg = act[:, D_INNER + GN + g * D_STATE:D_INNER + GN + (g + 1) * D_STATE]
        cgb = cg.astype(BF16)
        bgt = jnp.transpose(bg).astype(BF16)
        cb = _dot(cgb, bgt)
        gs = slice(g * gw, (g + 1) * gw)
        h_in = h_ref[g]
        y_off = _dot(cgb, h_in.astype(BF16)) * eac_x[:, gs]
        h_ref[g] = h_in * blkdec[:, gs] + _dot(bgt, xdb[:, gs])
        parts = []
        for pr in range(SSD_HPG // 2):
            h0 = g * SSD_HPG + 2 * pr
            xp = xb[:, (h0 // 2) * LANES:(h0 // 2 + 1) * LANES]
            ys = []
            for hh in (h0, h0 + 1):
                seg = acum[:, hh:hh + 1] - acum_t[hh:hh + 1, :]
                lm = jnp.exp(jnp.where(causal, seg, NEG_BIG))
                ys.append(_dot((cb * lm).astype(BF16), xp))
            parts.append(jnp.where(lane_lo, ys[0], ys[1]))
        y_g = jnp.concatenate(parts, axis=1) + y_off + dexp_ref[:, gs] * xs[:, gs]
        gt = y_g * _silu(z[:, gs])
        gn = gt * lax.rsqrt(jnp.mean(gt * gt, axis=-1, keepdims=True) + EPS) * og_ref[:, gs]
        y_ref[0, :, gs] = gn.astype(y_ref.dtype)

    @pl.when(c == pl.num_programs(1) - 1)
    def _():
        hout_ref[0] = h_ref[...]


def _ssd(u, u_small, conv_state8, h0_t, cw8, cb, dtb, alog, dexp, og, e_mat, q):
    b, t, _ = u.shape
    kern = functools.partial(_ssd_kernel, q=q)
    full2 = lambda i, c: (0, 0)
    return pl.pallas_call(
        kern,
        grid=(b, t // q),
        in_specs=[
            pl.BlockSpec((1, q, CONV_DIM), lambda i, c: (i, c, 0)),
            pl.BlockSpec((1, q, D_INNER), lambda i, c: (i, c, 2)),
            pl.BlockSpec((1, q, LANES), lambda i, c: (i, c, 0)),
            pl.BlockSpec((1, 8, CONV_DIM), lambda i, c: (i, 0, 0)),
            pl.BlockSpec((1, SSD_GROUPS, D_STATE, SSD_HPG * SSD_HEADDIM), lambda i, c: (i, 0, 0, 0)),
            pl.BlockSpec((8, CONV_DIM), full2),
            pl.BlockSpec((1, CONV_DIM), full2),
            pl.BlockSpec((1, LANES), full2),
            pl.BlockSpec((1, LANES), full2),
            pl.BlockSpec((1, D_INNER), full2),
            pl.BlockSpec((1, D_INNER), full2),
            pl.BlockSpec((LANES, D_INNER), full2),
        ],
        out_specs=[
            pl.BlockSpec((1, q, D_INNER), lambda i, c: (i, c, 0)),
            pl.BlockSpec((1, SSD_GROUPS, D_STATE, SSD_HPG * SSD_HEADDIM), lambda i, c: (i, 0, 0, 0)),
        ],
        out_shape=[
            jax.ShapeDtypeStruct((b, t, D_INNER), BF16),
            jax.ShapeDtypeStruct((b, SSD_GROUPS, D_STATE, SSD_HPG * SSD_HEADDIM), F32),
        ],
        scratch_shapes=[pltpu.VMEM((q + 8, CONV_DIM), F32),
                        pltpu.VMEM((SSD_GROUPS, D_STATE, SSD_HPG * SSD_HEADDIM), F32)],
        compiler_params=_cparams(("parallel", "arbitrary")),
        name="ssd",
    )(u, u, u_small, conv_state8, h0_t, cw8, cb, dtb, alog, dexp, og, e_mat)


def _dsa_prep_kernel(q_ref, k_ref, qi_ref, sm_ref, cos_ref, sin_ref, qg_ref, kg_ref,
                     qo_ref, ko_ref, qio_ref, kio_ref):
    cos = cos_ref[...]
    sin = sin_ref[...]
    lane = lax.broadcasted_iota(I32, (1, LANES), 1)

    def head_norm(x):
        s = x * x
        for sh in (1, 2, 4, 8, 16, 32):
            s = s + jnp.where((lane & sh) == 0, pltpu.roll(s, LANES - sh, 1), pltpu.roll(s, sh, 1))
        return x * lax.rsqrt(s * (1.0 / ATT_HD) + EPS)

    def rope(x):
        partner = jnp.where((lane & 32) == 0, pltpu.roll(x, LANES - 32, 1), pltpu.roll(x, 32, 1))
        return x * cos + partner * sin

    for c in range(ATT_Q // LANES):
        sl = slice(c * LANES, (c + 1) * LANES)
        qo_ref[0, :, sl] = rope(head_norm(q_ref[0, :, sl]) * qg_ref[...]).astype(qo_ref.dtype)
    for c in range(ATT_KV // LANES):
        sl = slice(c * LANES, (c + 1) * LANES)
        ko_ref[0, :, sl] = rope(head_norm(k_ref[0, :, sl]) * kg_ref[...])
    for c in range(IDX_Q // LANES):
        sl = slice(c * LANES, (c + 1) * LANES)
        qio_ref[0, :, sl] = rope(qi_ref[0, :, sl]).astype(qio_ref.dtype)
    kio_ref[0] = rope(sm_ref[0])


def _dsa_prep(u, u_small, cos, sin, qg, kg):
    b, t, _ = u.shape
    tr = min(512, t)
    full2 = lambda i, j: (0, 0)
    return pl.pallas_call(
        _dsa_prep_kernel,
        grid=(b, t // tr),
        in_specs=[
            pl.BlockSpec((1, tr, ATT_Q), lambda i, j: (i, j, 0)),
            pl.BlockSpec((1, tr, ATT_KV), lambda i, j: (i, j, 4)),
            pl.BlockSpec((1, tr, IDX_Q), lambda i, j: (i, j, 3)),
            pl.BlockSpec((1, tr, LANES), lambda i, j: (i, j, 0)),
            pl.BlockSpec((tr, LANES), lambda i, j: (j, 0)),
            pl.BlockSpec((tr, LANES), lambda i, j: (j, 0)),
            pl.BlockSpec((1, LANES), full2),
            pl.BlockSpec((1, LANES), full2),
        ],
        out_specs=[
            pl.BlockSpec((1, tr, ATT_Q), lambda i, j: (i, j, 0)),
            pl.BlockSpec((1, tr, ATT_KV), lambda i, j: (i, j, 0)),
            pl.BlockSpec((1, tr, IDX_Q), lambda i, j: (i, j, 0)),
            pl.BlockSpec((1, tr, LANES), lambda i, j: (i, j, 0)),
        ],
        out_shape=[
            jax.ShapeDtypeStruct((b, t, ATT_Q), BF16),
            jax.ShapeDtypeStruct((b, t, ATT_KV), F32),
            jax.ShapeDtypeStruct((b, t, IDX_Q), BF16),
            jax.ShapeDtypeStruct((b, t, LANES), F32),
        ],
        compiler_params=_cparams(("parallel", "parallel")),
        name="dsa_prep",
    )(u, u, u, u_small, cos, sin, qg, kg)


V_ROWS = 80


def _fold8(x, op):
    r, n = x.shape
    x = x.reshape(r // 8, 8, n)
    while x.shape[0] > 1:
        h = x.shape[0] // 2
        y = op(x[:h], x[h:2 * h])
        x = y if x.shape[0] % 2 == 0 else jnp.concatenate([y, x[2 * h:]], axis=0)
    return x[0]


def _dsa_kernel(qit_ref, wit_ref, ki_ref, qt_ref, k_ref, vt_ref, o_ref,
                rhs_i_ref, rhs_q_ref, key_ref, bias_ref, s_ref, acc_ref, mm_ref,
                *, tq, cw, l_keys, start, nsel):
    i = pl.program_id(1)
    cols4 = ATT_GROUPS * tq
    last_pos = start + i * tq + (tq - 1)
    max_limit = jnp.minimum((lax.shift_right_logical(last_pos, 6) + 1) * CHUNK, l_keys)
    n_c = lax.div(max_limit + (cw - 1), cw)

    pos = start + i * tq + lax.broadcasted_iota(I32, (1, tq), 1)
    limit = jnp.minimum((lax.shift_right_logical(pos, 6) + 1) * CHUNK, l_keys)
    sub_pos = lax.broadcasted_iota(I32, (cw, 1), 0)

    for h in range(IDX_HEADS):
        rhs_i_ref[:, h * tq:(h + 1) * tq] = qit_ref[0, 0, h * IDX_HD:(h + 1) * IDX_HD, :]

    @pl.when(i == 0)
    def _():
        rhs_q_ref[...] = jnp.zeros_like(rhs_q_ref)

    for j in range(KV_HEADS):
        for g in range(ATT_GROUPS):
            hq = ATT_GROUPS * j + g
            rhs_q_ref[j, j * ATT_HD:(j + 1) * ATT_HD, g * tq:(g + 1) * tq] = \
                qt_ref[0, 0, hq * ATT_HD:(hq + 1) * ATT_HD, :]

    w = [wit_ref[0, 0, h:h + 1, :] * IDX_SCALE for h in range(IDX_HEADS)]

    def score_chunk(c, carry):
        d = jnp.maximum(_dot(ki_ref[0, c], rhs_i_ref[...]), 0.0)
        score = d[:, 0:tq] * w[0]
        for h in range(1, IDX_HEADS):
            score = score + d[:, h * tq:(h + 1) * tq] * w[h]
        bits = pltpu.bitcast(score, I32)
        key = jnp.where(bits < 0, bits ^ 0x7FFFFFFF, bits)
        key = jnp.where(score == 0.0, 0, key)
        key_ref[c] = jnp.where(sub_pos + c * cw < limit, key, INT_MIN)
        return carry

    lax.fori_loop(0, n_c, score_chunk, 0)

    def count(pred):
        def body(c, acc):
            return acc + _fold8(jnp.where(pred(key_ref[c], c), 1.0, 0.0), jnp.add)

        acc = lax.fori_loop(0, n_c, body, jnp.zeros((8, tq), F32))
        return jnp.sum(acc, axis=0, keepdims=True)

    def search(it, lo):
        cand = lo + lax.shift_left(jnp.int32(1), 31 - it)
        cnt = count(lambda k, c: k >= cand)
        return jnp.where(cnt >= nsel, cand, lo)

    thr = lax.fori_loop(0, 32, search, jnp.full((1, tq), INT_MIN, I32))

    c_gt = count(lambda k, c: k > thr)
    n_eq = count(lambda k, c: k == thr)
    need = nsel - c_gt
    excess = jnp.where((n_eq > need) & (thr > INT_MIN), 1.0, 0.0)
    mm_ref[...] = jnp.full((1, tq), 1 << 14, I32)

    @pl.when(jnp.max(excess) > 0.0)
    def _():
        def tie_search(it, m):
            cand = m + lax.shift_left(jnp.int32(1), 13 - it)
            f = count(lambda k, c: (k == thr) & (sub_pos + c * cw < cand))
            return jnp.where(f < need, cand, m)

        mm_ref[...] = lax.fori_loop(0, 14, tie_search, jnp.zeros((1, tq), I32))

    mm = mm_ref[...]

    def bias_chunk(c, carry):
        k = key_ref[c]
        kpos = sub_pos + c * cw
        sel = ((k > thr) | ((k == thr) & (kpos <= mm))) & (kpos < limit)
        bias_ref[c] = jnp.where(sel, 0.0, NEG_BIG)
        return carry

    lax.fori_loop(0, n_c, bias_chunk, 0)

    scale = (ATT_HD ** -0.5) * math.log2(math.e)
    for j0 in range(0, KV_HEADS, 2):
        pair = (j0, j0 + 1)

        def qk_chunk(c, ms):
            b1 = bias_ref[c]
            b4 = jnp.concatenate([b1] * ATT_GROUPS, axis=1)
            out = []
            for jj, j in enumerate(pair):
                s = _dot(k_ref[0, c], rhs_q_ref[j]) * scale + b4
                s_ref[jj, c] = s
                out.append(jnp.maximum(ms[jj], _fold8(s, jnp.maximum)))
            return tuple(out)

        m0 = jnp.full((8, cols4), -3e38, F32)
        ms = lax.fori_loop(0, n_c, qk_chunk, (m0, m0))
        ms = [jnp.max(m, axis=0, keepdims=True) for m in ms]
        acc_ref[...] = jnp.zeros_like(acc_ref)

        def pv_chunk(c, carry):
            for jj, j in enumerate(pair):
                p = jnp.exp2(s_ref[jj, c] - ms[jj])
                acc_ref[jj] += _dot(vt_ref[0, j, c], p.astype(BF16))
            return carry

        lax.fori_loop(0, n_c, pv_chunk, 0)
        for jj, j in enumerate(pair):
            a = acc_ref[jj]
            o_ref[0, 0, j] = (a[0:ATT_HD] / a[ATT_HD:ATT_HD + 1]).astype(o_ref.dtype)


def _dsa(qit, wit, ki, qt, k, vt, *, tq, cw, l_keys, start):
    b, nb = qit.shape[0], qit.shape[1]
    nc = k.shape[1]
    nsel = min(TOPK_MAX, l_keys // 4)
    kern = functools.partial(_dsa_kernel, tq=tq, cw=cw, l_keys=l_keys, start=start, nsel=float(nsel))
    cols4 = ATT_GROUPS * tq
    return pl.pallas_call(
        kern,
        grid=(b, nb),
        in_specs=[
            pl.BlockSpec((1, 1, IDX_Q, tq), lambda i, j: (i, j, 0, 0)),
            pl.BlockSpec((1, 1, IDX_HEADS, tq), lambda i, j: (i, j, 0, 0)),
            pl.BlockSpec((1, nc, cw, IDX_HD), lambda i, j: (i, 0, 0, 0)),
            pl.BlockSpec((1, 1, ATT_Q, tq), lambda i, j: (i, j, 0, 0)),
            pl.BlockSpec((1, nc, cw, ATT_KV), lambda i, j: (i, 0, 0, 0)),
            pl.BlockSpec((1, KV_HEADS, nc, V_ROWS, cw), lambda i, j: (i, 0, 0, 0, 0)),
        ],
        out_specs=pl.BlockSpec((1, 1, KV_HEADS, ATT_HD, cols4), lambda i, j: (i, j, 0, 0, 0)),
        out_shape=jax.ShapeDtypeStruct((b, nb, KV_HEADS, ATT_HD, cols4), BF16),
        scratch_shapes=[
            pltpu.VMEM((IDX_HD, IDX_HEADS * tq), BF16),
            pltpu.VMEM((KV_HEADS, ATT_KV, cols4), BF16),
            pltpu.VMEM((nc, cw, tq), I32),
            pltpu.VMEM((nc, cw, tq), F32),
            pltpu.VMEM((2, nc, cw, cols4), F32),
            pltpu.VMEM((2, V_ROWS, cols4), F32),
            pltpu.VMEM((1, tq), I32),
        ],
        compiler_params=_cparams(("parallel", "arbitrary")),
        name="dsa",
    )(qit, wit, ki, qt, k, vt)


def _dsa_layouts(q_rot, qi_rot, wi, k_all, v_all, ki_all, *, tq, cw):
    b, t, _ = q_rot.shape
    nb = -(-t // tq)
    padq = lambda a: jnp.pad(a, ((0, 0), (0, nb * tq - t), (0, 0)))
    tr = lambda a: padq(a).reshape(b, nb, tq, a.shape[-1]).transpose(0, 1, 3, 2)
    l_keys = k_all.shape[1]
    nc = -(-l_keys // cw)
    padk = lambda a: jnp.pad(a, ((0, 0), (0, nc * cw - l_keys), (0, 0))).astype(BF16)
    vt = padk(v_all).reshape(b, nc, cw, KV_HEADS, ATT_HD).transpose(0, 3, 1, 4, 2)
    ones = jnp.ones((b, KV_HEADS, nc, 1, cw), BF16)
    zeros = jnp.zeros((b, KV_HEADS, nc, V_ROWS - ATT_HD - 1, cw), BF16)
    return (tr(qi_rot), tr(wi), padk(ki_all).reshape(b, nc, cw, IDX_HD), tr(q_rot),
            padk(k_all).reshape(b, nc, cw, ATT_KV), jnp.concatenate([vt, ones, zeros], axis=3))


def _dsa_unlayout(o, t):
    b, nb, _, _, c4 = o.shape
    tq = c4 // ATT_GROUPS
    o = o.reshape(b, nb, KV_HEADS, ATT_HD, ATT_GROUPS, tq).transpose(0, 1, 5, 2, 4, 3)
    return o.reshape(b, nb * tq, ATT_Q)[:, :t]


def _prep_weights(ssd_w_in, att_w_in, ssd_w_out, att_w_out, mem_w_kv, dense_w_gate, dense_w_up,
                  dense_w_down, moe_router, moe_w_gate, moe_w_up, moe_w_down):
    bf = lambda w: w.astype(BF16)
    w = ssd_w_in[0]
    o_xbc, o_dt, o_mq = D_INNER, D_INNER + CONV_DIM, D_INNER + CONV_DIM + SSD_HEADS
    ssd_main = bf(jnp.concatenate([w[:, o_xbc:o_dt], w[:, o_mq:], w[:, :D_INNER]], axis=1))
    ssd_small = bf(jnp.pad(w[:, o_dt:o_mq], ((0, 0), (0, LANES - SSD_HEADS))))
    w = att_w_in[0]
    o_wi = ATT_Q + 2 * ATT_KV + IDX_Q
    o_ki = o_wi + IDX_HEADS
    o_mq = o_ki + IDX_HD
    att_main = bf(jnp.concatenate([w[:, :o_wi], w[:, o_mq:]], axis=1))
    att_small = bf(jnp.pad(jnp.concatenate([w[:, o_ki:o_mq], w[:, o_wi:o_ki]], axis=1),
                           ((0, 0), (0, LANES - IDX_HD - IDX_HEADS))))
    r = jnp.pad(moe_router[0], ((0, 0), (0, LANES - N_EXPERTS)))
    r_hi = bf(r)
    r_lo = bf(r - r_hi.astype(F32))
    return dict(
        ssd_main=ssd_main, ssd_small=ssd_small, att_main=att_main, att_small=att_small,
        ssd_out_a=bf(ssd_w_out[0, :D_INNER]), ssd_out_b=bf(ssd_w_out[0, D_INNER:]),
        att_out_a=bf(att_w_out[0, :ATT_Q]), att_out_b=bf(att_w_out[0, ATT_Q:]),
        mem_w_kv=bf(mem_w_kv), dense_g=bf(dense_w_gate[0]), dense_u=bf(dense_w_up[0]),
        dense_d=bf(dense_w_down[0]), r_hi=r_hi, r_lo=r_lo,
        moe_g=bf(moe_w_gate[0]), moe_u=bf(moe_w_up[0]), moe_d=bf(moe_w_down[0]))


def _rope_tables(pos):
    half = ATT_HD // 2
    inv = ROPE_THETA ** (-jnp.arange(half, dtype=F32) / half)
    ang = pos.astype(F32)[:, None] * inv[None, :]
    cos = jnp.cos(ang)
    sin = jnp.sin(ang)
    cos_t = jnp.concatenate([cos, cos, cos, cos], axis=1)
    sin_t = jnp.concatenate([-sin, sin, -sin, sin], axis=1)
    return cos_t, sin_t


def _trunk(x, start, mem_k, mem_v, conv_in, ssm_in, kv_in, P, W, q_ssd, tq_dsa, cw_dsa):
    b, t, _ = x.shape
    n = b * t
    x2 = x.reshape(n, D_MODEL)

    u = _norm_matmul(x2, W['ssd_norm'][0], P['ssd_main'], tn=1024).reshape(b, t, -1)
    u_small = _norm_matmul(x2, W['ssd_norm'][0], P['ssd_small'], tn=LANES).reshape(b, t, LANES)
    conv8 = jnp.pad(conv_in, ((0, 0), (8 - (CONV_W - 1), 0), (0, 0)))
    h0_t = ssm_in.reshape(b, SSD_GROUPS, SSD_HPG, SSD_HEADDIM, D_STATE).transpose(0, 1, 4, 2, 3)
    h0_t = h0_t.reshape(b, SSD_GROUPS, D_STATE, SSD_HPG * SSD_HEADDIM)
    cw8 = jnp.pad(W['ssd_conv_w'][0], ((0, 8 - CONV_W), (0, 0)))
    pad_h = lambda v: jnp.pad(v.astype(F32), (0, LANES - SSD_HEADS)).reshape(1, LANES)
    e_mat = (jnp.arange(LANES)[:, None] == (jnp.arange(D_INNER)[None, :] // SSD_HEADDIM)).astype(BF16)
    y_mix, h_t = _ssd(u, u_small, conv8, h0_t, cw8, W['ssd_conv_b'][0].reshape(1, CONV_DIM),
                      pad_h(W['ssd_dt_bias'][0]), pad_h(W['ssd_A_log'][0]),
                      jnp.repeat(W['ssd_D'][0].astype(F32), SSD_HEADDIM).reshape(1, D_INNER),
                      W['ssd_out_norm'][0].reshape(1, D_INNER), e_mat, q_ssd)
    new_conv = u[:, t - (CONV_W - 1):, :CONV_DIM]
    new_ssm = h_t.reshape(b, SSD_GROUPS, D_STATE, SSD_HPG, SSD_HEADDIM).transpose(0, 1, 3, 4, 2)
    new_ssm = new_ssm.reshape(b, SSD_HEADS, SSD_HEADDIM, D_STATE)
    y_mem = _mem_attn(u, 3, mem_k[0], mem_v[0], W['mem_q_norm'][0])
    x2 = _proj_res(y_mix.reshape(n, D_INNER), y_mem.reshape(n, MEM_WIDTH), P['ssd_out_a'], P['ssd_out_b'], x2)
    x2 = _ffn(x2, W['ffn_norm'][0], P['dense_g'], P['dense_u'], P['dense_d'])

    u = _norm_matmul(x2, W['att_norm'][0], P['att_main'], tn=1024).reshape(b, t, -1)
    u_small = _norm_matmul(x2, W['att_norm'][0], P['att_small'], tn=LANES).reshape(b, t, LANES)
    pos = start + jnp.arange(t)
    cos_t, sin_t = _rope_tables(pos)
    tile2 = lambda v: jnp.tile(v.astype(F32), 2).reshape(1, LANES)
    q_rot, k_rot, qi_rot, ki_rot = _dsa_prep(u, u_small, cos_t, sin_t, tile2(W['att_q_norm'][0]),
                                             tile2(W['att_k_norm'][0]))
    v_new = u[:, :, ATT_Q + ATT_KV:ATT_Q + 2 * ATT_KV]
    ki_new = ki_rot[:, :, :IDX_HD]
    if kv_in is None:
        k_all, v_all, ki_all = k_rot, v_new, ki_new
    else:
        k_all = jnp.concatenate([kv_in[0], k_rot], axis=1)
        v_all = jnp.concatenate([kv_in[1], v_new], axis=1)
        ki_all = jnp.concatenate([kv_in[2], ki_new], axis=1)
    l_keys = k_all.shape[1]
    wi_raw = u_small[:, :, IDX_HD:IDX_HD + IDX_HEADS]
    lay = _dsa_layouts(q_rot, qi_rot, wi_raw, k_all, v_all, ki_all, tq=tq_dsa, cw=cw_dsa)
    o_t = _dsa(*lay, tq=tq_dsa, cw=cw_dsa, l_keys=l_keys, start=start)
    y_mix = _dsa_unlayout(o_t, t).reshape(n, ATT_Q)
    y_mem = _mem_attn(u, 2, mem_k[1], mem_v[1], W['mem_q_norm'][1])
    x2 = _proj_res(y_mix, y_mem.reshape(n, MEM_WIDTH), P['att_out_a'], P['att_out_b'], x2)
    x2 = _moe(x2, W['ffn_norm'][1], P['r_hi'], P['r_lo'], P['moe_g'], P['moe_u'], P['moe_d'])

    return (x2.reshape(b, t, D_MODEL), new_conv, new_ssm, k_rot.reshape(b, t, KV_HEADS, ATT_HD),
            v_new.reshape(b, t, KV_HEADS, ATT_HD), ki_new)


def kernel(x_prompt, x_sample, mem_prompt, cache_conv, state_ssm, cache_k, cache_v, cache_idx_k, cache_mem_k, cache_mem_v, ssd_norm, ssd_w_in, ssd_conv_w, ssd_conv_b, ssd_dt_bias, ssd_A_log, ssd_D, ssd_out_norm, ssd_w_out, att_norm, att_w_in, att_q_norm, att_k_norm, att_w_out, mem_norm, mem_w_kv, mem_q_norm, mem_k_norm, ffn_norm, dense_w_gate, dense_w_up, dense_w_down, moe_router, moe_w_gate, moe_w_up, moe_w_down):
    W = dict(ssd_norm=ssd_norm, ssd_conv_w=ssd_conv_w, ssd_conv_b=ssd_conv_b, ssd_dt_bias=ssd_dt_bias,
             ssd_A_log=ssd_A_log, ssd_D=ssd_D, ssd_out_norm=ssd_out_norm, att_norm=att_norm,
             att_q_norm=att_q_norm, att_k_norm=att_k_norm, mem_q_norm=mem_q_norm, ffn_norm=ffn_norm)
    P = _prep_weights(ssd_w_in, att_w_in, ssd_w_out, att_w_out, mem_w_kv, dense_w_gate, dense_w_up,
                      dense_w_down, moe_router, moe_w_gate, moe_w_up, moe_w_down)
    bp, sp = x_prompt.shape[0], x_prompt.shape[1]
    bs = x_sample.shape[0]

    mem2 = mem_prompt.reshape(bp * N_MEM, D_MODEL)
    pk, pv = [], []
    for i in range(2):
        kv = _norm_matmul(mem2, mem_norm[i], P['mem_w_kv'][i], tn=1024)
        pk.append(_head_norm(kv[:, :MEM_WIDTH], mem_k_norm[i]).reshape(bp, N_MEM, MEM_WIDTH))
        pv.append(kv[:, MEM_WIDTH:].reshape(bp, N_MEM, MEM_WIDTH))
    p_mem_k = jnp.stack(pk)
    p_mem_v = jnp.stack(pv)

    conv0 = jnp.zeros((bp, CONV_W - 1, CONV_DIM), F32)
    ssm0 = jnp.zeros((bp, SSD_HEADS, SSD_HEADDIM, D_STATE), F32)
    y_p, p_conv, p_ssm, p_k, p_v, p_ki = _trunk(x_prompt, 0, p_mem_k, p_mem_v, conv0, ssm0, None, P, W,
                                                q_ssd=128, tq_dsa=128, cw_dsa=512)
    past = cache_k.shape[2]
    kv_in = (cache_k[0].reshape(bs, past, ATT_KV), cache_v[0].reshape(bs, past, ATT_KV), cache_idx_k[0])
    y_s, s_conv, s_ssm, s_k, s_v, s_ki = _trunk(
        x_sample, past, cache_mem_k.reshape(2, bs, N_MEM, MEM_WIDTH), cache_mem_v.reshape(2, bs, N_MEM, MEM_WIDTH),
        cache_conv[0], state_ssm[0], kv_in, P, W, q_ssd=x_sample.shape[1], tq_dsa=128, cw_dsa=384)

    shp = (bp, N_MEM, MEM_HEADS, MEM_HD)
    return (y_p, y_s, p_conv[None], p_ssm[None], p_k[None], p_v[None], p_ki[None],
            p_mem_k.reshape((2,) + shp), p_mem_v.reshape((2,) + shp),
            s_conv[None], s_ssm[None], s_k[None], s_v[None], s_ki[None])
```

```python
import functools
import math

import jax
import jax.numpy as jnp
from jax import lax
from jax.experimental import pallas as pl
from jax.experimental.pallas import tpu as pltpu

F32 = jnp.float32
BF16 = jnp.bfloat16
I32 = jnp.int32

D_MODEL = 1024
CHUNK = 64
N_MEM = 256
EPS = 1e-6
D_INNER = 2048
SSD_HEADDIM = 64
SSD_HEADS = 32
SSD_GROUPS = 4
SSD_HPG = 8
D_STATE = 128
CONV_W = 4
GN = SSD_GROUPS * D_STATE
CONV_DIM = D_INNER + 2 * GN
ATT_HEADS = 16
ATT_HD = 64
KV_HEADS = 4
ATT_GROUPS = 4
ATT_Q = 1024
ATT_KV = 256
IDX_HEADS = 8
IDX_HD = 64
IDX_Q = 512
IDX_SCALE = (IDX_HEADS * IDX_HD) ** -0.5
TOPK_MAX = 256
ROPE_THETA = 10000.0
MEM_HEADS = 4
MEM_HD = 256
MEM_WIDTH = 1024
D_FF = 3584
N_EXPERTS = 8

LANES = 128
INT_MIN = -(2 ** 31)
NEG_BIG = -1e30
VMEM_LIMIT = 56 * 1024 * 1024


def _cparams(sem):
    return pltpu.CompilerParams(dimension_semantics=sem, vmem_limit_bytes=VMEM_LIMIT)


def _dot(a, b):
    return jnp.dot(a, b, preferred_element_type=F32)


def _split3(v):
    hi = v.astype(BF16)
    r1 = v - hi.astype(F32)
    mid = r1.astype(BF16)
    lo = (r1 - mid.astype(F32)).astype(BF16)
    return hi, mid, lo


def _silu(x):
    return x / (1.0 + jnp.exp(-x))


def _norm_matmul_kernel(x_ref, g_ref, w_ref, o_ref, h_ref):
    @pl.when(pl.program_id(1) == 0)
    def _():
        x = x_ref[...]
        ms = jnp.mean(x * x, axis=-1, keepdims=True)
        h_ref[...] = (x * lax.rsqrt(ms + EPS) * g_ref[...]).astype(BF16)

    o_ref[...] = _dot(h_ref[...], w_ref[...]).astype(o_ref.dtype)


def _norm_matmul(x, g, w, tn, out_dtype=F32):
    n, k = x.shape
    m = w.shape[1]
    tm = min(1024, n)
    return pl.pallas_call(
        _norm_matmul_kernel,
        grid=(n // tm, m // tn),
        in_specs=[
            pl.BlockSpec((tm, k), lambda i, j: (i, 0)),
            pl.BlockSpec((1, k), lambda i, j: (0, 0)),
            pl.BlockSpec((k, tn), lambda i, j: (0, j)),
        ],
        out_specs=pl.BlockSpec((tm, tn), lambda i, j: (i, j)),
        out_shape=jax.ShapeDtypeStruct((n, m), out_dtype),
        scratch_shapes=[pltpu.VMEM((tm, k), BF16)],
        compiler_params=_cparams(("parallel", "arbitrary")),
        name="norm_matmul",
    )(x, g.reshape(1, k), w)


def _head_norm_kernel(x_ref, g_ref, o_ref):
    for h in range(MEM_HEADS):
        x = x_ref[:, h * MEM_HD:(h + 1) * MEM_HD]
        ms = jnp.mean(x * x, axis=-1, keepdims=True)
        o_ref[:, h * MEM_HD:(h + 1) * MEM_HD] = x * lax.rsqrt(ms + EPS) * g_ref[...]


def _head_norm(x, g):
    n = x.shape[0]
    tm = min(512, n)
    return pl.pallas_call(
        _head_norm_kernel,
        grid=(n // tm,),
        in_specs=[pl.BlockSpec((tm, MEM_WIDTH), lambda i: (i, 0)),
                  pl.BlockSpec((1, MEM_HD), lambda i: (0, 0))],
        out_specs=pl.BlockSpec((tm, MEM_WIDTH), lambda i: (i, 0)),
        out_shape=jax.ShapeDtypeStruct((n, MEM_WIDTH), F32),
        compiler_params=_cparams(("parallel",)),
        name="head_norm",
    )(x, g.reshape(1, MEM_HD))


def _mem_attn_kernel(q_ref, k_ref, v_ref, g_ref, o_ref):
    for h in range(MEM_HEADS):
        sl = slice(h * MEM_HD, (h + 1) * MEM_HD)
        q = q_ref[0, :, sl]
        ms = jnp.mean(q * q, axis=-1, keepdims=True)
        qn = (q * lax.rsqrt(ms + EPS) * g_ref[...]).astype(BF16)
        k = k_ref[0, :, sl].astype(BF16)
        s = lax.dot_general(qn, k, (((1,), (1,)), ((), ())), preferred_element_type=F32)
        s = s * (MEM_HD ** -0.5)
        m = jnp.max(s, axis=-1, keepdims=True)
        p = jnp.exp(s - m)
        p = p / jnp.sum(p, axis=-1, keepdims=True)
        o = _dot(p.astype(BF16), v_ref[0, :, sl].astype(BF16))
        o_ref[0, :, sl] = o.astype(o_ref.dtype)


def _mem_attn(u, col_block, mk, mv, g):
    b, t, _ = u.shape
    tq = min(512, t)
    return pl.pallas_call(
        _mem_attn_kernel,
        grid=(b, t // tq),
        in_specs=[
            pl.BlockSpec((1, tq, MEM_WIDTH), lambda i, j: (i, j, col_block)),
            pl.BlockSpec((1, N_MEM, MEM_WIDTH), lambda i, j: (i, 0, 0)),
            pl.BlockSpec((1, N_MEM, MEM_WIDTH), lambda i, j: (i, 0, 0)),
            pl.BlockSpec((1, MEM_HD), lambda i, j: (0, 0)),
        ],
        out_specs=pl.BlockSpec((1, tq, MEM_WIDTH), lambda i, j: (i, j, 0)),
        out_shape=jax.ShapeDtypeStruct((b, t, MEM_WIDTH), BF16),
        compiler_params=_cparams(("parallel", "parallel")),
        name="mem_attn",
    )(u, mk, mv, g.reshape(1, MEM_HD))


def _proj_res_kernel(a_ref, b_ref, wa_ref, wb_ref, x_ref, o_ref):
    o_ref[...] = x_ref[...] + _dot(a_ref[...], wa_ref[...]) + _dot(b_ref[...], wb_ref[...])


def _proj_res(a, b, wa, wb, x):
    n = x.shape[0]
    tm = min(512, n)
    ka, kb = a.shape[1], b.shape[1]
    return pl.pallas_call(
        _proj_res_kernel,
        grid=(n // tm,),
        in_specs=[
            pl.BlockSpec((tm, ka), lambda i: (i, 0)),
            pl.BlockSpec((tm, kb), lambda i: (i, 0)),
            pl.BlockSpec((ka, D_MODEL), lambda i: (0, 0)),
            pl.BlockSpec((kb, D_MODEL), lambda i: (0, 0)),
            pl.BlockSpec((tm, D_MODEL), lambda i: (i, 0)),
        ],
        out_specs=pl.BlockSpec((tm, D_MODEL), lambda i: (i, 0)),
        out_shape=jax.ShapeDtypeStruct((n, D_MODEL), F32),
        compiler_params=_cparams(("parallel",)),
        name="proj_res",
    )(a, b, wa, wb, x)


def _ffn_kernel(x_ref, g_ref, wg_ref, wu_ref, wd_ref, o_ref, h_ref, acc_ref):
    f = pl.program_id(1)

    @pl.when(f == 0)
    def _():
        x = x_ref[...]
        ms = jnp.mean(x * x, axis=-1, keepdims=True)
        h_ref[...] = (x * lax.rsqrt(ms + EPS) * g_ref[...]).astype(BF16)
        acc_ref[...] = x

    h = h_ref[...]
    a = _silu(_dot(h, wg_ref[...])) * _dot(h, wu_ref[...])
    acc_ref[...] += _dot(a.astype(BF16), wd_ref[...])

    @pl.when(f == pl.num_programs(1) - 1)
    def _():
        o_ref[...] = acc_ref[...]


def _ffn(x, g, wg, wu, wd, tf=512):
    n = x.shape[0]
    tm = min(1024, n)
    return pl.pallas_call(
        _ffn_kernel,
        grid=(n // tm, D_FF // tf),
        in_specs=[
            pl.BlockSpec((tm, D_MODEL), lambda i, f: (i, 0)),
            pl.BlockSpec((1, D_MODEL), lambda i, f: (0, 0)),
            pl.BlockSpec((D_MODEL, tf), lambda i, f: (0, f)),
            pl.BlockSpec((D_MODEL, tf), lambda i, f: (0, f)),
            pl.BlockSpec((tf, D_MODEL), lambda i, f: (f, 0)),
        ],
        out_specs=pl.BlockSpec((tm, D_MODEL), lambda i, f: (i, 0)),
        out_shape=jax.ShapeDtypeStruct((n, D_MODEL), F32),
        scratch_shapes=[pltpu.VMEM((tm, D_MODEL), BF16), pltpu.VMEM((tm, D_MODEL), F32)],
        compiler_params=_cparams(("parallel", "arbitrary")),
        name="ffn",
    )(x, g.reshape(1, D_MODEL), wg, wu, wd)


MOE_ROWS = 256


def _split2(v):
    hi = v.astype(BF16)
    return hi, (v - hi.astype(F32)).astype(BF16)


def _moe_route_kernel(x_ref, g_ref, rh_ref, rl_ref, h_ref, gate_ref):
    lane = lax.broadcasted_iota(I32, (1, LANES), 1)
    x = x_ref[...]
    ms = jnp.mean(x * x, axis=-1, keepdims=True)
    hf = x * lax.rsqrt(ms + EPS) * g_ref[...]
    hb, hl = _split2(hf)
    h_ref[...] = hb
    logits = _dot(hb, rh_ref[...]) + _dot(hl, rh_ref[...]) + _dot(hb, rl_ref[...])
    valid = lane < N_EXPERTS
    logits = jnp.where(valid, logits, NEG_BIG)
    m = jnp.max(logits, axis=-1, keepdims=True)
    p = jnp.exp(logits - m)
    p = p / jnp.sum(p, axis=-1, keepdims=True)
    p = jnp.where(valid, p, -1.0)
    v1 = jnp.max(p, axis=-1, keepdims=True)
    i1 = jnp.min(jnp.where(p == v1, lane, LANES), axis=-1, keepdims=True)
    p2 = jnp.where(lane == i1, -1.0, p)
    v2 = jnp.max(p2, axis=-1, keepdims=True)
    i2 = jnp.min(jnp.where(p2 == v2, lane, LANES), axis=-1, keepdims=True)
    den = v1 + v2
    gate_ref[...] = jnp.where(lane == i1, v1 / den, jnp.where(lane == i2, v2 / den, 0.0))


def _moe_route(x, g, r_hi, r_lo):
    n = x.shape[0]
    tm = min(512, n)
    return pl.pallas_call(
        _moe_route_kernel,
        grid=(n // tm,),
        in_specs=[pl.BlockSpec((tm, D_MODEL), lambda i: (i, 0)),
                  pl.BlockSpec((1, D_MODEL), lambda i: (0, 0)),
                  pl.BlockSpec((D_MODEL, LANES), lambda i: (0, 0)),
                  pl.BlockSpec((D_MODEL, LANES), lambda i: (0, 0))],
        out_specs=[pl.BlockSpec((tm, D_MODEL), lambda i: (i, 0)),
                   pl.BlockSpec((tm, LANES), lambda i: (i, 0))],
        out_shape=[jax.ShapeDtypeStruct((n, D_MODEL), BF16), jax.ShapeDtypeStruct((n, LANES), F32)],
        compiler_params=_cparams(("parallel",)),
        name="moe_route",
    )(x, g.reshape(1, D_MODEL), r_hi, r_lo)


def _moe_ffn_kernel(h_ref, gate_ref, x_ref, wg_ref, wu_ref, wd_ref, o_ref,
                    rank_ref, rank_t_ref, sel_t_ref, gate_t_ref, xe_ref, acc_ref, nb_ref, *, tm):
    e = pl.program_id(1)
    f = pl.program_id(2)
    nf = pl.num_programs(2)
    big, small = MOE_ROWS, MOE_ROWS // 2
    lane = lax.broadcasted_iota(I32, (1, LANES), 1)

    @pl.when((e == 0) & (f == 0))
    def _():
        o_ref[...] = x_ref[...]
        gates = gate_ref[...]
        sel = jnp.where(gates > 0.0, 1.0, 0.0)
        r = lax.broadcasted_iota(I32, (tm, tm), 0)
        c = lax.broadcasted_iota(I32, (tm, tm), 1)
        below = jnp.where(c < r, 1.0, 0.0).astype(BF16)
        rank = _dot(below, sel.astype(BF16))
        rank_ref[...] = rank
        rank_t_ref[...] = jnp.transpose(rank)
        sel_t_ref[...] = jnp.transpose(sel)
        gate_t_ref[...] = jnp.transpose(gates)

    @pl.when(f == 0)
    def _():
        cnt = jnp.sum(jnp.where(lane == e, jnp.sum(jnp.where(gate_ref[...] > 0.0, 1.0, 0.0), axis=0, keepdims=True),
                                0.0)).astype(I32)
        n_big = lax.div(cnt, big)
        rem = cnt - n_big * big
        nb_ref[0] = n_big + jnp.where(rem > small, 1, 0)
        nb_ref[1] = jnp.where((rem > 0) & (rem <= small), 1, 0)

    n_big = nb_ref[0]
    has_small = nb_ref[1] > 0
    r_small = pl.multiple_of(n_big * big, big)

    def pick(r0, rows):
        want = (lax.broadcasted_iota(I32, (rows, 1), 0) + r0).astype(F32)
        return (rank_t_ref[pl.ds(e, 1), :] == want) & (sel_t_ref[pl.ds(e, 1), :] > 0.0)

    def for_blocks(fn):
        def body(rb, carry):
            fn(pl.multiple_of(rb * big, big), big)
            return carry

        lax.fori_loop(0, n_big, body, 0)

        @pl.when(has_small)
        def _():
            fn(r_small, small)

    def gather(r0, rows):
        p = jnp.where(pick(r0, rows), 1.0, 0.0).astype(BF16)
        xe_ref[pl.ds(r0, rows), :] = _dot(p, h_ref[...]).astype(BF16)

    def ffn(r0, rows):
        xe = xe_ref[pl.ds(r0, rows), :]
        a = _silu(_dot(xe, wg_ref[0])) * _dot(xe, wu_ref[0])
        y = _dot(a.astype(BF16), wd_ref[0])

        @pl.when(f == 0)
        def _():
            acc_ref[pl.ds(r0, rows), :] = y

        @pl.when(f > 0)
        def _():
            acc_ref[pl.ds(r0, rows), :] += y

    @pl.when(f == 0)
    def _():
        for_blocks(gather)

    for_blocks(ffn)

    @pl.when(f == nf - 1)
    def _():
        rank_col = jnp.sum(jnp.where(lane == e, rank_ref[...], 0.0), axis=1, keepdims=True)
        sel_col = jnp.sum(jnp.where(lane == e, gate_ref[...], 0.0), axis=1, keepdims=True) > 0.0

        def scatter(r0, rows):
            g_rows = jnp.sum(jnp.where(pick(r0, rows), gate_t_ref[pl.ds(e, 1), :], 0.0), axis=1, keepdims=True)
            y = (acc_ref[pl.ds(r0, rows), :] * g_rows).astype(BF16)
            col = (lax.broadcasted_iota(I32, (1, rows), 1) + r0).astype(F32)
            put = jnp.where((rank_col == col) & sel_col, 1.0, 0.0).astype(BF16)
            o_ref[...] += _dot(put, y)

        for_blocks(scatter)


def _moe_ffn(h, gates, x, wg, wu, wd, *, tm=1024, tf=512):
    n = x.shape[0]
    tm = min(tm, n)
    kern = functools.partial(_moe_ffn_kernel, tm=tm)
    return pl.pallas_call(
        kern,
        grid=(n // tm, N_EXPERTS, D_FF // tf),
        in_specs=[
            pl.BlockSpec((tm, D_MODEL), lambda i, e, f: (i, 0)),
            pl.BlockSpec((tm, LANES), lambda i, e, f: (i, 0)),
            pl.BlockSpec((tm, D_MODEL), lambda i, e, f: (i, 0)),
            pl.BlockSpec((1, D_MODEL, tf), lambda i, e, f: (e, 0, f)),
            pl.BlockSpec((1, D_MODEL, tf), lambda i, e, f: (e, 0, f)),
            pl.BlockSpec((1, tf, D_MODEL), lambda i, e, f: (e, f, 0)),
        ],
        out_specs=pl.BlockSpec((tm, D_MODEL), lambda i, e, f: (i, 0)),
        out_shape=jax.ShapeDtypeStruct((n, D_MODEL), F32),
        scratch_shapes=[
            pltpu.VMEM((tm, LANES), F32),
            pltpu.VMEM((LANES, tm), F32),
            pltpu.VMEM((LANES, tm), F32),
            pltpu.VMEM((LANES, tm), F32),
            pltpu.VMEM((tm, D_MODEL), BF16),
            pltpu.VMEM((tm, D_MODEL), F32),
            pltpu.SMEM((2,), I32),
        ],
        compiler_params=_cparams(("parallel", "arbitrary", "arbitrary")),
        name="moe_ffn",
    )(h, gates, x, wg, wu, wd)


def _moe(x, g, r_hi, r_lo, wg, wu, wd):
    h, gates = _moe_route(x, g, r_hi, r_lo)
    return _moe_ffn(h, gates, x, wg, wu, wd)


def _ssd_kernel(xbc_ref, z_ref, dt_ref, cs_ref, h0_ref, cw_ref, cb_ref, dtb_ref, alog_ref,
                dexp_ref, og_ref, e_ref, y_ref, hout_ref, xpad_ref, h_ref, *, q):
    c = pl.program_id(1)

    @pl.when(c == 0)
    def _():
        xpad_ref[0:8, :] = cs_ref[0]
        h_ref[...] = h0_ref[0]

    xbc = xbc_ref[0]
    xpad_ref[8:8 + q, :] = xbc
    cw = cw_ref[...]
    conv = cb_ref[...] + (xpad_ref[5:5 + q, :] * cw[0:1] + xpad_ref[6:6 + q, :] * cw[1:2]
                          + xpad_ref[7:7 + q, :] * cw[2:3] + xbc * cw[3:4])
    xpad_ref[0:8, :] = xpad_ref[q:q + 8, :]
    act = _silu(conv)

    lane = lax.broadcasted_iota(I32, (1, LANES), 1)
    xdt = dt_ref[0] + dtb_ref[...]
    sp = jnp.maximum(xdt, 0.0) + jnp.log1p(jnp.exp(-jnp.abs(xdt)))
    dt = jnp.where(lane < SSD_HEADS, sp, 0.0)
    a = dt * (-jnp.exp(alog_ref[...]))

    rows = lax.broadcasted_iota(I32, (q, q), 0)
    cols = lax.broadcasted_iota(I32, (q, q), 1)
    causal = rows >= cols
    tril = jnp.where(causal, 1.0, 0.0).astype(BF16)
    a3 = _split3(a)
    acum = _dot(tril, a3[0]) + _dot(tril, a3[1]) + _dot(tril, a3[2])
    acum_t = jnp.transpose(acum)

    e_mat = e_ref[...]
    ac3 = _split3(acum)
    acum_x = _dot(ac3[0], e_mat) + _dot(ac3[1], e_mat) + _dot(ac3[2], e_mat)
    dt3 = _split3(dt)
    dt_x = _dot(dt3[0], e_mat) + _dot(dt3[1], e_mat) + _dot(dt3[2], e_mat)
    last = acum_x[q - 1:q, :]
    eac_x = jnp.exp(acum_x)
    dte_x = jnp.exp(last - acum_x)
    blkdec = jnp.exp(last)

    xs = act[:, :D_INNER]
    x_dt = xs * dt_x
    xb = x_dt.astype(BF16)
    xdb = (x_dt * dte_x).astype(BF16)
    z = z_ref[0]
    lane_lo = lane < SSD_HEADDIM

    gw = SSD_HPG * SSD_HEADDIM
    for g in range(SSD_GROUPS):
        bg = act[:, D_INNER + g * D_STATE:D_INNER + (g + 1) * D_STATE]
        cg = act[:, D_INNER + GN + g * D_STATE:D_INNER + GN + (g + 1) * D_STATE]
        cgb = cg.astype(BF16)
        bgt = jnp.transpose(bg).astype(BF16)
        cb = _dot(cgb, bgt)
        gs = slice(g * gw, (g + 1) * gw)
        h_in = h_ref[g]
        y_off = _dot(cgb, h_in.astype(BF16)) * eac_x[:, gs]
        h_ref[g] = h_in * blkdec[:, gs] + _dot(bgt, xdb[:, gs])
        parts = []
        for pr in range(SSD_HPG // 2):
            h0 = g * SSD_HPG + 2 * pr
            xp = xb[:, (h0 // 2) * LANES:(h0 // 2 + 1) * LANES]
            ys = []
            for hh in (h0, h0 + 1):
                seg = acum[:, hh:hh + 1] - acum_t[hh:hh + 1, :]
                lm = jnp.exp(jnp.where(causal, seg, NEG_BIG))
                ys.append(_dot((cb * lm).astype(BF16), xp))
            parts.append(jnp.where(lane_lo, ys[0], ys[1]))
        y_g = jnp.concatenate(parts, axis=1) + y_off + dexp_ref[:, gs] * xs[:, gs]
        gt = y_g * _silu(z[:, gs])
        gn = gt * lax.rsqrt(jnp.mean(gt * gt, axis=-1, keepdims=True) + EPS) * og_ref[:, gs]
        y_ref[0, :, gs] = gn.astype(y_ref.dtype)

    @pl.when(c == pl.num_programs(1) - 1)
    def _():
        hout_ref[0] = h_ref[...]


def _ssd(u, u_small, conv_state8, h0_t, cw8, cb, dtb, alog, dexp, og, e_mat, q):
    b, t, _ = u.shape
    kern = functools.partial(_ssd_kernel, q=q)
    full2 = lambda i, c: (0, 0)
    return pl.pallas_call(
        kern,
        grid=(b, t // q),
        in_specs=[
            pl.BlockSpec((1, q, CONV_DIM), lambda i, c: (i, c, 0)),
            pl.BlockSpec((1, q, D_INNER), lambda i, c: (i, c, 2)),
            pl.BlockSpec((1, q, LANES), lambda i, c: (i, c, 0)),
            pl.BlockSpec((1, 8, CONV_DIM), lambda i, c: (i, 0, 0)),
            pl.BlockSpec((1, SSD_GROUPS, D_STATE, SSD_HPG * SSD_HEADDIM), lambda i, c: (i, 0, 0, 0)),
            pl.BlockSpec((8, CONV_DIM), full2),
            pl.BlockSpec((1, CONV_DIM), full2),
            pl.BlockSpec((1, LANES), full2),
            pl.BlockSpec((1, LANES), full2),
            pl.BlockSpec((1, D_INNER), full2),
            pl.BlockSpec((1, D_INNER), full2),
            pl.BlockSpec((LANES, D_INNER), full2),
        ],
        out_specs=[
            pl.BlockSpec((1, q, D_INNER), lambda i, c: (i, c, 0)),
            pl.BlockSpec((1, SSD_GROUPS, D_STATE, SSD_HPG * SSD_HEADDIM), lambda i, c: (i, 0, 0, 0)),
        ],
        out_shape=[
            jax.ShapeDtypeStruct((b, t, D_INNER), BF16),
            jax.ShapeDtypeStruct((b, SSD_GROUPS, D_STATE, SSD_HPG * SSD_HEADDIM), F32),
        ],
        scratch_shapes=[pltpu.VMEM((q + 8, CONV_DIM), F32),
                        pltpu.VMEM((SSD_GROUPS, D_STATE, SSD_HPG * SSD_HEADDIM), F32)],
        compiler_params=_cparams(("parallel", "arbitrary")),
        name="ssd",
    )(u, u, u_small, conv_state8, h0_t, cw8, cb, dtb, alog, dexp, og, e_mat)


V_ROWS = 80
DSA_TQ = LANES


def _dsa_prep_kernel(q_ref, k_ref, v_ref, qi_ref, sm_ref, cos_ref, sin_ref, qg_ref, kg_ref, seg_ref,
                     ko_ref, kio_ref, qt_ref, qit_ref, wit_ref, *key_refs, tr):
    cos = cos_ref[...]
    sin = sin_ref[...]
    seg = seg_ref[...]
    lane = lax.broadcasted_iota(I32, (1, LANES), 1)
    nqb = tr // DSA_TQ

    def head_norm(x):
        s3 = _split3(x * x)
        ss = _dot(s3[0], seg) + _dot(s3[1], seg) + _dot(s3[2], seg)
        return x * lax.rsqrt(ss * (1.0 / ATT_HD) + EPS)

    def rope(x):
        partner = jnp.where((lane & 32) == 0, pltpu.roll(x, LANES - 32, 1), pltpu.roll(x, 32, 1))
        return x * cos + partner * sin

    def put_t(dst_ref, row0, x):
        for qb in range(nqb):
            xt = jnp.transpose(x[qb * DSA_TQ:(qb + 1) * DSA_TQ, :])
            dst_ref[0, qb, row0:row0 + LANES, :] = xt.astype(dst_ref.dtype)

    for c in range(ATT_Q // LANES):
        sl = slice(c * LANES, (c + 1) * LANES)
        put_t(qt_ref, c * LANES, rope(head_norm(q_ref[0, :, sl]) * qg_ref[...]))
    for c in range(IDX_Q // LANES):
        sl = slice(c * LANES, (c + 1) * LANES)
        put_t(qit_ref, c * LANES, rope(qi_ref[0, :, sl]))
    sm = sm_ref[0]
    for qb in range(nqb):
        wit_ref[0, qb] = jnp.transpose(sm[qb * DSA_TQ:(qb + 1) * DSA_TQ, :])[IDX_HD:IDX_HD + IDX_HEADS, :]
    ki = rope(sm)
    kio_ref[0] = ki
    k_rot = []
    for c in range(ATT_KV // LANES):
        sl = slice(c * LANES, (c + 1) * LANES)
        k_rot.append(rope(head_norm(k_ref[0, :, sl]) * kg_ref[...]))
        ko_ref[0, :, sl] = k_rot[c]

    if key_refs:
        kb_ref, kib_ref, vt_ref = key_refs
        kib_ref[0] = ki[:, :IDX_HD].astype(BF16)
        ones_row = jnp.where(lax.broadcasted_iota(I32, (V_ROWS - ATT_HD, tr), 0) == 0, 1.0, 0.0).astype(BF16)
        for c in range(ATT_KV // LANES):
            sl = slice(c * LANES, (c + 1) * LANES)
            kb_ref[0, :, sl] = k_rot[c].astype(BF16)
            vt = jnp.transpose(v_ref[0, :, sl]).astype(BF16)
            for hh in range(2):
                vt_ref[0, 2 * c + hh, 0, 0:ATT_HD, :] = vt[hh * ATT_HD:(hh + 1) * ATT_HD, :]
                vt_ref[0, 2 * c + hh, 0, ATT_HD:V_ROWS, :] = ones_row


def _dsa_prep(u, u_small, cos, sin, qg, kg, *, tr, emit_keys):
    b, t, _ = u.shape
    nb = t // DSA_TQ
    nqb = tr // DSA_TQ
    full2 = lambda i, j: (0, 0)
    seg = (jnp.arange(LANES)[:, None] // ATT_HD == jnp.arange(LANES)[None, :] // ATT_HD).astype(BF16)
    out_specs = [
        pl.BlockSpec((1, tr, ATT_KV), lambda i, j: (i, j, 0)),
        pl.BlockSpec((1, tr, LANES), lambda i, j: (i, j, 0)),
        pl.BlockSpec((1, nqb, ATT_Q, DSA_TQ), lambda i, j: (i, j, 0, 0)),
        pl.BlockSpec((1, nqb, IDX_Q, DSA_TQ), lambda i, j: (i, j, 0, 0)),
        pl.BlockSpec((1, nqb, IDX_HEADS, DSA_TQ), lambda i, j: (i, j, 0, 0)),
    ]
    out_shape = [
        jax.ShapeDtypeStruct((b, t, ATT_KV), F32),
        jax.ShapeDtypeStruct((b, t, LANES), F32),
        jax.ShapeDtypeStruct((b, nb, ATT_Q, DSA_TQ), BF16),
        jax.ShapeDtypeStruct((b, nb, IDX_Q, DSA_TQ), BF16),
        jax.ShapeDtypeStruct((b, nb, IDX_HEADS, DSA_TQ), F32),
    ]
    if emit_keys:
        out_specs += [
            pl.BlockSpec((1, tr, ATT_KV), lambda i, j: (i, j, 0)),
            pl.BlockSpec((1, tr, IDX_HD), lambda i, j: (i, j, 0)),
            pl.BlockSpec((1, KV_HEADS, 1, V_ROWS, tr), lambda i, j: (i, 0, j, 0, 0)),
        ]
        out_shape += [
            jax.ShapeDtypeStruct((b, t, ATT_KV), BF16),
            jax.ShapeDtypeStruct((b, t, IDX_HD), BF16),
            jax.ShapeDtypeStruct((b, KV_HEADS, t // tr, V_ROWS, tr), BF16),
        ]
    return pl.pallas_call(
        functools.partial(_dsa_prep_kernel, tr=tr),
        grid=(b, t // tr),
        in_specs=[
            pl.BlockSpec((1, tr, ATT_Q), lambda i, j: (i, j, 0)),
            pl.BlockSpec((1, tr, ATT_KV), lambda i, j: (i, j, 4)),
            pl.BlockSpec((1, tr, ATT_KV), lambda i, j: (i, j, 5)),
            pl.BlockSpec((1, tr, IDX_Q), lambda i, j: (i, j, 3)),
            pl.BlockSpec((1, tr, LANES), lambda i, j: (i, j, 0)),
            pl.BlockSpec((tr, LANES), lambda i, j: (j, 0)),
            pl.BlockSpec((tr, LANES), lambda i, j: (j, 0)),
            pl.BlockSpec((1, LANES), full2),
            pl.BlockSpec((1, LANES), full2),
            pl.BlockSpec((LANES, LANES), full2),
        ],
        out_specs=out_specs,
        out_shape=out_shape,
        compiler_params=_cparams(("parallel", "parallel")),
        name="dsa_prep",
    )(u, u, u, u, u_small, cos, sin, qg, kg, seg)


def _fold8(x, op):
    r, n = x.shape
    x = x.reshape(r // 8, 8, n)
    while x.shape[0] > 1:
        h = x.shape[0] // 2
        y = op(x[:h], x[h:2 * h])
        x = y if x.shape[0] % 2 == 0 else jnp.concatenate([y, x[2 * h:]], axis=0)
    return x[0]


def _dsa_kernel(qit_ref, wit_ref, ki_ref, qt_ref, k_ref, vt_ref, o_ref,
                rhs_i_ref, rhs_q_ref, key_ref, bias_ref, s_ref, acc_ref, mm_ref,
                *, tq, cw, l_keys, start, nsel):
    i = pl.program_id(1)
    cols4 = ATT_GROUPS * tq
    last_pos = start + i * tq + (tq - 1)
    max_limit = jnp.minimum((lax.shift_right_logical(last_pos, 6) + 1) * CHUNK, l_keys)
    n_c = lax.div(max_limit + (cw - 1), cw)

    pos = start + i * tq + lax.broadcasted_iota(I32, (1, tq), 1)
    limit = jnp.minimum((lax.shift_right_logical(pos, 6) + 1) * CHUNK, l_keys)
    sub_pos = lax.broadcasted_iota(I32, (cw, 1), 0)

    for h in range(IDX_HEADS):
        rhs_i_ref[:, h * tq:(h + 1) * tq] = qit_ref[0, 0, h * IDX_HD:(h + 1) * IDX_HD, :]

    @pl.when(i == 0)
    def _():
        rhs_q_ref[...] = jnp.zeros_like(rhs_q_ref)

    for j in range(KV_HEADS):
        for g in range(ATT_GROUPS):
            hq = ATT_GROUPS * j + g
            rhs_q_ref[j, j * ATT_HD:(j + 1) * ATT_HD, g * tq:(g + 1) * tq] = \
                qt_ref[0, 0, hq * ATT_HD:(hq + 1) * ATT_HD, :]

    w = [wit_ref[0, 0, h:h + 1, :] * IDX_SCALE for h in range(IDX_HEADS)]

    def score_chunk(c, carry):
        d = jnp.maximum(_dot(ki_ref[0, c], rhs_i_ref[...]), 0.0)
        score = d[:, 0:tq] * w[0]
        for h in range(1, IDX_HEADS):
            score = score + d[:, h * tq:(h + 1) * tq] * w[h]
        bits = pltpu.bitcast(score, I32)
        key = jnp.where(bits < 0, bits ^ 0x7FFFFFFF, bits)
        key = jnp.where(score == 0.0, 0, key)
        key_ref[c] = jnp.where(sub_pos + c * cw < limit, key, INT_MIN)
        return carry

    lax.fori_loop(0, n_c, score_chunk, 0)

    def count(pred):
        def body(c, acc):
            return acc + _fold8(jnp.where(pred(key_ref[c], c * cw), 1.0, 0.0), jnp.add)

        acc = lax.fori_loop(0, n_c, body, jnp.zeros((8, tq), F32))
        return jnp.sum(acc, axis=0, keepdims=True)

    def search(it, lo):
        cand = lo + lax.shift_left(jnp.int32(1), 31 - it)
        cnt = count(lambda k, base: k >= cand)
        return jnp.where(cnt >= nsel, cand, lo)

    thr = lax.fori_loop(0, 32, search, jnp.full((1, tq), INT_MIN, I32))

    c_gt = count(lambda k, base: k > thr)
    n_eq = count(lambda k, base: k == thr)
    need = nsel - c_gt
    excess = jnp.where((n_eq > need) & (thr > INT_MIN), 1.0, 0.0)
    mm_ref[...] = jnp.full((1, tq), 1 << 14, I32)

    @pl.when(jnp.max(excess) > 0.0)
    def _():
        def tie_search(it, m):
            cand = m + lax.shift_left(jnp.int32(1), 13 - it)
            f = count(lambda k, base: (k == thr) & (sub_pos + base < cand))
            return jnp.where(f < need, cand, m)

        mm_ref[...] = lax.fori_loop(0, 14, tie_search, jnp.zeros((1, tq), I32))

    mm = mm_ref[...]

    def bias_chunk(c, carry):
        k = key_ref[c]
        kpos = sub_pos + c * cw
        sel = ((k > thr) | ((k == thr) & (kpos <= mm))) & (kpos < limit)
        bias_ref[c] = jnp.where(sel, 0.0, NEG_BIG)
        return carry

    lax.fori_loop(0, n_c, bias_chunk, 0)

    scale = (ATT_HD ** -0.5) * math.log2(math.e)
    m0 = jnp.full((8, cols4), -3e38, F32)
    for j0 in range(0, KV_HEADS, 2):
        pair = (j0, j0 + 1)

        def qk_chunk(c, ms, pair=pair):
            b1 = bias_ref[c]
            b4 = jnp.concatenate([b1] * ATT_GROUPS, axis=1)
            out = []
            for jj, j in enumerate(pair):
                s = _dot(k_ref[0, c], rhs_q_ref[j]) * scale + b4
                s_ref[jj, c] = s
                out.append(jnp.maximum(ms[jj], _fold8(s, jnp.maximum)))
            return tuple(out)

        ms = lax.fori_loop(0, n_c, qk_chunk, (m0, m0))
        ms = [jnp.max(m, axis=0, keepdims=True) for m in ms]
        acc_ref[...] = jnp.zeros_like(acc_ref)

        def pv_chunk(c, carry, pair=pair, ms=ms):
            for jj, j in enumerate(pair):
                p = jnp.exp2(s_ref[jj, c] - ms[jj])
                acc_ref[jj] += _dot(vt_ref[0, j, c], p.astype(BF16))
            return carry

        lax.fori_loop(0, n_c, pv_chunk, 0)
        for jj, j in enumerate(pair):
            a = acc_ref[jj]
            o = a[0:ATT_HD] / a[ATT_HD:ATT_HD + 1]
            for g in range(0, ATT_GROUPS, 2):
                two = jnp.concatenate([o[:, g * tq:(g + 1) * tq], o[:, (g + 1) * tq:(g + 2) * tq]], axis=0)
                lo = (ATT_GROUPS * j + g) * ATT_HD
                o_ref[0, :, lo:lo + 2 * ATT_HD] = jnp.transpose(two).astype(o_ref.dtype)


def _dsa(qit, wit, ki, qt, k, vt, *, tq, cw, l_keys, start):
    b, nb = qit.shape[0], qit.shape[1]
    nc = k.shape[1]
    nsel = min(TOPK_MAX, l_keys // 4)
    kern = functools.partial(_dsa_kernel, tq=tq, cw=cw, l_keys=l_keys, start=start, nsel=float(nsel))
    cols4 = ATT_GROUPS * tq
    return pl.pallas_call(
        kern,
        grid=(b, nb),
        in_specs=[
            pl.BlockSpec((1, 1, IDX_Q, tq), lambda i, j: (i, j, 0, 0)),
            pl.BlockSpec((1, 1, IDX_HEADS, tq), lambda i, j: (i, j, 0, 0)),
            pl.BlockSpec((1, nc, cw, IDX_HD), lambda i, j: (i, 0, 0, 0)),
            pl.BlockSpec((1, 1, ATT_Q, tq), lambda i, j: (i, j, 0, 0)),
            pl.BlockSpec((1, nc, cw, ATT_KV), lambda i, j: (i, 0, 0, 0)),
            pl.BlockSpec((1, KV_HEADS, nc, V_ROWS, cw), lambda i, j: (i, 0, 0, 0, 0)),
        ],
        out_specs=pl.BlockSpec((1, tq, ATT_Q), lambda i, j: (i, j, 0)),
        out_shape=jax.ShapeDtypeStruct((b, nb * tq, ATT_Q), BF16),
        scratch_shapes=[
            pltpu.VMEM((IDX_HD, IDX_HEADS * tq), BF16),
            pltpu.VMEM((KV_HEADS, ATT_KV, cols4), BF16),
            pltpu.VMEM((nc, cw, tq), I32),
            pltpu.VMEM((nc, cw, tq), F32),
            pltpu.VMEM((2, nc, cw, cols4), F32),
            pltpu.VMEM((2, V_ROWS, cols4), F32),
            pltpu.VMEM((1, tq), I32),
        ],
        compiler_params=_cparams(("parallel", "arbitrary")),
        name="dsa",
    )(qit, wit, ki, qt, k, vt)


def _dsa_key_layouts(k_all, v_all, ki_all, cw):
    b, l_keys, _ = k_all.shape
    nc = -(-l_keys // cw)
    padk = lambda a: jnp.pad(a, ((0, 0), (0, nc * cw - l_keys), (0, 0))).astype(BF16)
    vt = padk(v_all).reshape(b, nc, cw, KV_HEADS, ATT_HD).transpose(0, 3, 1, 4, 2)
    ones = jnp.ones((b, KV_HEADS, nc, 1, cw), BF16)
    zeros = jnp.zeros((b, KV_HEADS, nc, V_ROWS - ATT_HD - 1, cw), BF16)
    return (padk(ki_all).reshape(b, nc, cw, IDX_HD), padk(k_all).reshape(b, nc, cw, ATT_KV),
            jnp.concatenate([vt, ones, zeros], axis=3))


def _prep_weights(ssd_w_in, att_w_in, ssd_w_out, att_w_out, mem_w_kv, dense_w_gate, dense_w_up,
                  dense_w_down, moe_router, moe_w_gate, moe_w_up, moe_w_down):
    bf = lambda w: w.astype(BF16)
    w = ssd_w_in[0]
    o_xbc, o_dt, o_mq = D_INNER, D_INNER + CONV_DIM, D_INNER + CONV_DIM + SSD_HEADS
    ssd_main = bf(jnp.concatenate([w[:, o_xbc:o_dt], w[:, o_mq:], w[:, :D_INNER]], axis=1))
    ssd_small = bf(jnp.pad(w[:, o_dt:o_mq], ((0, 0), (0, LANES - SSD_HEADS))))
    w = att_w_in[0]
    o_wi = ATT_Q + 2 * ATT_KV + IDX_Q
    o_ki = o_wi + IDX_HEADS
    o_mq = o_ki + IDX_HD
    att_main = bf(jnp.concatenate([w[:, :o_wi], w[:, o_mq:]], axis=1))
    att_small = bf(jnp.pad(jnp.concatenate([w[:, o_ki:o_mq], w[:, o_wi:o_ki]], axis=1),
                           ((0, 0), (0, LANES - IDX_HD - IDX_HEADS))))
    r = jnp.pad(moe_router[0], ((0, 0), (0, LANES - N_EXPERTS)))
    r_hi = bf(r)
    r_lo = bf(r - r_hi.astype(F32))
    return dict(
        ssd_main=ssd_main, ssd_small=ssd_small, att_main=att_main, att_small=att_small,
        ssd_out_a=bf(ssd_w_out[0, :D_INNER]), ssd_out_b=bf(ssd_w_out[0, D_INNER:]),
        att_out_a=bf(att_w_out[0, :ATT_Q]), att_out_b=bf(att_w_out[0, ATT_Q:]),
        mem_w_kv=bf(mem_w_kv), dense_g=bf(dense_w_gate[0]), dense_u=bf(dense_w_up[0]),
        dense_d=bf(dense_w_down[0]), r_hi=r_hi, r_lo=r_lo,
        moe_g=bf(moe_w_gate[0]), moe_u=bf(moe_w_up[0]), moe_d=bf(moe_w_down[0]))


def _rope_tables(pos):
    half = ATT_HD // 2
    inv = ROPE_THETA ** (-jnp.arange(half, dtype=F32) / half)
    ang = pos.astype(F32)[:, None] * inv[None, :]
    cos = jnp.cos(ang)
    sin = jnp.sin(ang)
    cos_t = jnp.concatenate([cos, cos, cos, cos], axis=1)
    sin_t = jnp.concatenate([-sin, sin, -sin, sin], axis=1)
    return cos_t, sin_t


def _trunk(x, start, mem_k, mem_v, conv_in, ssm_in, kv_in, P, W, q_ssd, cw_dsa):
    b, t, _ = x.shape
    n = b * t
    x2 = x.reshape(n, D_MODEL)

    u = _norm_matmul(x2, W['ssd_norm'][0], P['ssd_main'], tn=1024).reshape(b, t, -1)
    u_small = _norm_matmul(x2, W['ssd_norm'][0], P['ssd_small'], tn=LANES).reshape(b, t, LANES)
    conv8 = jnp.pad(conv_in, ((0, 0), (8 - (CONV_W - 1), 0), (0, 0)))
    h0_t = ssm_in.reshape(b, SSD_GROUPS, SSD_HPG, SSD_HEADDIM, D_STATE).transpose(0, 1, 4, 2, 3)
    h0_t = h0_t.reshape(b, SSD_GROUPS, D_STATE, SSD_HPG * SSD_HEADDIM)
    cw8 = jnp.pad(W['ssd_conv_w'][0], ((0, 8 - CONV_W), (0, 0)))
    pad_h = lambda v: jnp.pad(v.astype(F32), (0, LANES - SSD_HEADS)).reshape(1, LANES)
    e_mat = (jnp.arange(LANES)[:, None] == (jnp.arange(D_INNER)[None, :] // SSD_HEADDIM)).astype(BF16)
    y_mix, h_t = _ssd(u, u_small, conv8, h0_t, cw8, W['ssd_conv_b'][0].reshape(1, CONV_DIM),
                      pad_h(W['ssd_dt_bias'][0]), pad_h(W['ssd_A_log'][0]),
                      jnp.repeat(W['ssd_D'][0].astype(F32), SSD_HEADDIM).reshape(1, D_INNER),
                      W['ssd_out_norm'][0].reshape(1, D_INNER), e_mat, q_ssd)
    new_conv = u[:, t - (CONV_W - 1):, :CONV_DIM]
    new_ssm = h_t.reshape(b, SSD_GROUPS, D_STATE, SSD_HPG, SSD_HEADDIM).transpose(0, 1, 3, 4, 2)
    new_ssm = new_ssm.reshape(b, SSD_HEADS, SSD_HEADDIM, D_STATE)
    y_mem = _mem_attn(u, 3, mem_k[0], mem_v[0], W['mem_q_norm'][0])
    x2 = _proj_res(y_mix.reshape(n, D_INNER), y_mem.reshape(n, MEM_WIDTH), P['ssd_out_a'], P['ssd_out_b'], x2)
    x2 = _ffn(x2, W['ffn_norm'][0], P['dense_g'], P['dense_u'], P['dense_d'])

    u = _norm_matmul(x2, W['att_norm'][0], P['att_main'], tn=1024).reshape(b, t, -1)
    u_small = _norm_matmul(x2, W['att_norm'][0], P['att_small'], tn=LANES).reshape(b, t, LANES)
    pos = start + jnp.arange(t)
    cos_t, sin_t = _rope_tables(pos)
    tile2 = lambda v: jnp.tile(v.astype(F32), 2).reshape(1, LANES)
    v_new = u[:, :, ATT_Q + ATT_KV:ATT_Q + 2 * ATT_KV]
    qg, kg = tile2(W['att_q_norm'][0]), tile2(W['att_k_norm'][0])
    if kv_in is None:
        k_rot, ki_rot, qt, qit, wit, k_b, ki_b, vt = _dsa_prep(u, u_small, cos_t, sin_t, qg, kg,
                                                               tr=cw_dsa, emit_keys=True)
        nc = t // cw_dsa
        keys = (ki_b.reshape(b, nc, cw_dsa, IDX_HD), k_b.reshape(b, nc, cw_dsa, ATT_KV), vt)
        ki_new = ki_rot[:, :, :IDX_HD]
        l_keys = t
    else:
        padt = lambda a: jnp.pad(a, ((0, DSA_TQ - t),) + ((0, 0),) * (a.ndim - 1))
        padbt = lambda a: jnp.pad(a, ((0, 0), (0, DSA_TQ - t), (0, 0)))
        k_rot, ki_rot, qt, qit, wit = _dsa_prep(padbt(u), padbt(u_small), padt(cos_t), padt(sin_t), qg, kg,
                                                tr=DSA_TQ, emit_keys=False)
        k_rot, ki_rot = k_rot[:, :t], ki_rot[:, :t]
        ki_new = ki_rot[:, :, :IDX_HD]
        keys = _dsa_key_layouts(jnp.concatenate([kv_in[0], k_rot], axis=1),
                                jnp.concatenate([kv_in[1], v_new], axis=1),
                                jnp.concatenate([kv_in[2], ki_new], axis=1), cw_dsa)
        l_keys = kv_in[0].shape[1] + t
    o_t = _dsa(qit, wit, keys[0], qt, keys[1], keys[2], tq=DSA_TQ, cw=cw_dsa, l_keys=l_keys, start=start)
    y_mix = o_t[:, :t].reshape(n, ATT_Q)
    y_mem = _mem_attn(u, 2, mem_k[1], mem_v[1], W['mem_q_norm'][1])
    x2 = _proj_res(y_mix, y_mem.reshape(n, MEM_WIDTH), P['att_out_a'], P['att_out_b'], x2)
    x2 = _moe(x2, W['ffn_norm'][1], P['r_hi'], P['r_lo'], P['moe_g'], P['moe_u'], P['moe_d'])

    return (x2.reshape(b, t, D_MODEL), new_conv, new_ssm, k_rot.reshape(b, t, KV_HEADS, ATT_HD),
            v_new.reshape(b, t, KV_HEADS, ATT_HD), ki_new)


def kernel(x_prompt, x_sample, mem_prompt, cache_conv, state_ssm, cache_k, cache_v, cache_idx_k, cache_mem_k, cache_mem_v, ssd_norm, ssd_w_in, ssd_conv_w, ssd_conv_b, ssd_dt_bias, ssd_A_log, ssd_D, ssd_out_norm, ssd_w_out, att_norm, att_w_in, att_q_norm, att_k_norm, att_w_out, mem_norm, mem_w_kv, mem_q_norm, mem_k_norm, ffn_norm, dense_w_gate, dense_w_up, dense_w_down, moe_router, moe_w_gate, moe_w_up, moe_w_down):
    W = dict(ssd_norm=ssd_norm, ssd_conv_w=ssd_conv_w, ssd_conv_b=ssd_conv_b, ssd_dt_bias=ssd_dt_bias,
             ssd_A_log=ssd_A_log, ssd_D=ssd_D, ssd_out_norm=ssd_out_norm, att_norm=att_norm,
             att_q_norm=att_q_norm, att_k_norm=att_k_norm, mem_q_norm=mem_q_norm, ffn_norm=ffn_norm)
    P = _prep_weights(ssd_w_in, att_w_in, ssd_w_out, att_w_out, mem_w_kv, dense_w_gate, dense_w_up,
                      dense_w_down, moe_router, moe_w_gate, moe_w_up, moe_w_down)
    bp, sp = x_prompt.shape[0], x_prompt.shape[1]
    bs = x_sample.shape[0]

    mem2 = mem_prompt.reshape(bp * N_MEM, D_MODEL)
    pk, pv = [], []
    for i in range(2):
        kv = _norm_matmul(mem2, mem_norm[i], P['mem_w_kv'][i], tn=1024)
        pk.append(_head_norm(kv[:, :MEM_WIDTH], mem_k_norm[i]).reshape(bp, N_MEM, MEM_WIDTH))
        pv.append(kv[:, MEM_WIDTH:].reshape(bp, N_MEM, MEM_WIDTH))
    p_mem_k = jnp.stack(pk)
    p_mem_v = jnp.stack(pv)

    conv0 = jnp.zeros((bp, CONV_W - 1, CONV_DIM), F32)
    ssm0 = jnp.zeros((bp, SSD_HEADS, SSD_HEADDIM, D_STATE), F32)
    y_p, p_conv, p_ssm, p_k, p_v, p_ki = _trunk(x_prompt, 0, p_mem_k, p_mem_v, conv0, ssm0, None, P, W,
                                                q_ssd=128, cw_dsa=512)
    past = cache_k.shape[2]
    kv_in = (cache_k[0].reshape(bs, past, ATT_KV), cache_v[0].reshape(bs, past, ATT_KV), cache_idx_k[0])
    y_s, s_conv, s_ssm, s_k, s_v, s_ki = _trunk(
        x_sample, past, cache_mem_k.reshape(2, bs, N_MEM, MEM_WIDTH), cache_mem_v.reshape(2, bs, N_MEM, MEM_WIDTH),
        cache_conv[0], state_ssm[0], kv_in, P, W, q_ssd=x_sample.shape[1], cw_dsa=384)

    shp = (bp, N_MEM, MEM_HEADS, MEM_HD)
    return (y_p, y_s, p_conv[None], p_ssm[None], p_k[None], p_v[None], p_ki[None],
            p_mem_k.reshape((2,) + shp), p_mem_v.reshape((2,) + shp),
            s_conv[None], s_ssm[None], s_k[None], s_v[None], s_ki[None])
```

```python
import functools
import math

import jax
import jax.numpy as jnp
from jax import lax
from jax.experimental import pallas as pl
from jax.experimental.pallas import tpu as pltpu

F32 = jnp.float32
BF16 = jnp.bfloat16
I32 = jnp.int32

D_MODEL = 1024
CHUNK = 64
N_MEM = 256
EPS = 1e-6
D_INNER = 2048
SSD_HEADDIM = 64
SSD_HEADS = 32
SSD_GROUPS = 4
SSD_HPG = 8
D_STATE = 128
CONV_W = 4
GN = SSD_GROUPS * D_STATE
CONV_DIM = D_INNER + 2 * GN
ATT_HEADS = 16
ATT_HD = 64
KV_HEADS = 4
ATT_GROUPS = 4
ATT_Q = 1024
ATT_KV = 256
IDX_HEADS = 8
IDX_HD = 64
IDX_Q = 512
IDX_SCALE = (IDX_HEADS * IDX_HD) ** -0.5
TOPK_MAX = 256
ROPE_THETA = 10000.0
MEM_HEADS = 4
MEM_HD = 256
MEM_WIDTH = 1024
D_FF = 3584
N_EXPERTS = 8

LANES = 128
INT_MIN = -(2 ** 31)
NEG_BIG = -1e30
VMEM_LIMIT = 56 * 1024 * 1024


def _cparams(sem):
    return pltpu.CompilerParams(dimension_semantics=sem, vmem_limit_bytes=VMEM_LIMIT)


def _dot(a, b):
    return jnp.dot(a, b, preferred_element_type=F32)


def _split3(v):
    hi = v.astype(BF16)
    r1 = v - hi.astype(F32)
    mid = r1.astype(BF16)
    lo = (r1 - mid.astype(F32)).astype(BF16)
    return hi, mid, lo


def _silu(x):
    return x * (0.5 * jnp.tanh(0.5 * x) + 0.5)


def _norm_matmul_kernel(x_ref, g_ref, w_ref, o_ref, h_ref):
    @pl.when(pl.program_id(1) == 0)
    def _():
        x = x_ref[...]
        ms = jnp.mean(x * x, axis=-1, keepdims=True)
        h_ref[...] = (x * lax.rsqrt(ms + EPS) * g_ref[...]).astype(BF16)

    o_ref[...] = _dot(h_ref[...], w_ref[...]).astype(o_ref.dtype)


def _norm_matmul(x, g, w, tn, out_dtype=F32):
    n, k = x.shape
    m = w.shape[1]
    tm = min(1024, n)
    return pl.pallas_call(
        _norm_matmul_kernel,
        grid=(n // tm, m // tn),
        in_specs=[
            pl.BlockSpec((tm, k), lambda i, j: (i, 0)),
            pl.BlockSpec((1, k), lambda i, j: (0, 0)),
            pl.BlockSpec((k, tn), lambda i, j: (0, j)),
        ],
        out_specs=pl.BlockSpec((tm, tn), lambda i, j: (i, j)),
        out_shape=jax.ShapeDtypeStruct((n, m), out_dtype),
        scratch_shapes=[pltpu.VMEM((tm, k), BF16)],
        compiler_params=_cparams(("parallel", "arbitrary")),
        name="norm_matmul",
    )(x, g.reshape(1, k), w)


def _head_norm_kernel(x_ref, g_ref, o_ref):
    for h in range(MEM_HEADS):
        x = x_ref[:, h * MEM_HD:(h + 1) * MEM_HD]
        ms = jnp.mean(x * x, axis=-1, keepdims=True)
        o_ref[:, h * MEM_HD:(h + 1) * MEM_HD] = x * lax.rsqrt(ms + EPS) * g_ref[...]


def _head_norm(x, g):
    n = x.shape[0]
    tm = min(512, n)
    return pl.pallas_call(
        _head_norm_kernel,
        grid=(n // tm,),
        in_specs=[pl.BlockSpec((tm, MEM_WIDTH), lambda i: (i, 0)),
                  pl.BlockSpec((1, MEM_HD), lambda i: (0, 0))],
        out_specs=pl.BlockSpec((tm, MEM_WIDTH), lambda i: (i, 0)),
        out_shape=jax.ShapeDtypeStruct((n, MEM_WIDTH), F32),
        compiler_params=_cparams(("parallel",)),
        name="head_norm",
    )(x, g.reshape(1, MEM_HD))


def _mem_attn_kernel(q_ref, k_ref, v_ref, g_ref, o_ref):
    for h in range(MEM_HEADS):
        sl = slice(h * MEM_HD, (h + 1) * MEM_HD)
        q = q_ref[0, :, sl]
        ms = jnp.mean(q * q, axis=-1, keepdims=True)
        qn = (q * lax.rsqrt(ms + EPS) * g_ref[...]).astype(BF16)
        k = k_ref[0, :, sl].astype(BF16)
        s = lax.dot_general(qn, k, (((1,), (1,)), ((), ())), preferred_element_type=F32)
        s = s * (MEM_HD ** -0.5)
        m = jnp.max(s, axis=-1, keepdims=True)
        p = jnp.exp(s - m)
        p = p / jnp.sum(p, axis=-1, keepdims=True)
        o = _dot(p.astype(BF16), v_ref[0, :, sl].astype(BF16))
        o_ref[0, :, sl] = o.astype(o_ref.dtype)


def _mem_attn(u, col_block, mk, mv, g):
    b, t, _ = u.shape
    tq = min(512, t)
    return pl.pallas_call(
        _mem_attn_kernel,
        grid=(b, t // tq),
        in_specs=[
            pl.BlockSpec((1, tq, MEM_WIDTH), lambda i, j: (i, j, col_block)),
            pl.BlockSpec((1, N_MEM, MEM_WIDTH), lambda i, j: (i, 0, 0)),
            pl.BlockSpec((1, N_MEM, MEM_WIDTH), lambda i, j: (i, 0, 0)),
            pl.BlockSpec((1, MEM_HD), lambda i, j: (0, 0)),
        ],
        out_specs=pl.BlockSpec((1, tq, MEM_WIDTH), lambda i, j: (i, j, 0)),
        out_shape=jax.ShapeDtypeStruct((b, t, MEM_WIDTH), BF16),
        compiler_params=_cparams(("parallel", "parallel")),
        name="mem_attn",
    )(u, mk, mv, g.reshape(1, MEM_HD))


def _proj_res_kernel(a_ref, b_ref, wa_ref, wb_ref, x_ref, o_ref):
    o_ref[...] = x_ref[...] + _dot(a_ref[...], wa_ref[...]) + _dot(b_ref[...], wb_ref[...])


def _proj_res(a, b, wa, wb, x):
    n = x.shape[0]
    tm = min(512, n)
    ka, kb = a.shape[1], b.shape[1]
    return pl.pallas_call(
        _proj_res_kernel,
        grid=(n // tm,),
        in_specs=[
            pl.BlockSpec((tm, ka), lambda i: (i, 0)),
            pl.BlockSpec((tm, kb), lambda i: (i, 0)),
            pl.BlockSpec((ka, D_MODEL), lambda i: (0, 0)),
            pl.BlockSpec((kb, D_MODEL), lambda i: (0, 0)),
            pl.BlockSpec((tm, D_MODEL), lambda i: (i, 0)),
        ],
        out_specs=pl.BlockSpec((tm, D_MODEL), lambda i: (i, 0)),
        out_shape=jax.ShapeDtypeStruct((n, D_MODEL), F32),
        compiler_params=_cparams(("parallel",)),
        name="proj_res",
    )(a, b, wa, wb, x)


def _ffn_kernel(x_ref, g_ref, wg_ref, wu_ref, wd_ref, o_ref, h_ref, acc_ref):
    f = pl.program_id(1)

    @pl.when(f == 0)
    def _():
        x = x_ref[...]
        ms = jnp.mean(x * x, axis=-1, keepdims=True)
        h_ref[...] = (x * lax.rsqrt(ms + EPS) * g_ref[...]).astype(BF16)
        acc_ref[...] = x

    h = h_ref[...]
    a = _silu(_dot(h, wg_ref[...])) * _dot(h, wu_ref[...])
    acc_ref[...] += _dot(a.astype(BF16), wd_ref[...])

    @pl.when(f == pl.num_programs(1) - 1)
    def _():
        o_ref[...] = acc_ref[...]


def _ffn(x, g, wg, wu, wd, tf=512):
    n = x.shape[0]
    tm = min(1024, n)
    return pl.pallas_call(
        _ffn_kernel,
        grid=(n // tm, D_FF // tf),
        in_specs=[
            pl.BlockSpec((tm, D_MODEL), lambda i, f: (i, 0)),
            pl.BlockSpec((1, D_MODEL), lambda i, f: (0, 0)),
            pl.BlockSpec((D_MODEL, tf), lambda i, f: (0, f)),
            pl.BlockSpec((D_MODEL, tf), lambda i, f: (0, f)),
            pl.BlockSpec((tf, D_MODEL), lambda i, f: (f, 0)),
        ],
        out_specs=pl.BlockSpec((tm, D_MODEL), lambda i, f: (i, 0)),
        out_shape=jax.ShapeDtypeStruct((n, D_MODEL), F32),
        scratch_shapes=[pltpu.VMEM((tm, D_MODEL), BF16), pltpu.VMEM((tm, D_MODEL), F32)],
        compiler_params=_cparams(("parallel", "arbitrary")),
        name="ffn",
    )(x, g.reshape(1, D_MODEL), wg, wu, wd)


MOE_ROWS = 256


def _split2(v):
    hi = v.astype(BF16)
    return hi, (v - hi.astype(F32)).astype(BF16)


def _moe_route_kernel(x_ref, g_ref, rh_ref, rl_ref, h_ref, gate_ref):
    lane = lax.broadcasted_iota(I32, (1, LANES), 1)
    x = x_ref[...]
    ms = jnp.mean(x * x, axis=-1, keepdims=True)
    hf = x * lax.rsqrt(ms + EPS) * g_ref[...]
    hb, hl = _split2(hf)
    h_ref[...] = hb
    logits = _dot(hb, rh_ref[...]) + _dot(hl, rh_ref[...]) + _dot(hb, rl_ref[...])
    valid = lane < N_EXPERTS
    logits = jnp.where(valid, logits, NEG_BIG)
    m = jnp.max(logits, axis=-1, keepdims=True)
    p = jnp.exp(logits - m)
    p = p / jnp.sum(p, axis=-1, keepdims=True)
    p = jnp.where(valid, p, -1.0)
    v1 = jnp.max(p, axis=-1, keepdims=True)
    i1 = jnp.min(jnp.where(p == v1, lane, LANES), axis=-1, keepdims=True)
    p2 = jnp.where(lane == i1, -1.0, p)
    v2 = jnp.max(p2, axis=-1, keepdims=True)
    i2 = jnp.min(jnp.where(p2 == v2, lane, LANES), axis=-1, keepdims=True)
    den = v1 + v2
    gate_ref[...] = jnp.where(lane == i1, v1 / den, jnp.where(lane == i2, v2 / den, 0.0))


def _moe_route(x, g, r_hi, r_lo):
    n = x.shape[0]
    tm = min(512, n)
    return pl.pallas_call(
        _moe_route_kernel,
        grid=(n // tm,),
        in_specs=[pl.BlockSpec((tm, D_MODEL), lambda i: (i, 0)),
                  pl.BlockSpec((1, D_MODEL), lambda i: (0, 0)),
                  pl.BlockSpec((D_MODEL, LANES), lambda i: (0, 0)),
                  pl.BlockSpec((D_MODEL, LANES), lambda i: (0, 0))],
        out_specs=[pl.BlockSpec((tm, D_MODEL), lambda i: (i, 0)),
                   pl.BlockSpec((tm, LANES), lambda i: (i, 0))],
        out_shape=[jax.ShapeDtypeStruct((n, D_MODEL), BF16), jax.ShapeDtypeStruct((n, LANES), F32)],
        compiler_params=_cparams(("parallel",)),
        name="moe_route",
    )(x, g.reshape(1, D_MODEL), r_hi, r_lo)


def _moe_ffn_kernel(h_ref, gate_ref, x_ref, wg_ref, wu_ref, wd_ref, o_ref,
                    rank_ref, rank_t_ref, sel_t_ref, gate_t_ref, xe_ref, acc_ref, nb_ref, *, tm):
    e = pl.program_id(1)
    f = pl.program_id(2)
    nf = pl.num_programs(2)
    big, small = MOE_ROWS, MOE_ROWS // 2
    lane = lax.broadcasted_iota(I32, (1, LANES), 1)

    @pl.when((e == 0) & (f == 0))
    def _():
        o_ref[...] = x_ref[...]
        gates = gate_ref[...]
        sel = jnp.where(gates > 0.0, 1.0, 0.0)
        r = lax.broadcasted_iota(I32, (tm, tm), 0)
        c = lax.broadcasted_iota(I32, (tm, tm), 1)
        below = jnp.where(c < r, 1.0, 0.0).astype(BF16)
        rank = _dot(below, sel.astype(BF16))
        rank_ref[...] = rank
        rank_t_ref[...] = jnp.transpose(rank)
        sel_t_ref[...] = jnp.transpose(sel)
        gate_t_ref[...] = jnp.transpose(gates)

    @pl.when(f == 0)
    def _():
        cnt = jnp.sum(jnp.where(lane == e, jnp.sum(jnp.where(gate_ref[...] > 0.0, 1.0, 0.0), axis=0, keepdims=True),
                                0.0)).astype(I32)
        n_big = lax.div(cnt, big)
        rem = cnt - n_big * big
        nb_ref[0] = n_big + jnp.where(rem > small, 1, 0)
        nb_ref[1] = jnp.where((rem > 0) & (rem <= small), 1, 0)

    n_big = nb_ref[0]
    has_small = nb_ref[1] > 0
    r_small = pl.multiple_of(n_big * big, big)

    def pick(r0, rows):
        want = (lax.broadcasted_iota(I32, (rows, 1), 0) + r0).astype(F32)
        return (rank_t_ref[pl.ds(e, 1), :] == want) & (sel_t_ref[pl.ds(e, 1), :] > 0.0)

    def for_blocks(fn):
        def body(rb, carry):
            fn(pl.multiple_of(rb * big, big), big)
            return carry

        lax.fori_loop(0, n_big, body, 0)

        @pl.when(has_small)
        def _():
            fn(r_small, small)

    def gather(r0, rows):
        p = jnp.where(pick(r0, rows), 1.0, 0.0).astype(BF16)
        xe_ref[pl.ds(r0, rows), :] = _dot(p, h_ref[...]).astype(BF16)

    def ffn(r0, rows):
        xe = xe_ref[pl.ds(r0, rows), :]
        a = _silu(_dot(xe, wg_ref[0])) * _dot(xe, wu_ref[0])
        y = _dot(a.astype(BF16), wd_ref[0])

        @pl.when(f == 0)
        def _():
            acc_ref[pl.ds(r0, rows), :] = y

        @pl.when(f > 0)
        def _():
            acc_ref[pl.ds(r0, rows), :] += y

    @pl.when(f == 0)
    def _():
        for_blocks(gather)

    for_blocks(ffn)

    @pl.when(f == nf - 1)
    def _():
        rank_col = jnp.sum(jnp.where(lane == e, rank_ref[...], 0.0), axis=1, keepdims=True)
        sel_col = jnp.sum(jnp.where(lane == e, gate_ref[...], 0.0), axis=1, keepdims=True) > 0.0

        def scatter(r0, rows):
            g_rows = jnp.sum(jnp.where(pick(r0, rows), gate_t_ref[pl.ds(e, 1), :], 0.0), axis=1, keepdims=True)
            y = (acc_ref[pl.ds(r0, rows), :] * g_rows).astype(BF16)
            col = (lax.broadcasted_iota(I32, (1, rows), 1) + r0).astype(F32)
            put = jnp.where((rank_col == col) & sel_col, 1.0, 0.0).astype(BF16)
            o_ref[...] += _dot(put, y)

        for_blocks(scatter)


def _moe_ffn(h, gates, x, wg, wu, wd, *, tm=1024, tf=896):
    n = x.shape[0]
    tm = min(tm, n)
    kern = functools.partial(_moe_ffn_kernel, tm=tm)
    return pl.pallas_call(
        kern,
        grid=(n // tm, N_EXPERTS, D_FF // tf),
        in_specs=[
            pl.BlockSpec((tm, D_MODEL), lambda i, e, f: (i, 0)),
            pl.BlockSpec((tm, LANES), lambda i, e, f: (i, 0)),
            pl.BlockSpec((tm, D_MODEL), lambda i, e, f: (i, 0)),
            pl.BlockSpec((1, D_MODEL, tf), lambda i, e, f: (e, 0, f)),
            pl.BlockSpec((1, D_MODEL, tf), lambda i, e, f: (e, 0, f)),
            pl.BlockSpec((1, tf, D_MODEL), lambda i, e, f: (e, f, 0)),
        ],
        out_specs=pl.BlockSpec((tm, D_MODEL), lambda i, e, f: (i, 0)),
        out_shape=jax.ShapeDtypeStruct((n, D_MODEL), F32),
        scratch_shapes=[
            pltpu.VMEM((tm, LANES), F32),
            pltpu.VMEM((LANES, tm), F32),
            pltpu.VMEM((LANES, tm), F32),
            pltpu.VMEM((LANES, tm), F32),
            pltpu.VMEM((tm, D_MODEL), BF16),
            pltpu.VMEM((tm, D_MODEL), F32),
            pltpu.SMEM((2,), I32),
        ],
        compiler_params=_cparams(("parallel", "arbitrary", "arbitrary")),
        name="moe_ffn",
    )(h, gates, x, wg, wu, wd)


def _moe(x, g, r_hi, r_lo, wg, wu, wd):
    h, gates = _moe_route(x, g, r_hi, r_lo)
    return _moe_ffn(h, gates, x, wg, wu, wd)


def _ssd_kernel(xbc_ref, z_ref, dt_ref, cs_ref, h0_ref, cw_ref, cb_ref, dtb_ref, alog_ref,
                dexp_ref, og_ref, e_ref, y_ref, hout_ref, xpad_ref, h_ref, *, q):
    c = pl.program_id(1)

    @pl.when(c == 0)
    def _():
        xpad_ref[0:8, :] = cs_ref[0]
        h_ref[...] = h0_ref[0]

    xbc = xbc_ref[0]
    xpad_ref[8:8 + q, :] = xbc
    cw = cw_ref[...]
    conv = cb_ref[...] + (xpad_ref[5:5 + q, :] * cw[0:1] + xpad_ref[6:6 + q, :] * cw[1:2]
                          + xpad_ref[7:7 + q, :] * cw[2:3] + xbc * cw[3:4])
    xpad_ref[0:8, :] = xpad_ref[q:q + 8, :]
    act = _silu(conv)

    lane = lax.broadcasted_iota(I32, (1, LANES), 1)
    xdt = dt_ref[0] + dtb_ref[...]
    sp = jnp.maximum(xdt, 0.0) + jnp.log1p(jnp.exp(-jnp.abs(xdt)))
    dt = jnp.where(lane < SSD_HEADS, sp, 0.0)
    a = dt * (-jnp.exp(alog_ref[...]))

    rows = lax.broadcasted_iota(I32, (q, q), 0)
    cols = lax.broadcasted_iota(I32, (q, q), 1)
    causal = rows >= cols
    tril = jnp.where(causal, 1.0, 0.0).astype(BF16)
    a3 = _split3(a)
    acum = _dot(tril, a3[0]) + _dot(tril, a3[1]) + _dot(tril, a3[2])
    acum_t = jnp.transpose(acum)

    e_mat = e_ref[...]
    ac3 = _split3(acum)
    acum_x = _dot(ac3[0], e_mat) + _dot(ac3[1], e_mat) + _dot(ac3[2], e_mat)
    dt3 = _split3(dt)
    dt_x = _dot(dt3[0], e_mat) + _dot(dt3[1], e_mat) + _dot(dt3[2], e_mat)
    last = acum_x[q - 1:q, :]
    eac_x = jnp.exp(acum_x)
    dte_x = jnp.exp(last - acum_x)
    blkdec = jnp.exp(last)

    xs = act[:, :D_INNER]
    x_dt = xs * dt_x
    xb = x_dt.astype(BF16)
    xdb = (x_dt * dte_x).astype(BF16)
    z = z_ref[0]
    lane_lo = lane < SSD_HEADDIM

    gw = SSD_HPG * SSD_HEADDIM
    for g in range(SSD_GROUPS):
        bg = act[:, D_INNER + g * D_STATE:D_INNER + (g + 1) * D_STATE]
        cg = act[:, D_INNER + GN + g * D_STATE:D_INNER + GN + (g + 1) * D_STATE]
        cgb = cg.astype(BF16)
        bgt = jnp.transpose(bg).astype(BF16)
        cb = _dot(cgb, bgt)
        gs = slice(g * gw, (g + 1) * gw)
        h_in = h_ref[g]
        y_off = _dot(cgb, h_in.astype(BF16)) * eac_x[:, gs]
        h_ref[g] = h_in * blkdec[:, gs] + _dot(bgt, xdb[:, gs])
        parts = []
        for pr in range(SSD_HPG // 2):
            h0 = g * SSD_HPG + 2 * pr
            xp = xb[:, (h0 // 2) * LANES:(h0 // 2 + 1) * LANES]
            ys = []
            for hh in (h0, h0 + 1):
                seg = acum[:, hh:hh + 1] - acum_t[hh:hh + 1, :]
                lm = jnp.exp(jnp.where(causal, seg, NEG_BIG))
                ys.append(_dot((cb * lm).astype(BF16), xp))
            parts.append(jnp.where(lane_lo, ys[0], ys[1]))
        y_g = jnp.concatenate(parts, axis=1) + y_off + dexp_ref[:, gs] * xs[:, gs]
        gt = y_g * _silu(z[:, gs])
        gn = gt * lax.rsqrt(jnp.mean(gt * gt, axis=-1, keepdims=True) + EPS) * og_ref[:, gs]
        y_ref[0, :, gs] = gn.astype(y_ref.dtype)

    @pl.when(c == pl.num_programs(1) - 1)
    def _():
        hout_ref[0] = h_ref[...]


def _ssd(u, u_small, conv_state8, h0_t, cw8, cb, dtb, alog, dexp, og, e_mat, q):
    b, t, _ = u.shape
    kern = functools.partial(_ssd_kernel, q=q)
    full2 = lambda i, c: (0, 0)
    return pl.pallas_call(
        kern,
        grid=(b, t // q),
        in_specs=[
            pl.BlockSpec((1, q, CONV_DIM), lambda i, c: (i, c, 0)),
            pl.BlockSpec((1, q, D_INNER), lambda i, c: (i, c, 2)),
            pl.BlockSpec((1, q, LANES), lambda i, c: (i, c, 0)),
            pl.BlockSpec((1, 8, CONV_DIM), lambda i, c: (i, 0, 0)),
            pl.BlockSpec((1, SSD_GROUPS, D_STATE, SSD_HPG * SSD_HEADDIM), lambda i, c: (i, 0, 0, 0)),
            pl.BlockSpec((8, CONV_DIM), full2),
            pl.BlockSpec((1, CONV_DIM), full2),
            pl.BlockSpec((1, LANES), full2),
            pl.BlockSpec((1, LANES), full2),
            pl.BlockSpec((1, D_INNER), full2),
            pl.BlockSpec((1, D_INNER), full2),
            pl.BlockSpec((LANES, D_INNER), full2),
        ],
        out_specs=[
            pl.BlockSpec((1, q, D_INNER), lambda i, c: (i, c, 0)),
            pl.BlockSpec((1, SSD_GROUPS, D_STATE, SSD_HPG * SSD_HEADDIM), lambda i, c: (i, 0, 0, 0)),
        ],
        out_shape=[
            jax.ShapeDtypeStruct((b, t, D_INNER), BF16),
            jax.ShapeDtypeStruct((b, SSD_GROUPS, D_STATE, SSD_HPG * SSD_HEADDIM), F32),
        ],
        scratch_shapes=[pltpu.VMEM((q + 8, CONV_DIM), F32),
                        pltpu.VMEM((SSD_GROUPS, D_STATE, SSD_HPG * SSD_HEADDIM), F32)],
        compiler_params=_cparams(("parallel", "arbitrary")),
        name="ssd",
    )(u, u, u_small, conv_state8, h0_t, cw8, cb, dtb, alog, dexp, og, e_mat)


V_ROWS = 80
DSA_TQ = LANES
DSA_HPT = 4


def _dsa_prep_kernel(q_ref, k_ref, v_ref, qi_ref, sm_ref, cos_ref, sin_ref, qg_ref, kg_ref, seg_ref,
                     ko_ref, kio_ref, qt_ref, qit_ref, wit_ref, *key_refs, tr):
    cos = cos_ref[...]
    sin = sin_ref[...]
    seg = seg_ref[...]
    lane = lax.broadcasted_iota(I32, (1, LANES), 1)
    nqb = tr // DSA_TQ

    def head_norm(x):
        s3 = _split3(x * x)
        ss = _dot(s3[0], seg) + _dot(s3[1], seg) + _dot(s3[2], seg)
        return x * lax.rsqrt(ss * (1.0 / ATT_HD) + EPS)

    def rope(x):
        partner = jnp.where((lane & 32) == 0, pltpu.roll(x, LANES - 32, 1), pltpu.roll(x, 32, 1))
        return x * cos + partner * sin

    def put_t(dst_ref, row0, x):
        for qb in range(nqb):
            xt = jnp.transpose(x[qb * DSA_TQ:(qb + 1) * DSA_TQ, :])
            dst_ref[0, qb, row0:row0 + LANES, :] = xt.astype(dst_ref.dtype)

    for c in range(ATT_Q // LANES):
        sl = slice(c * LANES, (c + 1) * LANES)
        put_t(qt_ref, c * LANES, rope(head_norm(q_ref[0, :, sl]) * qg_ref[...]))
    for c in range(IDX_Q // LANES):
        sl = slice(c * LANES, (c + 1) * LANES)
        put_t(qit_ref, c * LANES, rope(qi_ref[0, :, sl]))
    sm = sm_ref[0]
    for qb in range(nqb):
        wit_ref[0, qb] = jnp.transpose(sm[qb * DSA_TQ:(qb + 1) * DSA_TQ, :])[IDX_HD:IDX_HD + IDX_HEADS, :]
    ki = rope(sm)
    kio_ref[0] = ki
    k_rot = []
    for c in range(ATT_KV // LANES):
        sl = slice(c * LANES, (c + 1) * LANES)
        k_rot.append(rope(head_norm(k_ref[0, :, sl]) * kg_ref[...]))
        ko_ref[0, :, sl] = k_rot[c]

    if key_refs:
        kb_ref, kib_ref, vt_ref = key_refs
        kib_ref[0] = ki[:, :IDX_HD].astype(BF16)
        ones_row = jnp.where(lax.broadcasted_iota(I32, (V_ROWS - ATT_HD, tr), 0) == 0, 1.0, 0.0).astype(BF16)
        for c in range(ATT_KV // LANES):
            sl = slice(c * LANES, (c + 1) * LANES)
            kb_ref[0, :, sl] = k_rot[c].astype(BF16)
            vt = jnp.transpose(v_ref[0, :, sl]).astype(BF16)
            for hh in range(2):
                vt_ref[0, 2 * c + hh, 0, 0:ATT_HD, :] = vt[hh * ATT_HD:(hh + 1) * ATT_HD, :]
                vt_ref[0, 2 * c + hh, 0, ATT_HD:V_ROWS, :] = ones_row


def _dsa_prep(u, u_small, cos, sin, qg, kg, *, tr, emit_keys):
    b, t, _ = u.shape
    nb = t // DSA_TQ
    nqb = tr // DSA_TQ
    full2 = lambda i, j: (0, 0)
    seg = (jnp.arange(LANES)[:, None] // ATT_HD == jnp.arange(LANES)[None, :] // ATT_HD).astype(BF16)
    out_specs = [
        pl.BlockSpec((1, tr, ATT_KV), lambda i, j: (i, j, 0)),
        pl.BlockSpec((1, tr, LANES), lambda i, j: (i, j, 0)),
        pl.BlockSpec((1, nqb, ATT_Q, DSA_TQ), lambda i, j: (i, j, 0, 0)),
        pl.BlockSpec((1, nqb, IDX_Q, DSA_TQ), lambda i, j: (i, j, 0, 0)),
        pl.BlockSpec((1, nqb, IDX_HEADS, DSA_TQ), lambda i, j: (i, j, 0, 0)),
    ]
    out_shape = [
        jax.ShapeDtypeStruct((b, t, ATT_KV), F32),
        jax.ShapeDtypeStruct((b, t, LANES), F32),
        jax.ShapeDtypeStruct((b, nb, ATT_Q, DSA_TQ), BF16),
        jax.ShapeDtypeStruct((b, nb, IDX_Q, DSA_TQ), BF16),
        jax.ShapeDtypeStruct((b, nb, IDX_HEADS, DSA_TQ), F32),
    ]
    if emit_keys:
        out_specs += [
            pl.BlockSpec((1, tr, ATT_KV), lambda i, j: (i, j, 0)),
            pl.BlockSpec((1, tr, IDX_HD), lambda i, j: (i, j, 0)),
            pl.BlockSpec((1, KV_HEADS, 1, V_ROWS, tr), lambda i, j: (i, 0, j, 0, 0)),
        ]
        out_shape += [
            jax.ShapeDtypeStruct((b, t, ATT_KV), BF16),
            jax.ShapeDtypeStruct((b, t, IDX_HD), BF16),
            jax.ShapeDtypeStruct((b, KV_HEADS, t // tr, V_ROWS, tr), BF16),
        ]
    return pl.pallas_call(
        functools.partial(_dsa_prep_kernel, tr=tr),
        grid=(b, t // tr),
        in_specs=[
            pl.BlockSpec((1, tr, ATT_Q), lambda i, j: (i, j, 0)),
            pl.BlockSpec((1, tr, ATT_KV), lambda i, j: (i, j, 4)),
            pl.BlockSpec((1, tr, ATT_KV), lambda i, j: (i, j, 5)),
            pl.BlockSpec((1, tr, IDX_Q), lambda i, j: (i, j, 3)),
            pl.BlockSpec((1, tr, LANES), lambda i, j: (i, j, 0)),
            pl.BlockSpec((tr, LANES), lambda i, j: (j, 0)),
            pl.BlockSpec((tr, LANES), lambda i, j: (j, 0)),
            pl.BlockSpec((1, LANES), full2),
            pl.BlockSpec((1, LANES), full2),
            pl.BlockSpec((LANES, LANES), full2),
        ],
        out_specs=out_specs,
        out_shape=out_shape,
        compiler_params=_cparams(("parallel", "parallel")),
        name="dsa_prep",
    )(u, u, u, u, u_small, cos, sin, qg, kg, seg)


def _fold8(x, op):
    r, n = x.shape
    x = x.reshape(r // 8, 8, n)
    while x.shape[0] > 1:
        h = x.shape[0] // 2
        y = op(x[:h], x[h:2 * h])
        x = y if x.shape[0] % 2 == 0 else jnp.concatenate([y, x[2 * h:]], axis=0)
    return x[0]


def _dsa_kernel(qit_ref, wit_ref, ki_ref, qt_ref, k_ref, vt_ref, o_ref,
                rhs_i_ref, rhs_q_ref, key_ref, bias_ref, s_ref, acc_ref, mm_ref,
                *, tq, cw, l_keys, start, nsel):
    i = pl.program_id(1)
    cols4 = ATT_GROUPS * tq
    last_pos = start + i * tq + (tq - 1)
    max_limit = jnp.minimum((lax.shift_right_logical(last_pos, 6) + 1) * CHUNK, l_keys)
    n_c = lax.div(max_limit + (cw - 1), cw)

    pos = start + i * tq + lax.broadcasted_iota(I32, (1, tq), 1)
    limit = jnp.minimum((lax.shift_right_logical(pos, 6) + 1) * CHUNK, l_keys)
    sub_pos = lax.broadcasted_iota(I32, (cw, 1), 0)

    for h in range(IDX_HEADS):
        rhs_i_ref[:, h * tq:(h + 1) * tq] = qit_ref[0, 0, h * IDX_HD:(h + 1) * IDX_HD, :]

    @pl.when(i == 0)
    def _():
        rhs_q_ref[...] = jnp.zeros_like(rhs_q_ref)

    for j in range(KV_HEADS):
        for g in range(ATT_GROUPS):
            hq = ATT_GROUPS * j + g
            rhs_q_ref[j, j * ATT_HD:(j + 1) * ATT_HD, g * tq:(g + 1) * tq] = \
                qt_ref[0, 0, hq * ATT_HD:(hq + 1) * ATT_HD, :]

    w = [wit_ref[0, 0, h:h + 1, :] * IDX_SCALE for h in range(IDX_HEADS)]

    def score_chunk(c, carry):
        d = jnp.maximum(_dot(ki_ref[0, c], rhs_i_ref[...]), 0.0)
        score = d[:, 0:tq] * w[0]
        for h in range(1, IDX_HEADS):
            score = score + d[:, h * tq:(h + 1) * tq] * w[h]
        bits = pltpu.bitcast(score, I32)
        key = jnp.where(bits < 0, bits ^ 0x7FFFFFFF, bits)
        key = jnp.where(score == 0.0, 0, key)
        key_ref[c] = jnp.where(sub_pos + c * cw < limit, key, INT_MIN)
        return carry

    lax.fori_loop(0, n_c, score_chunk, 0)

    def count(pred):
        def body(c, acc):
            return acc + _fold8(jnp.where(pred(key_ref[c], c * cw), 1.0, 0.0), jnp.add)

        acc = lax.fori_loop(0, n_c, body, jnp.zeros((8, tq), F32))
        return jnp.sum(acc, axis=0, keepdims=True)

    def search(it, lo):
        cand = lo + lax.shift_left(jnp.int32(1), 31 - it)
        cnt = count(lambda k, base: k >= cand)
        return jnp.where(cnt >= nsel, cand, lo)

    thr = lax.fori_loop(0, 32, search, jnp.full((1, tq), INT_MIN, I32))

    c_gt = count(lambda k, base: k > thr)
    n_eq = count(lambda k, base: k == thr)
    need = nsel - c_gt
    excess = jnp.where((n_eq > need) & (thr > INT_MIN), 1.0, 0.0)
    mm_ref[...] = jnp.full((1, tq), 1 << 14, I32)

    @pl.when(jnp.max(excess) > 0.0)
    def _():
        def tie_search(it, m):
            cand = m + lax.shift_left(jnp.int32(1), 13 - it)
            f = count(lambda k, base: (k == thr) & (sub_pos + base < cand))
            return jnp.where(f < need, cand, m)

        mm_ref[...] = lax.fori_loop(0, 14, tie_search, jnp.zeros((1, tq), I32))

    mm = mm_ref[...]

    def bias_chunk(c, carry):
        k = key_ref[c]
        kpos = sub_pos + c * cw
        sel = ((k > thr) | ((k == thr) & (kpos <= mm))) & (kpos < limit)
        bias_ref[c] = jnp.where(sel, 0.0, NEG_BIG)
        return carry

    lax.fori_loop(0, n_c, bias_chunk, 0)

    scale = (ATT_HD ** -0.5) * math.log2(math.e)
    m0 = jnp.full((8, cols4), -3e38, F32)
    for j0 in range(0, KV_HEADS, DSA_HPT):
        pair = tuple(range(j0, j0 + DSA_HPT))

        def qk_chunk(c, ms, pair=pair):
            b1 = bias_ref[c]
            b4 = jnp.concatenate([b1] * ATT_GROUPS, axis=1)
            out = []
            for jj, j in enumerate(pair):
                s = _dot(k_ref[0, c], rhs_q_ref[j]) * scale + b4
                s_ref[jj, c] = s
                out.append(jnp.maximum(ms[jj], _fold8(s, jnp.maximum)))
            return tuple(out)

        ms = lax.fori_loop(0, n_c, qk_chunk, (m0,) * DSA_HPT)
        ms = [jnp.max(m, axis=0, keepdims=True) for m in ms]
        acc_ref[...] = jnp.zeros_like(acc_ref)

        def pv_chunk(c, carry, pair=pair, ms=ms):
            for jj, j in enumerate(pair):
                p = jnp.exp2(s_ref[jj, c] - ms[jj])
                acc_ref[jj] += _dot(vt_ref[0, j, c], p.astype(BF16))
            return carry

        lax.fori_loop(0, n_c, pv_chunk, 0)
        for jj, j in enumerate(pair):
            a = acc_ref[jj]
            o = a[0:ATT_HD] / a[ATT_HD:ATT_HD + 1]
            for g in range(0, ATT_GROUPS, 2):
                two = jnp.concatenate([o[:, g * tq:(g + 1) * tq], o[:, (g + 1) * tq:(g + 2) * tq]], axis=0)
                lo = (ATT_GROUPS * j + g) * ATT_HD
                o_ref[0, :, lo:lo + 2 * ATT_HD] = jnp.transpose(two).astype(o_ref.dtype)


def _dsa(qit, wit, ki, qt, k, vt, *, tq, cw, l_keys, start):
    b, nb = qit.shape[0], qit.shape[1]
    nc = k.shape[1]
    nsel = min(TOPK_MAX, l_keys // 4)
    kern = functools.partial(_dsa_kernel, tq=tq, cw=cw, l_keys=l_keys, start=start, nsel=float(nsel))
    cols4 = ATT_GROUPS * tq
    return pl.pallas_call(
        kern,
        grid=(b, nb),
        in_specs=[
            pl.BlockSpec((1, 1, IDX_Q, tq), lambda i, j: (i, j, 0, 0)),
            pl.BlockSpec((1, 1, IDX_HEADS, tq), lambda i, j: (i, j, 0, 0)),
            pl.BlockSpec((1, nc, cw, IDX_HD), lambda i, j: (i, 0, 0, 0)),
            pl.BlockSpec((1, 1, ATT_Q, tq), lambda i, j: (i, j, 0, 0)),
            pl.BlockSpec((1, nc, cw, ATT_KV), lambda i, j: (i, 0, 0, 0)),
            pl.BlockSpec((1, KV_HEADS, nc, V_ROWS, cw), lambda i, j: (i, 0, 0, 0, 0)),
        ],
        out_specs=pl.BlockSpec((1, tq, ATT_Q), lambda i, j: (i, j, 0)),
        out_shape=jax.ShapeDtypeStruct((b, nb * tq, ATT_Q), BF16),
        scratch_shapes=[
            pltpu.VMEM((IDX_HD, IDX_HEADS * tq), BF16),
            pltpu.VMEM((KV_HEADS, ATT_KV, cols4), BF16),
            pltpu.VMEM((nc, cw, tq), I32),
            pltpu.VMEM((nc, cw, tq), F32),
            pltpu.VMEM((DSA_HPT, nc, cw, cols4), F32),
            pltpu.VMEM((DSA_HPT, V_ROWS, cols4), F32),
            pltpu.VMEM((1, tq), I32),
        ],
        compiler_params=_cparams(("parallel", "arbitrary")),
        name="dsa",
    )(qit, wit, ki, qt, k, vt)


def _dsa_key_layouts(k_all, v_all, ki_all, cw):
    b, l_keys, _ = k_all.shape
    nc = -(-l_keys // cw)
    padk = lambda a: jnp.pad(a, ((0, 0), (0, nc * cw - l_keys), (0, 0))).astype(BF16)
    vt = padk(v_all).reshape(b, nc, cw, KV_HEADS, ATT_HD).transpose(0, 3, 1, 4, 2)
    ones = jnp.ones((b, KV_HEADS, nc, 1, cw), BF16)
    zeros = jnp.zeros((b, KV_HEADS, nc, V_ROWS - ATT_HD - 1, cw), BF16)
    return (padk(ki_all).reshape(b, nc, cw, IDX_HD), padk(k_all).reshape(b, nc, cw, ATT_KV),
            jnp.concatenate([vt, ones, zeros], axis=3))


def _prep_weights(ssd_w_in, att_w_in, ssd_w_out, att_w_out, mem_w_kv, dense_w_gate, dense_w_up,
                  dense_w_down, moe_router, moe_w_gate, moe_w_up, moe_w_down):
    bf = lambda w: w.astype(BF16)
    w = ssd_w_in[0]
    o_xbc, o_dt, o_mq = D_INNER, D_INNER + CONV_DIM, D_INNER + CONV_DIM + SSD_HEADS
    ssd_main = bf(jnp.concatenate([w[:, o_xbc:o_dt], w[:, o_mq:], w[:, :D_INNER]], axis=1))
    ssd_small = bf(jnp.pad(w[:, o_dt:o_mq], ((0, 0), (0, LANES - SSD_HEADS))))
    w = att_w_in[0]
    o_wi = ATT_Q + 2 * ATT_KV + IDX_Q
    o_ki = o_wi + IDX_HEADS
    o_mq = o_ki + IDX_HD
    att_main = bf(jnp.concatenate([w[:, :o_wi], w[:, o_mq:]], axis=1))
    att_small = bf(jnp.pad(jnp.concatenate([w[:, o_ki:o_mq], w[:, o_wi:o_ki]], axis=1),
                           ((0, 0), (0, LANES - IDX_HD - IDX_HEADS))))
    r = jnp.pad(moe_router[0], ((0, 0), (0, LANES - N_EXPERTS)))
    r_hi = bf(r)
    r_lo = bf(r - r_hi.astype(F32))
    return dict(
        ssd_main=ssd_main, ssd_small=ssd_small, att_main=att_main, att_small=att_small,
        ssd_out_a=bf(ssd_w_out[0, :D_INNER]), ssd_out_b=bf(ssd_w_out[0, D_INNER:]),
        att_out_a=bf(att_w_out[0, :ATT_Q]), att_out_b=bf(att_w_out[0, ATT_Q:]),
        mem_w_kv=bf(mem_w_kv), dense_g=bf(dense_w_gate[0]), dense_u=bf(dense_w_up[0]),
        dense_d=bf(dense_w_down[0]), r_hi=r_hi, r_lo=r_lo,
        moe_g=bf(moe_w_gate[0]), moe_u=bf(moe_w_up[0]), moe_d=bf(moe_w_down[0]))


def _rope_tables(pos):
    half = ATT_HD // 2
    inv = ROPE_THETA ** (-jnp.arange(half, dtype=F32) / half)
    ang = pos.astype(F32)[:, None] * inv[None, :]
    cos = jnp.cos(ang)
    sin = jnp.sin(ang)
    cos_t = jnp.concatenate([cos, cos, cos, cos], axis=1)
    sin_t = jnp.concatenate([-sin, sin, -sin, sin], axis=1)
    return cos_t, sin_t


def _trunk(x, start, mem_k, mem_v, conv_in, ssm_in, kv_in, P, W, q_ssd, cw_dsa):
    b, t, _ = x.shape
    n = b * t
    x2 = x.reshape(n, D_MODEL)

    u = _norm_matmul(x2, W['ssd_norm'][0], P['ssd_main'], tn=1024).reshape(b, t, -1)
    u_small = _norm_matmul(x2, W['ssd_norm'][0], P['ssd_small'], tn=LANES).reshape(b, t, LANES)
    conv8 = jnp.pad(conv_in, ((0, 0), (8 - (CONV_W - 1), 0), (0, 0)))
    h0_t = ssm_in.reshape(b, SSD_GROUPS, SSD_HPG, SSD_HEADDIM, D_STATE).transpose(0, 1, 4, 2, 3)
    h0_t = h0_t.reshape(b, SSD_GROUPS, D_STATE, SSD_HPG * SSD_HEADDIM)
    cw8 = jnp.pad(W['ssd_conv_w'][0], ((0, 8 - CONV_W), (0, 0)))
    pad_h = lambda v: jnp.pad(v.astype(F32), (0, LANES - SSD_HEADS)).reshape(1, LANES)
    e_mat = (jnp.arange(LANES)[:, None] == (jnp.arange(D_INNER)[None, :] // SSD_HEADDIM)).astype(BF16)
    y_mix, h_t = _ssd(u, u_small, conv8, h0_t, cw8, W['ssd_conv_b'][0].reshape(1, CONV_DIM),
                      pad_h(W['ssd_dt_bias'][0]), pad_h(W['ssd_A_log'][0]),
                      jnp.repeat(W['ssd_D'][0].astype(F32), SSD_HEADDIM).reshape(1, D_INNER),
                      W['ssd_out_norm'][0].reshape(1, D_INNER), e_mat, q_ssd)
    new_conv = u[:, t - (CONV_W - 1):, :CONV_DIM]
    new_ssm = h_t.reshape(b, SSD_GROUPS, D_STATE, SSD_HPG, SSD_HEADDIM).transpose(0, 1, 3, 4, 2)
    new_ssm = new_ssm.reshape(b, SSD_HEADS, SSD_HEADDIM, D_STATE)
    y_mem = _mem_attn(u, 3, mem_k[0], mem_v[0], W['mem_q_norm'][0])
    x2 = _proj_res(y_mix.reshape(n, D_INNER), y_mem.reshape(n, MEM_WIDTH), P['ssd_out_a'], P['ssd_out_b'], x2)
    x2 = _ffn(x2, W['ffn_norm'][0], P['dense_g'], P['dense_u'], P['dense_d'])

    u = _norm_matmul(x2, W['att_norm'][0], P['att_main'], tn=1024).reshape(b, t, -1)
    u_small = _norm_matmul(x2, W['att_norm'][0], P['att_small'], tn=LANES).reshape(b, t, LANES)
    pos = start + jnp.arange(t)
    cos_t, sin_t = _rope_tables(pos)
    tile2 = lambda v: jnp.tile(v.astype(F32), 2).reshape(1, LANES)
    v_new = u[:, :, ATT_Q + ATT_KV:ATT_Q + 2 * ATT_KV]
    qg, kg = tile2(W['att_q_norm'][0]), tile2(W['att_k_norm'][0])
    if kv_in is None:
        k_rot, ki_rot, qt, qit, wit, k_b, ki_b, vt = _dsa_prep(u, u_small, cos_t, sin_t, qg, kg,
                                                               tr=cw_dsa, emit_keys=True)
        nc = t // cw_dsa
        keys = (ki_b.reshape(b, nc, cw_dsa, IDX_HD), k_b.reshape(b, nc, cw_dsa, ATT_KV), vt)
        ki_new = ki_rot[:, :, :IDX_HD]
        l_keys = t
    else:
        padt = lambda a: jnp.pad(a, ((0, DSA_TQ - t),) + ((0, 0),) * (a.ndim - 1))
        padbt = lambda a: jnp.pad(a, ((0, 0), (0, DSA_TQ - t), (0, 0)))
        k_rot, ki_rot, qt, qit, wit = _dsa_prep(padbt(u), padbt(u_small), padt(cos_t), padt(sin_t), qg, kg,
                                                tr=DSA_TQ, emit_keys=False)
        k_rot, ki_rot = k_rot[:, :t], ki_rot[:, :t]
        ki_new = ki_rot[:, :, :IDX_HD]
        keys = _dsa_key_layouts(jnp.concatenate([kv_in[0], k_rot], axis=1),
                                jnp.concatenate([kv_in[1], v_new], axis=1),
                                jnp.concatenate([kv_in[2], ki_new], axis=1), cw_dsa)
        l_keys = kv_in[0].shape[1] + t
    o_t = _dsa(qit, wit, keys[0], qt, keys[1], keys[2], tq=DSA_TQ, cw=cw_dsa, l_keys=l_keys, start=start)
    y_mix = o_t[:, :t].reshape(n, ATT_Q)
    y_mem = _mem_attn(u, 2, mem_k[1], mem_v[1], W['mem_q_norm'][1])
    x2 = _proj_res(y_mix, y_mem.reshape(n, MEM_WIDTH), P['att_out_a'], P['att_out_b'], x2)
    x2 = _moe(x2, W['ffn_norm'][1], P['r_hi'], P['r_lo'], P['moe_g'], P['moe_u'], P['moe_d'])

    return (x2.reshape(b, t, D_MODEL), new_conv, new_ssm, k_rot.reshape(b, t, KV_HEADS, ATT_HD),
            v_new.reshape(b, t, KV_HEADS, ATT_HD), ki_new)


def kernel(x_prompt, x_sample, mem_prompt, cache_conv, state_ssm, cache_k, cache_v, cache_idx_k, cache_mem_k, cache_mem_v, ssd_norm, ssd_w_in, ssd_conv_w, ssd_conv_b, ssd_dt_bias, ssd_A_log, ssd_D, ssd_out_norm, ssd_w_out, att_norm, att_w_in, att_q_norm, att_k_norm, att_w_out, mem_norm, mem_w_kv, mem_q_norm, mem_k_norm, ffn_norm, dense_w_gate, dense_w_up, dense_w_down, moe_router, moe_w_gate, moe_w_up, moe_w_down):
    W = dict(ssd_norm=ssd_norm, ssd_conv_w=ssd_conv_w, ssd_conv_b=ssd_conv_b, ssd_dt_bias=ssd_dt_bias,
             ssd_A_log=ssd_A_log, ssd_D=ssd_D, ssd_out_norm=ssd_out_norm, att_norm=att_norm,
             att_q_norm=att_q_norm, att_k_norm=att_k_norm, mem_q_norm=mem_q_norm, ffn_norm=ffn_norm)
    P = _prep_weights(ssd_w_in, att_w_in, ssd_w_out, att_w_out, mem_w_kv, dense_w_gate, dense_w_up,
                      dense_w_down, moe_router, moe_w_gate, moe_w_up, moe_w_down)
    bp, sp = x_prompt.shape[0], x_prompt.shape[1]
    bs = x_sample.shape[0]

    mem2 = mem_prompt.reshape(bp * N_MEM, D_MODEL)
    pk, pv = [], []
    for i in range(2):
        kv = _norm_matmul(mem2, mem_norm[i], P['mem_w_kv'][i], tn=1024)
        pk.append(_head_norm(kv[:, :MEM_WIDTH], mem_k_norm[i]).reshape(bp, N_MEM, MEM_WIDTH))
        pv.append(kv[:, MEM_WIDTH:].reshape(bp, N_MEM, MEM_WIDTH))
    p_mem_k = jnp.stack(pk)
    p_mem_v = jnp.stack(pv)

    conv0 = jnp.zeros((bp, CONV_W - 1, CONV_DIM), F32)
    ssm0 = jnp.zeros((bp, SSD_HEADS, SSD_HEADDIM, D_STATE), F32)
    y_p, p_conv, p_ssm, p_k, p_v, p_ki = _trunk(x_prompt, 0, p_mem_k, p_mem_v, conv0, ssm0, None, P, W,
                                                q_ssd=128, cw_dsa=512)
    past = cache_k.shape[2]
    kv_in = (cache_k[0].reshape(bs, past, ATT_KV), cache_v[0].reshape(bs, past, ATT_KV), cache_idx_k[0])
    y_s, s_conv, s_ssm, s_k, s_v, s_ki = _trunk(
        x_sample, past, cache_mem_k.reshape(2, bs, N_MEM, MEM_WIDTH), cache_mem_v.reshape(2, bs, N_MEM, MEM_WIDTH),
        cache_conv[0], state_ssm[0], kv_in, P, W, q_ssd=x_sample.shape[1], cw_dsa=384)

    shp = (bp, N_MEM, MEM_HEADS, MEM_HD)
    return (y_p, y_s, p_conv[None], p_ssm[None], p_k[None], p_v[None], p_ki[None],
            p_mem_k.reshape((2,) + shp), p_mem_v.reshape((2,) + shp),
            s_conv[None], s_ssm[None], s_k[None], s_v[None], s_ki[None])
```

```python
import functools
import math

import jax
import jax.numpy as jnp
from jax import lax
from jax.experimental import pallas as pl
from jax.experimental.pallas import tpu as pltpu

F32 = jnp.float32
BF16 = jnp.bfloat16
I32 = jnp.int32
I16 = jnp.int16

D_MODEL = 1024
CHUNK = 64
N_MEM = 256
EPS = 1e-6
D_INNER = 2048
SSD_HEADDIM = 64
SSD_HEADS = 32
SSD_GROUPS = 4
SSD_HPG = 8
D_STATE = 128
CONV_W = 4
GN = SSD_GROUPS * D_STATE
CONV_DIM = D_INNER + 2 * GN
ATT_HEADS = 16
ATT_HD = 64
KV_HEADS = 4
ATT_GROUPS = 4
ATT_Q = 1024
ATT_KV = 256
IDX_HEADS = 8
IDX_HD = 64
IDX_Q = 512
IDX_SCALE = (IDX_HEADS * IDX_HD) ** -0.5
TOPK_MAX = 256
ROPE_THETA = 10000.0
MEM_HEADS = 4
MEM_HD = 256
MEM_WIDTH = 1024
D_FF = 3584
N_EXPERTS = 8

LANES = 128
INT_MIN = -(2 ** 31)
NEG_BIG = -1e30
VMEM_LIMIT = 56 * 1024 * 1024


def _cparams(sem):
    return pltpu.CompilerParams(dimension_semantics=sem, vmem_limit_bytes=VMEM_LIMIT)


def _dot(a, b):
    return jnp.dot(a, b, preferred_element_type=F32)


def _split3(v):
    hi = v.astype(BF16)
    r1 = v - hi.astype(F32)
    mid = r1.astype(BF16)
    lo = (r1 - mid.astype(F32)).astype(BF16)
    return hi, mid, lo


def _silu(x):
    return x * (0.5 * jnp.tanh(0.5 * x) + 0.5)


def _norm_matmul_kernel(x_ref, g_ref, w_ref, o_ref, h_ref):
    @pl.when(pl.program_id(1) == 0)
    def _():
        x = x_ref[...]
        ms = jnp.mean(x * x, axis=-1, keepdims=True)
        h_ref[...] = (x * lax.rsqrt(ms + EPS) * g_ref[...]).astype(BF16)

    o_ref[...] = _dot(h_ref[...], w_ref[...]).astype(o_ref.dtype)


def _norm_matmul(x, g, w, tn, out_dtype=F32):
    n, k = x.shape
    m = w.shape[1]
    tm = min(1024, n)
    return pl.pallas_call(
        _norm_matmul_kernel,
        grid=(n // tm, m // tn),
        in_specs=[
            pl.BlockSpec((tm, k), lambda i, j: (i, 0)),
            pl.BlockSpec((1, k), lambda i, j: (0, 0)),
            pl.BlockSpec((k, tn), lambda i, j: (0, j)),
        ],
        out_specs=pl.BlockSpec((tm, tn), lambda i, j: (i, j)),
        out_shape=jax.ShapeDtypeStruct((n, m), out_dtype),
        scratch_shapes=[pltpu.VMEM((tm, k), BF16)],
        compiler_params=_cparams(("parallel", "arbitrary")),
        name="norm_matmul",
    )(x, g.reshape(1, k), w)


def _head_norm_kernel(x_ref, g_ref, o_ref):
    for h in range(MEM_HEADS):
        x = x_ref[:, h * MEM_HD:(h + 1) * MEM_HD]
        ms = jnp.mean(x * x, axis=-1, keepdims=True)
        o_ref[:, h * MEM_HD:(h + 1) * MEM_HD] = x * lax.rsqrt(ms + EPS) * g_ref[...]


def _head_norm(x, g):
    n = x.shape[0]
    tm = min(512, n)
    return pl.pallas_call(
        _head_norm_kernel,
        grid=(n // tm,),
        in_specs=[pl.BlockSpec((tm, MEM_WIDTH), lambda i: (i, 0)),
                  pl.BlockSpec((1, MEM_HD), lambda i: (0, 0))],
        out_specs=pl.BlockSpec((tm, MEM_WIDTH), lambda i: (i, 0)),
        out_shape=jax.ShapeDtypeStruct((n, MEM_WIDTH), F32),
        compiler_params=_cparams(("parallel",)),
        name="head_norm",
    )(x, g.reshape(1, MEM_HD))


def _mem_attn_kernel(q_ref, k_ref, v_ref, g_ref, o_ref):
    for h in range(MEM_HEADS):
        sl = slice(h * MEM_HD, (h + 1) * MEM_HD)
        q = q_ref[0, :, sl]
        ms = jnp.mean(q * q, axis=-1, keepdims=True)
        qn = (q * lax.rsqrt(ms + EPS) * g_ref[...]).astype(BF16)
        k = k_ref[0, :, sl].astype(BF16)
        s = lax.dot_general(qn, k, (((1,), (1,)), ((), ())), preferred_element_type=F32)
        s = s * (MEM_HD ** -0.5)
        m = jnp.max(s, axis=-1, keepdims=True)
        p = jnp.exp(s - m)
        p = p / jnp.sum(p, axis=-1, keepdims=True)
        o = _dot(p.astype(BF16), v_ref[0, :, sl].astype(BF16))
        o_ref[0, :, sl] = o.astype(o_ref.dtype)


def _mem_attn(u, col_block, mk, mv, g):
    b, t, _ = u.shape
    tq = min(512, t)
    return pl.pallas_call(
        _mem_attn_kernel,
        grid=(b, t // tq),
        in_specs=[
            pl.BlockSpec((1, tq, MEM_WIDTH), lambda i, j: (i, j, col_block)),
            pl.BlockSpec((1, N_MEM, MEM_WIDTH), lambda i, j: (i, 0, 0)),
            pl.BlockSpec((1, N_MEM, MEM_WIDTH), lambda i, j: (i, 0, 0)),
            pl.BlockSpec((1, MEM_HD), lambda i, j: (0, 0)),
        ],
        out_specs=pl.BlockSpec((1, tq, MEM_WIDTH), lambda i, j: (i, j, 0)),
        out_shape=jax.ShapeDtypeStruct((b, t, MEM_WIDTH), BF16),
        compiler_params=_cparams(("parallel", "parallel")),
        name="mem_attn",
    )(u, mk, mv, g.reshape(1, MEM_HD))


def _proj_res_kernel(a_ref, b_ref, wa_ref, wb_ref, x_ref, o_ref):
    o_ref[...] = x_ref[...] + _dot(a_ref[...], wa_ref[...]) + _dot(b_ref[...], wb_ref[...])


def _proj_res(a, b, wa, wb, x):
    n = x.shape[0]
    tm = min(512, n)
    ka, kb = a.shape[1], b.shape[1]
    return pl.pallas_call(
        _proj_res_kernel,
        grid=(n // tm,),
        in_specs=[
            pl.BlockSpec((tm, ka), lambda i: (i, 0)),
            pl.BlockSpec((tm, kb), lambda i: (i, 0)),
            pl.BlockSpec((ka, D_MODEL), lambda i: (0, 0)),
            pl.BlockSpec((kb, D_MODEL), lambda i: (0, 0)),
            pl.BlockSpec((tm, D_MODEL), lambda i: (i, 0)),
        ],
        out_specs=pl.BlockSpec((tm, D_MODEL), lambda i: (i, 0)),
        out_shape=jax.ShapeDtypeStruct((n, D_MODEL), F32),
        compiler_params=_cparams(("parallel",)),
        name="proj_res",
    )(a, b, wa, wb, x)


def _ffn_kernel(x_ref, g_ref, wg_ref, wu_ref, wd_ref, o_ref, h_ref, acc_ref):
    f = pl.program_id(1)

    @pl.when(f == 0)
    def _():
        x = x_ref[...]
        ms = jnp.mean(x * x, axis=-1, keepdims=True)
        h_ref[...] = (x * lax.rsqrt(ms + EPS) * g_ref[...]).astype(BF16)
        acc_ref[...] = x

    h = h_ref[...]
    a = _silu(_dot(h, wg_ref[...])) * _dot(h, wu_ref[...])
    acc_ref[...] += _dot(a.astype(BF16), wd_ref[...])

    @pl.when(f == pl.num_programs(1) - 1)
    def _():
        o_ref[...] = acc_ref[...]


def _ffn(x, g, wg, wu, wd, tf=512):
    n = x.shape[0]
    tm = min(1024, n)
    return pl.pallas_call(
        _ffn_kernel,
        grid=(n // tm, D_FF // tf),
        in_specs=[
            pl.BlockSpec((tm, D_MODEL), lambda i, f: (i, 0)),
            pl.BlockSpec((1, D_MODEL), lambda i, f: (0, 0)),
            pl.BlockSpec((D_MODEL, tf), lambda i, f: (0, f)),
            pl.BlockSpec((D_MODEL, tf), lambda i, f: (0, f)),
            pl.BlockSpec((tf, D_MODEL), lambda i, f: (f, 0)),
        ],
        out_specs=pl.BlockSpec((tm, D_MODEL), lambda i, f: (i, 0)),
        out_shape=jax.ShapeDtypeStruct((n, D_MODEL), F32),
        scratch_shapes=[pltpu.VMEM((tm, D_MODEL), BF16), pltpu.VMEM((tm, D_MODEL), F32)],
        compiler_params=_cparams(("parallel", "arbitrary")),
        name="ffn",
    )(x, g.reshape(1, D_MODEL), wg, wu, wd)


MOE_ROWS = 256


def _split2(v):
    hi = v.astype(BF16)
    return hi, (v - hi.astype(F32)).astype(BF16)


def _moe_route_kernel(x_ref, g_ref, rh_ref, rl_ref, h_ref, gate_ref):
    lane = lax.broadcasted_iota(I32, (1, LANES), 1)
    x = x_ref[...]
    ms = jnp.mean(x * x, axis=-1, keepdims=True)
    hf = x * lax.rsqrt(ms + EPS) * g_ref[...]
    hb, hl = _split2(hf)
    h_ref[...] = hb
    logits = _dot(hb, rh_ref[...]) + _dot(hl, rh_ref[...]) + _dot(hb, rl_ref[...])
    valid = lane < N_EXPERTS
    logits = jnp.where(valid, logits, NEG_BIG)
    m = jnp.max(logits, axis=-1, keepdims=True)
    p = jnp.exp(logits - m)
    p = p / jnp.sum(p, axis=-1, keepdims=True)
    p = jnp.where(valid, p, -1.0)
    v1 = jnp.max(p, axis=-1, keepdims=True)
    i1 = jnp.min(jnp.where(p == v1, lane, LANES), axis=-1, keepdims=True)
    p2 = jnp.where(lane == i1, -1.0, p)
    v2 = jnp.max(p2, axis=-1, keepdims=True)
    i2 = jnp.min(jnp.where(p2 == v2, lane, LANES), axis=-1, keepdims=True)
    den = v1 + v2
    gate_ref[...] = jnp.where(lane == i1, v1 / den, jnp.where(lane == i2, v2 / den, 0.0))


def _moe_route(x, g, r_hi, r_lo):
    n = x.shape[0]
    tm = min(512, n)
    return pl.pallas_call(
        _moe_route_kernel,
        grid=(n // tm,),
        in_specs=[pl.BlockSpec((tm, D_MODEL), lambda i: (i, 0)),
                  pl.BlockSpec((1, D_MODEL), lambda i: (0, 0)),
                  pl.BlockSpec((D_MODEL, LANES), lambda i: (0, 0)),
                  pl.BlockSpec((D_MODEL, LANES), lambda i: (0, 0))],
        out_specs=[pl.BlockSpec((tm, D_MODEL), lambda i: (i, 0)),
                   pl.BlockSpec((tm, LANES), lambda i: (i, 0))],
        out_shape=[jax.ShapeDtypeStruct((n, D_MODEL), BF16), jax.ShapeDtypeStruct((n, LANES), F32)],
        compiler_params=_cparams(("parallel",)),
        name="moe_route",
    )(x, g.reshape(1, D_MODEL), r_hi, r_lo)


def _moe_ffn_kernel(h_ref, gate_ref, x_ref, wg_ref, wu_ref, wd_ref, o_ref,
                    rank_ref, rank_t_ref, sel_t_ref, gate_t_ref, xe_ref, acc_ref, nb_ref, *, tm):
    e = pl.program_id(1)
    f = pl.program_id(2)
    nf = pl.num_programs(2)
    big, small = MOE_ROWS, MOE_ROWS // 2
    lane = lax.broadcasted_iota(I32, (1, LANES), 1)

    @pl.when((e == 0) & (f == 0))
    def _():
        o_ref[...] = x_ref[...]
        gates = gate_ref[...]
        sel = jnp.where(gates > 0.0, 1.0, 0.0)
        r = lax.broadcasted_iota(I32, (tm, tm), 0)
        c = lax.broadcasted_iota(I32, (tm, tm), 1)
        below = jnp.where(c < r, 1.0, 0.0).astype(BF16)
        rank = _dot(below, sel.astype(BF16))
        rank_ref[...] = rank
        rank_t_ref[...] = jnp.transpose(rank)
        sel_t_ref[...] = jnp.transpose(sel)
        gate_t_ref[...] = jnp.transpose(gates)

    @pl.when(f == 0)
    def _():
        cnt = jnp.sum(jnp.where(lane == e, jnp.sum(jnp.where(gate_ref[...] > 0.0, 1.0, 0.0), axis=0, keepdims=True),
                                0.0)).astype(I32)
        q = lax.div(cnt, big)
        rem = cnt - q * big
        merge = (q >= 1) & (rem > 0) & (rem <= small)
        kind = jnp.where(rem == 0, 0,
                         jnp.where(merge, jnp.where(rem <= small // 2, 2, 3), jnp.where(rem <= small, 1, 0)))
        nb_ref[0] = q - jnp.where(merge, 1, 0) + jnp.where(rem > small, 1, 0)
        nb_ref[1] = kind

    n_big = nb_ref[0]
    last_kind = nb_ref[1]
    r_last = pl.multiple_of(n_big * big, big)
    last_rows = (small, big + small // 2, big + small)

    def pick(r0, rows):
        want = (lax.broadcasted_iota(I32, (rows, 1), 0) + r0).astype(F32)
        return (rank_t_ref[pl.ds(e, 1), :] == want) & (sel_t_ref[pl.ds(e, 1), :] > 0.0)

    def for_blocks(fn):
        def body(rb, carry):
            fn(pl.multiple_of(rb * big, big), big)
            return carry

        lax.fori_loop(0, n_big, body, 0)
        for kind, rows in enumerate(last_rows, start=1):
            @pl.when(last_kind == kind)
            def _(rows=rows):
                fn(r_last, rows)

    def gather(r0, rows):
        p = jnp.where(pick(r0, rows), 1.0, 0.0).astype(BF16)
        xe_ref[pl.ds(r0, rows), :] = _dot(p, h_ref[...]).astype(BF16)

    def ffn(r0, rows):
        xe = xe_ref[pl.ds(r0, rows), :]
        a = _silu(_dot(xe, wg_ref[0])) * _dot(xe, wu_ref[0])
        y = _dot(a.astype(BF16), wd_ref[0])

        @pl.when(f == 0)
        def _():
            acc_ref[pl.ds(r0, rows), :] = y

        @pl.when(f > 0)
        def _():
            acc_ref[pl.ds(r0, rows), :] += y

    @pl.when(f == 0)
    def _():
        for_blocks(gather)

    for_blocks(ffn)

    @pl.when(f == nf - 1)
    def _():
        rank_col = jnp.sum(jnp.where(lane == e, rank_ref[...], 0.0), axis=1, keepdims=True)
        sel_col = jnp.sum(jnp.where(lane == e, gate_ref[...], 0.0), axis=1, keepdims=True) > 0.0

        def scatter(r0, rows):
            g_rows = jnp.sum(jnp.where(pick(r0, rows), gate_t_ref[pl.ds(e, 1), :], 0.0), axis=1, keepdims=True)
            y = (acc_ref[pl.ds(r0, rows), :] * g_rows).astype(BF16)
            col = (lax.broadcasted_iota(I32, (1, rows), 1) + r0).astype(F32)
            put = jnp.where((rank_col == col) & sel_col, 1.0, 0.0).astype(BF16)
            o_ref[...] += _dot(put, y)

        for_blocks(scatter)


def _moe_ffn(h, gates, x, wg, wu, wd, *, tm=1024, tf=1792):
    n = x.shape[0]
    tm = min(tm, n)
    kern = functools.partial(_moe_ffn_kernel, tm=tm)
    return pl.pallas_call(
        kern,
        grid=(n // tm, N_EXPERTS, D_FF // tf),
        in_specs=[
            pl.BlockSpec((tm, D_MODEL), lambda i, e, f: (i, 0)),
            pl.BlockSpec((tm, LANES), lambda i, e, f: (i, 0)),
            pl.BlockSpec((tm, D_MODEL), lambda i, e, f: (i, 0)),
            pl.BlockSpec((1, D_MODEL, tf), lambda i, e, f: (e, 0, f)),
            pl.BlockSpec((1, D_MODEL, tf), lambda i, e, f: (e, 0, f)),
            pl.BlockSpec((1, tf, D_MODEL), lambda i, e, f: (e, f, 0)),
        ],
        out_specs=pl.BlockSpec((tm, D_MODEL), lambda i, e, f: (i, 0)),
        out_shape=jax.ShapeDtypeStruct((n, D_MODEL), F32),
        scratch_shapes=[
            pltpu.VMEM((tm, LANES), F32),
            pltpu.VMEM((LANES, tm), F32),
            pltpu.VMEM((LANES, tm), F32),
            pltpu.VMEM((LANES, tm), F32),
            pltpu.VMEM((tm, D_MODEL), BF16),
            pltpu.VMEM((tm, D_MODEL), F32),
            pltpu.SMEM((2,), I32),
        ],
        compiler_params=_cparams(("parallel", "arbitrary", "arbitrary")),
        name="moe_ffn",
    )(h, gates, x, wg, wu, wd)


def _moe(x, g, r_hi, r_lo, wg, wu, wd):
    h, gates = _moe_route(x, g, r_hi, r_lo)
    return _moe_ffn(h, gates, x, wg, wu, wd)


def _ssd_kernel(xbc_ref, z_ref, dt_ref, cs_ref, h0_ref, cw_ref, cb_ref, dtb_ref, alog_ref,
                dexp_ref, og_ref, e_ref, y_ref, hout_ref, xpad_ref, h_ref, *, q):
    c = pl.program_id(1)

    @pl.when(c == 0)
    def _():
        xpad_ref[0:8, :] = cs_ref[0]
        h_ref[...] = h0_ref[0]

    xbc = xbc_ref[0]
    xpad_ref[8:8 + q, :] = xbc
    cw = cw_ref[...]
    conv = cb_ref[...] + (xpad_ref[5:5 + q, :] * cw[0:1] + xpad_ref[6:6 + q, :] * cw[1:2]
                          + xpad_ref[7:7 + q, :] * cw[2:3] + xbc * cw[3:4])
    xpad_ref[0:8, :] = xpad_ref[q:q + 8, :]
    act = _silu(conv)

    lane = lax.broadcasted_iota(I32, (1, LANES), 1)
    xdt = dt_ref[0] + dtb_ref[...]
    sp = jnp.maximum(xdt, 0.0) + jnp.log1p(jnp.exp(-jnp.abs(xdt)))
    dt = jnp.where(lane < SSD_HEADS, sp, 0.0)
    a = dt * (-jnp.exp(alog_ref[...]))

    rows = lax.broadcasted_iota(I32, (q, q), 0)
    cols = lax.broadcasted_iota(I32, (q, q), 1)
    causal = rows >= cols
    tril = jnp.where(causal, 1.0, 0.0).astype(BF16)
    a3 = _split3(a)
    acum = _dot(tril, a3[0]) + _dot(tril, a3[1]) + _dot(tril, a3[2])
    acum_t = jnp.transpose(acum)

    e_mat = e_ref[...]
    ac3 = _split3(acum)
    acum_x = _dot(ac3[0], e_mat) + _dot(ac3[1], e_mat) + _dot(ac3[2], e_mat)
    dt3 = _split3(dt)
    dt_x = _dot(dt3[0], e_mat) + _dot(dt3[1], e_mat) + _dot(dt3[2], e_mat)
    last = acum_x[q - 1:q, :]
    eac_x = jnp.exp(acum_x)
    dte_x = jnp.exp(last - acum_x)
    blkdec = jnp.exp(last)

    xs = act[:, :D_INNER]
    x_dt = xs * dt_x
    xb = x_dt.astype(BF16)
    xdb = (x_dt * dte_x).astype(BF16)
    z = z_ref[0]
    lane_lo = lane < SSD_HEADDIM

    gw = SSD_HPG * SSD_HEADDIM
    for g in range(SSD_GROUPS):
        bg = act[:, D_INNER + g * D_STATE:D_INNER + (g + 1) * D_STATE]
        cg = act[:, D_INNER + GN + g * D_STATE:D_INNER + GN + (g + 1) * D_STATE]
        cgb = cg.astype(BF16)
        bgt = jnp.transpose(bg).astype(BF16)
        cb = _dot(cgb, bgt)
        gs = slice(g * gw, (g + 1) * gw)
        h_in = h_ref[g]
        y_off = _dot(cgb, h_in.astype(BF16)) * eac_x[:, gs]
        h_ref[g] = h_in * blkdec[:, gs] + _dot(bgt, xdb[:, gs])
        parts = []
        for pr in range(SSD_HPG // 2):
            h0 = g * SSD_HPG + 2 * pr
            xp = xb[:, (h0 // 2) * LANES:(h0 // 2 + 1) * LANES]
            ys = []
            for hh in (h0, h0 + 1):
                seg = acum[:, hh:hh + 1] - acum_t[hh:hh + 1, :]
                lm = jnp.exp(jnp.where(causal, seg, NEG_BIG))
                ys.append(_dot((cb * lm).astype(BF16), xp))
            parts.append(jnp.where(lane_lo, ys[0], ys[1]))
        y_g = jnp.concatenate(parts, axis=1) + y_off + dexp_ref[:, gs] * xs[:, gs]
        gt = y_g * _silu(z[:, gs])
        gn = gt * lax.rsqrt(jnp.mean(gt * gt, axis=-1, keepdims=True) + EPS) * og_ref[:, gs]
        y_ref[0, :, gs] = gn.astype(y_ref.dtype)

    @pl.when(c == pl.num_programs(1) - 1)
    def _():
        hout_ref[0] = h_ref[...]


def _ssd(u, u_small, conv_state8, h0_t, cw8, cb, dtb, alog, dexp, og, e_mat, q):
    b, t, _ = u.shape
    kern = functools.partial(_ssd_kernel, q=q)
    full2 = lambda i, c: (0, 0)
    return pl.pallas_call(
        kern,
        grid=(b, t // q),
        in_specs=[
            pl.BlockSpec((1, q, CONV_DIM), lambda i, c: (i, c, 0)),
            pl.BlockSpec((1, q, D_INNER), lambda i, c: (i, c, 2)),
            pl.BlockSpec((1, q, LANES), lambda i, c: (i, c, 0)),
            pl.BlockSpec((1, 8, CONV_DIM), lambda i, c: (i, 0, 0)),
            pl.BlockSpec((1, SSD_GROUPS, D_STATE, SSD_HPG * SSD_HEADDIM), lambda i, c: (i, 0, 0, 0)),
            pl.BlockSpec((8, CONV_DIM), full2),
            pl.BlockSpec((1, CONV_DIM), full2),
            pl.BlockSpec((1, LANES), full2),
            pl.BlockSpec((1, LANES), full2),
            pl.BlockSpec((1, D_INNER), full2),
            pl.BlockSpec((1, D_INNER), full2),
            pl.BlockSpec((LANES, D_INNER), full2),
        ],
        out_specs=[
            pl.BlockSpec((1, q, D_INNER), lambda i, c: (i, c, 0)),
            pl.BlockSpec((1, SSD_GROUPS, D_STATE, SSD_HPG * SSD_HEADDIM), lambda i, c: (i, 0, 0, 0)),
        ],
        out_shape=[
            jax.ShapeDtypeStruct((b, t, D_INNER), BF16),
            jax.ShapeDtypeStruct((b, SSD_GROUPS, D_STATE, SSD_HPG * SSD_HEADDIM), F32),
        ],
        scratch_shapes=[pltpu.VMEM((q + 8, CONV_DIM), F32),
                        pltpu.VMEM((SSD_GROUPS, D_STATE, SSD_HPG * SSD_HEADDIM), F32)],
        compiler_params=_cparams(("parallel", "arbitrary")),
        name="ssd",
    )(u, u, u_small, conv_state8, h0_t, cw8, cb, dtb, alog, dexp, og, e_mat)


V_ROWS = 80
DSA_TQ = LANES
DSA_HPT = 4


def _dsa_prep_kernel(q_ref, k_ref, v_ref, qi_ref, sm_ref, cos_ref, sin_ref, qg_ref, kg_ref, seg_ref,
                     ko_ref, kio_ref, qt_ref, qit_ref, wit_ref, *key_refs, tr):
    cos = cos_ref[...]
    sin = sin_ref[...]
    seg = seg_ref[...]
    lane = lax.broadcasted_iota(I32, (1, LANES), 1)
    nqb = tr // DSA_TQ

    def head_norm(x):
        s3 = _split3(x * x)
        ss = _dot(s3[0], seg) + _dot(s3[1], seg) + _dot(s3[2], seg)
        return x * lax.rsqrt(ss * (1.0 / ATT_HD) + EPS)

    def rope(x):
        partner = jnp.where((lane & 32) == 0, pltpu.roll(x, LANES - 32, 1), pltpu.roll(x, 32, 1))
        return x * cos + partner * sin

    def put_t(dst_ref, row0, x):
        for qb in range(nqb):
            xt = jnp.transpose(x[qb * DSA_TQ:(qb + 1) * DSA_TQ, :])
            dst_ref[0, qb, row0:row0 + LANES, :] = xt.astype(dst_ref.dtype)

    for c in range(ATT_Q // LANES):
        sl = slice(c * LANES, (c + 1) * LANES)
        put_t(qt_ref, c * LANES, rope(head_norm(q_ref[0, :, sl]) * qg_ref[...]))
    for c in range(IDX_Q // LANES):
        sl = slice(c * LANES, (c + 1) * LANES)
        put_t(qit_ref, c * LANES, rope(qi_ref[0, :, sl]))
    sm = sm_ref[0]
    for qb in range(nqb):
        wit_ref[0, qb] = jnp.transpose(sm[qb * DSA_TQ:(qb + 1) * DSA_TQ, :])[IDX_HD:IDX_HD + IDX_HEADS, :]
    ki = rope(sm)
    kio_ref[0] = ki
    k_rot = []
    for c in range(ATT_KV // LANES):
        sl = slice(c * LANES, (c + 1) * LANES)
        k_rot.append(rope(head_norm(k_ref[0, :, sl]) * kg_ref[...]))
        ko_ref[0, :, sl] = k_rot[c]

    if key_refs:
        kb_ref, kib_ref, vt_ref = key_refs
        kib_ref[0] = ki[:, :IDX_HD].astype(BF16)
        ones_row = jnp.where(lax.broadcasted_iota(I32, (V_ROWS - ATT_HD, tr), 0) == 0, 1.0, 0.0).astype(BF16)
        for c in range(ATT_KV // LANES):
            sl = slice(c * LANES, (c + 1) * LANES)
            kb_ref[0, :, sl] = k_rot[c].astype(BF16)
            vt = jnp.transpose(v_ref[0, :, sl]).astype(BF16)
            for hh in range(2):
                vt_ref[0, 2 * c + hh, 0, 0:ATT_HD, :] = vt[hh * ATT_HD:(hh + 1) * ATT_HD, :]
                vt_ref[0, 2 * c + hh, 0, ATT_HD:V_ROWS, :] = ones_row


def _dsa_prep(u, u_small, cos, sin, qg, kg, *, tr, emit_keys):
    b, t, _ = u.shape
    nb = t // DSA_TQ
    nqb = tr // DSA_TQ
    full2 = lambda i, j: (0, 0)
    seg = (jnp.arange(LANES)[:, None] // ATT_HD == jnp.arange(LANES)[None, :] // ATT_HD).astype(BF16)
    out_specs = [
        pl.BlockSpec((1, tr, ATT_KV), lambda i, j: (i, j, 0)),
        pl.BlockSpec((1, tr, LANES), lambda i, j: (i, j, 0)),
        pl.BlockSpec((1, nqb, ATT_Q, DSA_TQ), lambda i, j: (i, j, 0, 0)),
        pl.BlockSpec((1, nqb, IDX_Q, DSA_TQ), lambda i, j: (i, j, 0, 0)),
        pl.BlockSpec((1, nqb, IDX_HEADS, DSA_TQ), lambda i, j: (i, j, 0, 0)),
    ]
    out_shape = [
        jax.ShapeDtypeStruct((b, t, ATT_KV), F32),
        jax.ShapeDtypeStruct((b, t, LANES), F32),
        jax.ShapeDtypeStruct((b, nb, ATT_Q, DSA_TQ), BF16),
        jax.ShapeDtypeStruct((b, nb, IDX_Q, DSA_TQ), BF16),
        jax.ShapeDtypeStruct((b, nb, IDX_HEADS, DSA_TQ), F32),
    ]
    if emit_keys:
        out_specs += [
            pl.BlockSpec((1, tr, ATT_KV), lambda i, j: (i, j, 0)),
            pl.BlockSpec((1, tr, IDX_HD), lambda i, j: (i, j, 0)),
            pl.BlockSpec((1, KV_HEADS, 1, V_ROWS, tr), lambda i, j: (i, 0, j, 0, 0)),
        ]
        out_shape += [
            jax.ShapeDtypeStruct((b, t, ATT_KV), BF16),
            jax.ShapeDtypeStruct((b, t, IDX_HD), BF16),
            jax.ShapeDtypeStruct((b, KV_HEADS, t // tr, V_ROWS, tr), BF16),
        ]
    return pl.pallas_call(
        functools.partial(_dsa_prep_kernel, tr=tr),
        grid=(b, t // tr),
        in_specs=[
            pl.BlockSpec((1, tr, ATT_Q), lambda i, j: (i, j, 0)),
            pl.BlockSpec((1, tr, ATT_KV), lambda i, j: (i, j, 4)),
            pl.BlockSpec((1, tr, ATT_KV), lambda i, j: (i, j, 5)),
            pl.BlockSpec((1, tr, IDX_Q), lambda i, j: (i, j, 3)),
            pl.BlockSpec((1, tr, LANES), lambda i, j: (i, j, 0)),
            pl.BlockSpec((tr, LANES), lambda i, j: (j, 0)),
            pl.BlockSpec((tr, LANES), lambda i, j: (j, 0)),
            pl.BlockSpec((1, LANES), full2),
            pl.BlockSpec((1, LANES), full2),
            pl.BlockSpec((LANES, LANES), full2),
        ],
        out_specs=out_specs,
        out_shape=out_shape,
        compiler_params=_cparams(("parallel", "parallel")),
        name="dsa_prep",
    )(u, u, u, u, u_small, cos, sin, qg, kg, seg)


def _fold8(x, op, rows=8):
    r, n = x.shape
    x = x.reshape(r // rows, rows, n)
    while x.shape[0] > 1:
        h = x.shape[0] // 2
        y = op(x[:h], x[h:2 * h])
        x = y if x.shape[0] % 2 == 0 else jnp.concatenate([y, x[2 * h:]], axis=0)
    return x[0]


def _dsa_kernel(qit_ref, wit_ref, ki_ref, qt_ref, k_ref, vt_ref, o_ref,
                rhs_i_ref, rhs_q_ref, key_ref, half_ref, s_ref, acc_ref, mm_ref,
                *, tq, cw, l_keys, start, nsel):
    i = pl.program_id(1)
    cols4 = ATT_GROUPS * tq
    last_pos = start + i * tq + (tq - 1)
    max_limit = jnp.minimum((lax.shift_right_logical(last_pos, 6) + 1) * CHUNK, l_keys)
    n_c = lax.div(max_limit + (cw - 1), cw)

    pos = start + i * tq + lax.broadcasted_iota(I32, (1, tq), 1)
    limit = jnp.minimum((lax.shift_right_logical(pos, 6) + 1) * CHUNK, l_keys)
    sub_pos = lax.broadcasted_iota(I32, (cw, 1), 0)

    for h in range(IDX_HEADS):
        rhs_i_ref[:, h * tq:(h + 1) * tq] = qit_ref[0, 0, h * IDX_HD:(h + 1) * IDX_HD, :]

    @pl.when(i == 0)
    def _():
        rhs_q_ref[...] = jnp.zeros_like(rhs_q_ref)

    for j in range(KV_HEADS):
        for g in range(ATT_GROUPS):
            hq = ATT_GROUPS * j + g
            rhs_q_ref[j, j * ATT_HD:(j + 1) * ATT_HD, g * tq:(g + 1) * tq] = \
                qt_ref[0, 0, hq * ATT_HD:(hq + 1) * ATT_HD, :]

    w = [wit_ref[0, 0, h:h + 1, :] * IDX_SCALE for h in range(IDX_HEADS)]

    def score_chunk(c, carry):
        d = jnp.maximum(_dot(ki_ref[0, c], rhs_i_ref[...]), 0.0)
        score = d[:, 0:tq] * w[0]
        for h in range(1, IDX_HEADS):
            score = score + d[:, h * tq:(h + 1) * tq] * w[h]
        bits = pltpu.bitcast(score, I32)
        key = jnp.where(bits < 0, bits ^ 0x7FFFFFFF, bits)
        key = jnp.where(score == 0.0, 0, key)
        key = jnp.where(sub_pos + c * cw < limit, key, INT_MIN)
        key_ref[c] = key
        half_ref[c] = lax.shift_right_arithmetic(key, 16).astype(I16)
        return carry

    lax.fori_loop(0, n_c, score_chunk, 0)

    def count(pred):
        def body(c, acc):
            return acc + _fold8(jnp.where(pred(key_ref[c], c * cw), 1.0, 0.0), jnp.add)

        acc = lax.fori_loop(0, n_c, body, jnp.zeros((8, tq), F32))
        return jnp.sum(acc, axis=0, keepdims=True)

    def count16(cands):
        def body(c, accs):
            k = half_ref[c]
            hits = [jnp.where(k >= cand, jnp.int16(1), jnp.int16(0)) for cand in cands]
            return tuple(acc + _fold8(h, jnp.add, rows=16) for acc, h in zip(accs, hits))

        z = jnp.zeros((16, tq), I16)
        accs = lax.fori_loop(0, n_c, body, (z,) * len(cands))
        return [jnp.sum(a.astype(F32), axis=0, keepdims=True) for a in accs]

    def search16(need):
        def step(it, lo):
            d = lax.shift_left(jnp.int32(1), 14 - 2 * it)
            cnts = count16([(lo + m * d).astype(I16) for m in (1, 2, 3)])
            inc = sum(jnp.where(cnt >= need, 1, 0) for cnt in cnts)
            return lo + inc * d

        return lax.fori_loop(0, 8, step, jnp.full((1, tq), -(2 ** 15), I32))

    hi_thr = search16(nsel)
    n_above = count(lambda k, base: lax.shift_right_arithmetic(k, 16) > hi_thr)

    def low_chunk(c, carry):
        k = key_ref[c]
        low = (k & 0xFFFF) - (2 ** 15)
        half_ref[c] = jnp.where(lax.shift_right_arithmetic(k, 16) == hi_thr, low, -(2 ** 15)).astype(I16)
        return carry

    lax.fori_loop(0, n_c, low_chunk, 0)
    lo_thr = search16(nsel - n_above)
    thr = lax.shift_left(hi_thr, 16) | ((lo_thr + 2 ** 15) & 0xFFFF)

    c_gt = count(lambda k, base: k > thr)
    n_eq = count(lambda k, base: k == thr)
    need = nsel - c_gt
    excess = jnp.where((n_eq > need) & (thr > INT_MIN), 1.0, 0.0)
    mm_ref[...] = jnp.full((1, tq), 1 << 14, I32)

    @pl.when(jnp.max(excess) > 0.0)
    def _():
        def tie_search(it, m):
            cand = m + lax.shift_left(jnp.int32(1), 13 - it)
            f = count(lambda k, base: (k == thr) & (sub_pos + base < cand))
            return jnp.where(f < need, cand, m)

        mm_ref[...] = lax.fori_loop(0, 14, tie_search, jnp.zeros((1, tq), I32))

    mm = mm_ref[...]

    scale = (ATT_HD ** -0.5) * math.log2(math.e)
    m0 = jnp.full((8, cols4), -3e38, F32)
    for j0 in range(0, KV_HEADS, DSA_HPT):
        pair = tuple(range(j0, j0 + DSA_HPT))

        def qk_chunk(c, ms, pair=pair):
            k = key_ref[c]
            kpos = sub_pos + c * cw
            sel = ((k > thr) | ((k == thr) & (kpos <= mm))) & (kpos < limit)
            b1 = jnp.where(sel, 0.0, NEG_BIG)
            b4 = jnp.concatenate([b1] * ATT_GROUPS, axis=1)
            out = []
            for jj, j in enumerate(pair):
                s = _dot(k_ref[0, c], rhs_q_ref[j]) * scale + b4
                s_ref[jj, c] = s
                out.append(jnp.maximum(ms[jj], _fold8(s, jnp.maximum)))
            return tuple(out)

        ms = lax.fori_loop(0, n_c, qk_chunk, (m0,) * DSA_HPT)
        ms = [jnp.max(m, axis=0, keepdims=True) for m in ms]
        acc_ref[...] = jnp.zeros_like(acc_ref)

        def pv_chunk(c, carry, pair=pair, ms=ms):
            for jj, j in enumerate(pair):
                p = jnp.exp2(s_ref[jj, c] - ms[jj])
                acc_ref[jj] += _dot(vt_ref[0, j, c], p.astype(BF16))
            return carry

        lax.fori_loop(0, n_c, pv_chunk, 0)
        for jj, j in enumerate(pair):
            a = acc_ref[jj]
            o = a[0:ATT_HD] / a[ATT_HD:ATT_HD + 1]
            for g in range(0, ATT_GROUPS, 2):
                two = jnp.concatenate([o[:, g * tq:(g + 1) * tq], o[:, (g + 1) * tq:(g + 2) * tq]], axis=0)
                lo = (ATT_GROUPS * j + g) * ATT_HD
                o_ref[0, :, lo:lo + 2 * ATT_HD] = jnp.transpose(two).astype(o_ref.dtype)


def _dsa(qit, wit, ki, qt, k, vt, *, tq, cw, l_keys, start):
    b, nb = qit.shape[0], qit.shape[1]
    nc = k.shape[1]
    nsel = min(TOPK_MAX, l_keys // 4)
    kern = functools.partial(_dsa_kernel, tq=tq, cw=cw, l_keys=l_keys, start=start, nsel=float(nsel))
    cols4 = ATT_GROUPS * tq
    return pl.pallas_call(
        kern,
        grid=(b, nb),
        in_specs=[
            pl.BlockSpec((1, 1, IDX_Q, tq), lambda i, j: (i, j, 0, 0)),
            pl.BlockSpec((1, 1, IDX_HEADS, tq), lambda i, j: (i, j, 0, 0)),
            pl.BlockSpec((1, nc, cw, IDX_HD), lambda i, j: (i, 0, 0, 0)),
            pl.BlockSpec((1, 1, ATT_Q, tq), lambda i, j: (i, j, 0, 0)),
            pl.BlockSpec((1, nc, cw, ATT_KV), lambda i, j: (i, 0, 0, 0)),
            pl.BlockSpec((1, KV_HEADS, nc, V_ROWS, cw), lambda i, j: (i, 0, 0, 0, 0)),
        ],
        out_specs=pl.BlockSpec((1, tq, ATT_Q), lambda i, j: (i, j, 0)),
        out_shape=jax.ShapeDtypeStruct((b, nb * tq, ATT_Q), BF16),
        scratch_shapes=[
            pltpu.VMEM((IDX_HD, IDX_HEADS * tq), BF16),
            pltpu.VMEM((KV_HEADS, ATT_KV, cols4), BF16),
            pltpu.VMEM((nc, cw, tq), I32),
            pltpu.VMEM((nc, cw, tq), I16),
            pltpu.VMEM((DSA_HPT, nc, cw, cols4), F32),
            pltpu.VMEM((DSA_HPT, V_ROWS, cols4), F32),
            pltpu.VMEM((1, tq), I32),
        ],
        compiler_params=_cparams(("parallel", "arbitrary")),
        name="dsa",
    )(qit, wit, ki, qt, k, vt)


def _dsa_key_layouts(k_all, v_all, ki_all, cw):
    b, l_keys, _ = k_all.shape
    nc = -(-l_keys // cw)
    padk = lambda a: jnp.pad(a, ((0, 0), (0, nc * cw - l_keys), (0, 0))).astype(BF16)
    vt = padk(v_all).reshape(b, nc, cw, KV_HEADS, ATT_HD).transpose(0, 3, 1, 4, 2)
    ones = jnp.ones((b, KV_HEADS, nc, 1, cw), BF16)
    zeros = jnp.zeros((b, KV_HEADS, nc, V_ROWS - ATT_HD - 1, cw), BF16)
    return (padk(ki_all).reshape(b, nc, cw, IDX_HD), padk(k_all).reshape(b, nc, cw, ATT_KV),
            jnp.concatenate([vt, ones, zeros], axis=3))


def _prep_weights(ssd_w_in, att_w_in, ssd_w_out, att_w_out, mem_w_kv, dense_w_gate, dense_w_up,
                  dense_w_down, moe_router, moe_w_gate, moe_w_up, moe_w_down):
    bf = lambda w: w.astype(BF16)
    w = ssd_w_in[0]
    o_xbc, o_dt, o_mq = D_INNER, D_INNER + CONV_DIM, D_INNER + CONV_DIM + SSD_HEADS
    ssd_main = bf(jnp.concatenate([w[:, o_xbc:o_dt], w[:, o_mq:], w[:, :D_INNER]], axis=1))
    ssd_small = bf(jnp.pad(w[:, o_dt:o_mq], ((0, 0), (0, LANES - SSD_HEADS))))
    w = att_w_in[0]
    o_wi = ATT_Q + 2 * ATT_KV + IDX_Q
    o_ki = o_wi + IDX_HEADS
    o_mq = o_ki + IDX_HD
    att_main = bf(jnp.concatenate([w[:, :o_wi], w[:, o_mq:]], axis=1))
    att_small = bf(jnp.pad(jnp.concatenate([w[:, o_ki:o_mq], w[:, o_wi:o_ki]], axis=1),
                           ((0, 0), (0, LANES - IDX_HD - IDX_HEADS))))
    r = jnp.pad(moe_router[0], ((0, 0), (0, LANES - N_EXPERTS)))
    r_hi = bf(r)
    r_lo = bf(r - r_hi.astype(F32))
    return dict(
        ssd_main=ssd_main, ssd_small=ssd_small, att_main=att_main, att_small=att_small,
        ssd_out_a=bf(ssd_w_out[0, :D_INNER]), ssd_out_b=bf(ssd_w_out[0, D_INNER:]),
        att_out_a=bf(att_w_out[0, :ATT_Q]), att_out_b=bf(att_w_out[0, ATT_Q:]),
        mem_w_kv=bf(mem_w_kv), dense_g=bf(dense_w_gate[0]), dense_u=bf(dense_w_up[0]),
        dense_d=bf(dense_w_down[0]), r_hi=r_hi, r_lo=r_lo,
        moe_g=bf(moe_w_gate[0]), moe_u=bf(moe_w_up[0]), moe_d=bf(moe_w_down[0]))


def _rope_tables(pos):
    half = ATT_HD // 2
    inv = ROPE_THETA ** (-jnp.arange(half, dtype=F32) / half)
    ang = pos.astype(F32)[:, None] * inv[None, :]
    cos = jnp.cos(ang)
    sin = jnp.sin(ang)
    cos_t = jnp.concatenate([cos, cos, cos, cos], axis=1)
    sin_t = jnp.concatenate([-sin, sin, -sin, sin], axis=1)
    return cos_t, sin_t


def _trunk(x, start, mem_k, mem_v, conv_in, ssm_in, kv_in, P, W, q_ssd, cw_dsa):
    b, t, _ = x.shape
    n = b * t
    x2 = x.reshape(n, D_MODEL)

    u = _norm_matmul(x2, W['ssd_norm'][0], P['ssd_main'], tn=1024).reshape(b, t, -1)
    u_small = _norm_matmul(x2, W['ssd_norm'][0], P['ssd_small'], tn=LANES).reshape(b, t, LANES)
    conv8 = jnp.pad(conv_in, ((0, 0), (8 - (CONV_W - 1), 0), (0, 0)))
    h0_t = ssm_in.reshape(b, SSD_GROUPS, SSD_HPG, SSD_HEADDIM, D_STATE).transpose(0, 1, 4, 2, 3)
    h0_t = h0_t.reshape(b, SSD_GROUPS, D_STATE, SSD_HPG * SSD_HEADDIM)
    cw8 = jnp.pad(W['ssd_conv_w'][0], ((0, 8 - CONV_W), (0, 0)))
    pad_h = lambda v: jnp.pad(v.astype(F32), (0, LANES - SSD_HEADS)).reshape(1, LANES)
    e_mat = (jnp.arange(LANES)[:, None] == (jnp.arange(D_INNER)[None, :] // SSD_HEADDIM)).astype(BF16)
    y_mix, h_t = _ssd(u, u_small, conv8, h0_t, cw8, W['ssd_conv_b'][0].reshape(1, CONV_DIM),
                      pad_h(W['ssd_dt_bias'][0]), pad_h(W['ssd_A_log'][0]),
                      jnp.repeat(W['ssd_D'][0].astype(F32), SSD_HEADDIM).reshape(1, D_INNER),
                      W['ssd_out_norm'][0].reshape(1, D_INNER), e_mat, q_ssd)
    new_conv = u[:, t - (CONV_W - 1):, :CONV_DIM]
    new_ssm = h_t.reshape(b, SSD_GROUPS, D_STATE, SSD_HPG, SSD_HEADDIM).transpose(0, 1, 3, 4, 2)
    new_ssm = new_ssm.reshape(b, SSD_HEADS, SSD_HEADDIM, D_STATE)
    y_mem = _mem_attn(u, 3, mem_k[0], mem_v[0], W['mem_q_norm'][0])
    x2 = _proj_res(y_mix.reshape(n, D_INNER), y_mem.reshape(n, MEM_WIDTH), P['ssd_out_a'], P['ssd_out_b'], x2)
    x2 = _ffn(x2, W['ffn_norm'][0], P['dense_g'], P['dense_u'], P['dense_d'])

    u = _norm_matmul(x2, W['att_norm'][0], P['att_main'], tn=1024).reshape(b, t, -1)
    u_small = _norm_matmul(x2, W['att_norm'][0], P['att_small'], tn=LANES).reshape(b, t, LANES)
    pos = start + jnp.arange(t)
    cos_t, sin_t = _rope_tables(pos)
    tile2 = lambda v: jnp.tile(v.astype(F32), 2).reshape(1, LANES)
    v_new = u[:, :, ATT_Q + ATT_KV:ATT_Q + 2 * ATT_KV]
    qg, kg = tile2(W['att_q_norm'][0]), tile2(W['att_k_norm'][0])
    if kv_in is None:
        k_rot, ki_rot, qt, qit, wit, k_b, ki_b, vt = _dsa_prep(u, u_small, cos_t, sin_t, qg, kg,
                                                               tr=cw_dsa, emit_keys=True)
        nc = t // cw_dsa
        keys = (ki_b.reshape(b, nc, cw_dsa, IDX_HD), k_b.reshape(b, nc, cw_dsa, ATT_KV), vt)
        ki_new = ki_rot[:, :, :IDX_HD]
        l_keys = t
    else:
        padt = lambda a: jnp.pad(a, ((0, DSA_TQ - t),) + ((0, 0),) * (a.ndim - 1))
        padbt = lambda a: jnp.pad(a, ((0, 0), (0, DSA_TQ - t), (0, 0)))
        k_rot, ki_rot, qt, qit, wit = _dsa_prep(padbt(u), padbt(u_small), padt(cos_t), padt(sin_t), qg, kg,
                                                tr=DSA_TQ, emit_keys=False)
        k_rot, ki_rot = k_rot[:, :t], ki_rot[:, :t]
        ki_new = ki_rot[:, :, :IDX_HD]
        keys = _dsa_key_layouts(jnp.concatenate([kv_in[0], k_rot], axis=1),
                                jnp.concatenate([kv_in[1], v_new], axis=1),
                                jnp.concatenate([kv_in[2], ki_new], axis=1), cw_dsa)
        l_keys = kv_in[0].shape[1] + t
    o_t = _dsa(qit, wit, keys[0], qt, keys[1], keys[2], tq=DSA_TQ, cw=cw_dsa, l_keys=l_keys, start=start)
    y_mix = o_t[:, :t].reshape(n, ATT_Q)
    y_mem = _mem_attn(u, 2, mem_k[1], mem_v[1], W['mem_q_norm'][1])
    x2 = _proj_res(y_mix, y_mem.reshape(n, MEM_WIDTH), P['att_out_a'], P['att_out_b'], x2)
    x2 = _moe(x2, W['ffn_norm'][1], P['r_hi'], P['r_lo'], P['moe_g'], P['moe_u'], P['moe_d'])

    return (x2.reshape(b, t, D_MODEL), new_conv, new_ssm, k_rot.reshape(b, t, KV_HEADS, ATT_HD),
            v_new.reshape(b, t, KV_HEADS, ATT_HD), ki_new)


def kernel(x_prompt, x_sample, mem_prompt, cache_conv, state_ssm, cache_k, cache_v, cache_idx_k, cache_mem_k, cache_mem_v, ssd_norm, ssd_w_in, ssd_conv_w, ssd_conv_b, ssd_dt_bias, ssd_A_log, ssd_D, ssd_out_norm, ssd_w_out, att_norm, att_w_in, att_q_norm, att_k_norm, att_w_out, mem_norm, mem_w_kv, mem_q_norm, mem_k_norm, ffn_norm, dense_w_gate, dense_w_up, dense_w_down, moe_router, moe_w_gate, moe_w_up, moe_w_down):
    W = dict(ssd_norm=ssd_norm, ssd_conv_w=ssd_conv_w, ssd_conv_b=ssd_conv_b, ssd_dt_bias=ssd_dt_bias,
             ssd_A_log=ssd_A_log, ssd_D=ssd_D, ssd_out_norm=ssd_out_norm, att_norm=att_norm,
             att_q_norm=att_q_norm, att_k_norm=att_k_norm, mem_q_norm=mem_q_norm, ffn_norm=ffn_norm)
    P = _prep_weights(ssd_w_in, att_w_in, ssd_w_out, att_w_out, mem_w_kv, dense_w_gate, dense_w_up,
                      dense_w_down, moe_router, moe_w_gate, moe_w_up, moe_w_down)
    bp, sp = x_prompt.shape[0], x_prompt.shape[1]
    bs = x_sample.shape[0]

    mem2 = mem_prompt.reshape(bp * N_MEM, D_MODEL)
    pk, pv = [], []
    for i in range(2):
        kv = _norm_matmul(mem2, mem_norm[i], P['mem_w_kv'][i], tn=1024)
        pk.append(_head_norm(kv[:, :MEM_WIDTH], mem_k_norm[i]).reshape(bp, N_MEM, MEM_WIDTH))
        pv.append(kv[:, MEM_WIDTH:].reshape(bp, N_MEM, MEM_WIDTH))
    p_mem_k = jnp.stack(pk)
    p_mem_v = jnp.stack(pv)

    conv0 = jnp.zeros((bp, CONV_W - 1, CONV_DIM), F32)
    ssm0 = jnp.zeros((bp, SSD_HEADS, SSD_HEADDIM, D_STATE), F32)
    y_p, p_conv, p_ssm, p_k, p_v, p_ki = _trunk(x_prompt, 0, p_mem_k, p_mem_v, conv0, ssm0, None, P, W,
                                                q_ssd=128, cw_dsa=512)
    past = cache_k.shape[2]
    kv_in = (cache_k[0].reshape(bs, past, ATT_KV), cache_v[0].reshape(bs, past, ATT_KV), cache_idx_k[0])
    y_s, s_conv, s_ssm, s_k, s_v, s_ki = _trunk(
        x_sample, past, cache_mem_k.reshape(2, bs, N_MEM, MEM_WIDTH), cache_mem_v.reshape(2, bs, N_MEM, MEM_WIDTH),
        cache_conv[0], state_ssm[0], kv_in, P, W, q_ssd=x_sample.shape[1], cw_dsa=384)

    shp = (bp, N_MEM, MEM_HEADS, MEM_HD)
    return (y_p, y_s, p_conv[None], p_ssm[None], p_k[None], p_v[None], p_ki[None],
            p_mem_k.reshape((2,) + shp), p_mem_v.reshape((2,) + shp),
            s_conv[None], s_ssm[None], s_k[None], s_v[None], s_ki[None])
```

```python
import functools
import math

import jax
import jax.numpy as jnp
from jax import lax
from jax.experimental import pallas as pl
from jax.experimental.pallas import tpu as pltpu

F32 = jnp.float32
BF16 = jnp.bfloat16
I32 = jnp.int32

D_MODEL = 1024
CHUNK = 64
N_MEM = 256
EPS = 1e-6
D_INNER = 2048
SSD_HEADDIM = 64
SSD_HEADS = 32
SSD_GROUPS = 4
SSD_HPG = 8
D_STATE = 128
CONV_W = 4
GN = SSD_GROUPS * D_STATE
CONV_DIM = D_INNER + 2 * GN
ATT_HEADS = 16
ATT_HD = 64
KV_HEADS = 4
ATT_GROUPS = 4
ATT_Q = 1024
ATT_KV = 256
IDX_HEADS = 8
IDX_HD = 64
IDX_Q = 512
IDX_SCALE = (IDX_HEADS * IDX_HD) ** -0.5
TOPK_MAX = 256
ROPE_THETA = 10000.0
MEM_HEADS = 4
MEM_HD = 256
MEM_WIDTH = 1024
D_FF = 3584
N_EXPERTS = 8

LANES = 128
INT_MIN = -(2 ** 31)
NEG_BIG = -1e30
VMEM_LIMIT = 56 * 1024 * 1024


def _cparams(sem):
    return pltpu.CompilerParams(dimension_semantics=sem, vmem_limit_bytes=VMEM_LIMIT)


def _dot(a, b):
    return jnp.dot(a, b, preferred_element_type=F32)


def _split3(v):
    hi = v.astype(BF16)
    r1 = v - hi.astype(F32)
    mid = r1.astype(BF16)
    lo = (r1 - mid.astype(F32)).astype(BF16)
    return hi, mid, lo


def _silu(x):
    return x * (0.5 * jnp.tanh(0.5 * x) + 0.5)


def _norm_matmul_kernel(x_ref, g_ref, w_ref, o_ref, h_ref):
    @pl.when(pl.program_id(1) == 0)
    def _():
        x = x_ref[...]
        ms = jnp.mean(x * x, axis=-1, keepdims=True)
        h_ref[...] = (x * lax.rsqrt(ms + EPS) * g_ref[...]).astype(BF16)

    o_ref[...] = _dot(h_ref[...], w_ref[...]).astype(o_ref.dtype)


def _norm_matmul(x, g, w, tn, out_dtype=F32):
    n, k = x.shape
    m = w.shape[1]
    tm = min(1024, n)
    return pl.pallas_call(
        _norm_matmul_kernel,
        grid=(n // tm, m // tn),
        in_specs=[
            pl.BlockSpec((tm, k), lambda i, j: (i, 0)),
            pl.BlockSpec((1, k), lambda i, j: (0, 0)),
            pl.BlockSpec((k, tn), lambda i, j: (0, j)),
        ],
        out_specs=pl.BlockSpec((tm, tn), lambda i, j: (i, j)),
        out_shape=jax.ShapeDtypeStruct((n, m), out_dtype),
        scratch_shapes=[pltpu.VMEM((tm, k), BF16)],
        compiler_params=_cparams(("parallel", "arbitrary")),
        name="norm_matmul",
    )(x, g.reshape(1, k), w)


def _head_norm_kernel(x_ref, g_ref, o_ref):
    for h in range(MEM_HEADS):
        x = x_ref[:, h * MEM_HD:(h + 1) * MEM_HD]
        ms = jnp.mean(x * x, axis=-1, keepdims=True)
        o_ref[:, h * MEM_HD:(h + 1) * MEM_HD] = x * lax.rsqrt(ms + EPS) * g_ref[...]


def _head_norm(x, g):
    n = x.shape[0]
    tm = min(512, n)
    return pl.pallas_call(
        _head_norm_kernel,
        grid=(n // tm,),
        in_specs=[pl.BlockSpec((tm, MEM_WIDTH), lambda i: (i, 0)),
                  pl.BlockSpec((1, MEM_HD), lambda i: (0, 0))],
        out_specs=pl.BlockSpec((tm, MEM_WIDTH), lambda i: (i, 0)),
        out_shape=jax.ShapeDtypeStruct((n, MEM_WIDTH), F32),
        compiler_params=_cparams(("parallel",)),
        name="head_norm",
    )(x, g.reshape(1, MEM_HD))


def _mem_attn_kernel(q_ref, k_ref, v_ref, g_ref, o_ref):
    for h in range(MEM_HEADS):
        sl = slice(h * MEM_HD, (h + 1) * MEM_HD)
        q = q_ref[0, :, sl].astype(F32)
        ms = jnp.mean(q * q, axis=-1, keepdims=True)
        qn = (q * lax.rsqrt(ms + EPS) * g_ref[...]).astype(BF16)
        k = k_ref[0, :, sl].astype(BF16)
        s = lax.dot_general(qn, k, (((1,), (1,)), ((), ())), preferred_element_type=F32)
        s = s * (MEM_HD ** -0.5)
        m = jnp.max(s, axis=-1, keepdims=True)
        p = jnp.exp(s - m)
        p = p / jnp.sum(p, axis=-1, keepdims=True)
        o = _dot(p.astype(BF16), v_ref[0, :, sl].astype(BF16))
        o_ref[0, :, sl] = o.astype(o_ref.dtype)


def _mem_attn(u, col_block, mk, mv, g):
    b, t, _ = u.shape
    tq = min(512, t)
    return pl.pallas_call(
        _mem_attn_kernel,
        grid=(b, t // tq),
        in_specs=[
            pl.BlockSpec((1, tq, MEM_WIDTH), lambda i, j: (i, j, col_block)),
            pl.BlockSpec((1, N_MEM, MEM_WIDTH), lambda i, j: (i, 0, 0)),
            pl.BlockSpec((1, N_MEM, MEM_WIDTH), lambda i, j: (i, 0, 0)),
            pl.BlockSpec((1, MEM_HD), lambda i, j: (0, 0)),
        ],
        out_specs=pl.BlockSpec((1, tq, MEM_WIDTH), lambda i, j: (i, j, 0)),
        out_shape=jax.ShapeDtypeStruct((b, t, MEM_WIDTH), BF16),
        compiler_params=_cparams(("parallel", "parallel")),
        name="mem_attn",
    )(u, mk, mv, g.reshape(1, MEM_HD))


def _proj_res_kernel(a_ref, b_ref, wa_ref, wb_ref, x_ref, o_ref):
    o_ref[...] = x_ref[...] + _dot(a_ref[...], wa_ref[...]) + _dot(b_ref[...], wb_ref[...])


def _proj_res(a, b, wa, wb, x):
    n = x.shape[0]
    tm = min(512, n)
    ka, kb = a.shape[1], b.shape[1]
    return pl.pallas_call(
        _proj_res_kernel,
        grid=(n // tm,),
        in_specs=[
            pl.BlockSpec((tm, ka), lambda i: (i, 0)),
            pl.BlockSpec((tm, kb), lambda i: (i, 0)),
            pl.BlockSpec((ka, D_MODEL), lambda i: (0, 0)),
            pl.BlockSpec((kb, D_MODEL), lambda i: (0, 0)),
            pl.BlockSpec((tm, D_MODEL), lambda i: (i, 0)),
        ],
        out_specs=pl.BlockSpec((tm, D_MODEL), lambda i: (i, 0)),
        out_shape=jax.ShapeDtypeStruct((n, D_MODEL), F32),
        compiler_params=_cparams(("parallel",)),
        name="proj_res",
    )(a, b, wa, wb, x)


def _ffn_kernel(x_ref, g_ref, wg_ref, wu_ref, wd_ref, o_ref, h_ref, acc_ref):
    f = pl.program_id(1)

    @pl.when(f == 0)
    def _():
        x = x_ref[...]
        ms = jnp.mean(x * x, axis=-1, keepdims=True)
        h_ref[...] = (x * lax.rsqrt(ms + EPS) * g_ref[...]).astype(BF16)
        acc_ref[...] = x

    h = h_ref[...]
    a = _silu(_dot(h, wg_ref[...])) * _dot(h, wu_ref[...])
    acc_ref[...] += _dot(a.astype(BF16), wd_ref[...])

    @pl.when(f == pl.num_programs(1) - 1)
    def _():
        o_ref[...] = acc_ref[...]


def _ffn(x, g, wg, wu, wd, tf=896):
    n = x.shape[0]
    tm = min(1024, n)
    return pl.pallas_call(
        _ffn_kernel,
        grid=(n // tm, D_FF // tf),
        in_specs=[
            pl.BlockSpec((tm, D_MODEL), lambda i, f: (i, 0)),
            pl.BlockSpec((1, D_MODEL), lambda i, f: (0, 0)),
            pl.BlockSpec((D_MODEL, tf), lambda i, f: (0, f)),
            pl.BlockSpec((D_MODEL, tf), lambda i, f: (0, f)),
            pl.BlockSpec((tf, D_MODEL), lambda i, f: (f, 0)),
        ],
        out_specs=pl.BlockSpec((tm, D_MODEL), lambda i, f: (i, 0)),
        out_shape=jax.ShapeDtypeStruct((n, D_MODEL), F32),
        scratch_shapes=[pltpu.VMEM((tm, D_MODEL), BF16), pltpu.VMEM((tm, D_MODEL), F32)],
        compiler_params=_cparams(("parallel", "arbitrary")),
        name="ffn",
    )(x, g.reshape(1, D_MODEL), wg, wu, wd)


MOE_ROWS = 256


def _split2(v):
    hi = v.astype(BF16)
    return hi, (v - hi.astype(F32)).astype(BF16)


def _moe_route_kernel(x_ref, g_ref, rh_ref, rl_ref, h_ref, gate_ref):
    lane = lax.broadcasted_iota(I32, (1, LANES), 1)
    x = x_ref[...]
    ms = jnp.mean(x * x, axis=-1, keepdims=True)
    hf = x * lax.rsqrt(ms + EPS) * g_ref[...]
    hb, hl = _split2(hf)
    h_ref[...] = hb
    logits = _dot(hb, rh_ref[...]) + _dot(hl, rh_ref[...]) + _dot(hb, rl_ref[...])
    valid = lane < N_EXPERTS
    logits = jnp.where(valid, logits, NEG_BIG)
    m = jnp.max(logits, axis=-1, keepdims=True)
    p = jnp.exp(logits - m)
    p = p / jnp.sum(p, axis=-1, keepdims=True)
    p = jnp.where(valid, p, -1.0)
    v1 = jnp.max(p, axis=-1, keepdims=True)
    i1 = jnp.min(jnp.where(p == v1, lane, LANES), axis=-1, keepdims=True)
    p2 = jnp.where(lane == i1, -1.0, p)
    v2 = jnp.max(p2, axis=-1, keepdims=True)
    i2 = jnp.min(jnp.where(p2 == v2, lane, LANES), axis=-1, keepdims=True)
    den = v1 + v2
    gate_ref[...] = jnp.where(lane == i1, v1 / den, jnp.where(lane == i2, v2 / den, 0.0))


def _moe_route(x, g, r_hi, r_lo):
    n = x.shape[0]
    tm = min(512, n)
    return pl.pallas_call(
        _moe_route_kernel,
        grid=(n // tm,),
        in_specs=[pl.BlockSpec((tm, D_MODEL), lambda i: (i, 0)),
                  pl.BlockSpec((1, D_MODEL), lambda i: (0, 0)),
                  pl.BlockSpec((D_MODEL, LANES), lambda i: (0, 0)),
                  pl.BlockSpec((D_MODEL, LANES), lambda i: (0, 0))],
        out_specs=[pl.BlockSpec((tm, D_MODEL), lambda i: (i, 0)),
                   pl.BlockSpec((tm, LANES), lambda i: (i, 0))],
        out_shape=[jax.ShapeDtypeStruct((n, D_MODEL), BF16), jax.ShapeDtypeStruct((n, LANES), F32)],
        compiler_params=_cparams(("parallel",)),
        name="moe_route",
    )(x, g.reshape(1, D_MODEL), r_hi, r_lo)


def _moe_ffn_kernel(h_ref, gate_ref, x_ref, wg_ref, wu_ref, wd_ref, o_ref,
                    rank_ref, rank_t_ref, sel_t_ref, gate_t_ref, xe_ref, acc_ref, nb_ref, *, tm):
    e = pl.program_id(1)
    f = pl.program_id(2)
    nf = pl.num_programs(2)
    big, small = MOE_ROWS, MOE_ROWS // 2
    lane = lax.broadcasted_iota(I32, (1, LANES), 1)

    @pl.when((e == 0) & (f == 0))
    def _():
        o_ref[...] = x_ref[...]
        gates = gate_ref[...]
        sel = jnp.where(gates > 0.0, 1.0, 0.0)
        r = lax.broadcasted_iota(I32, (tm, tm), 0)
        c = lax.broadcasted_iota(I32, (tm, tm), 1)
        below = jnp.where(c < r, 1.0, 0.0).astype(BF16)
        rank = _dot(below, sel.astype(BF16))
        rank_ref[...] = rank
        rank_t_ref[...] = jnp.transpose(rank)
        sel_t_ref[...] = jnp.transpose(sel)
        gate_t_ref[...] = jnp.transpose(gates)

    @pl.when(f == 0)
    def _():
        cnt = jnp.sum(jnp.where(lane == e, jnp.sum(jnp.where(gate_ref[...] > 0.0, 1.0, 0.0), axis=0, keepdims=True),
                                0.0)).astype(I32)
        q = lax.div(cnt, big)
        rem = cnt - q * big
        merge = (q >= 1) & (rem > 0) & (rem <= small)
        kind = jnp.where(rem == 0, 0,
                         jnp.where(merge, jnp.where(rem <= small // 2, 2, 3), jnp.where(rem <= small, 1, 0)))
        nb_ref[0] = q - jnp.where(merge, 1, 0) + jnp.where(rem > small, 1, 0)
        nb_ref[1] = kind

    n_big = nb_ref[0]
    last_kind = nb_ref[1]
    r_last = pl.multiple_of(n_big * big, big)
    last_rows = (small, big + small // 2, big + small)

    def pick(r0, rows):
        want = (lax.broadcasted_iota(I32, (rows, 1), 0) + r0).astype(F32)
        return (rank_t_ref[pl.ds(e, 1), :] == want) & (sel_t_ref[pl.ds(e, 1), :] > 0.0)

    def for_blocks(fn):
        def body(rb, carry):
            fn(pl.multiple_of(rb * big, big), big)
            return carry

        lax.fori_loop(0, n_big, body, 0)
        for kind, rows in enumerate(last_rows, start=1):
            @pl.when(last_kind == kind)
            def _(rows=rows):
                fn(r_last, rows)

    def gather(r0, rows):
        p = jnp.where(pick(r0, rows), 1.0, 0.0).astype(BF16)
        xe_ref[pl.ds(r0, rows), :] = _dot(p, h_ref[...]).astype(BF16)

    def ffn(r0, rows):
        xe = xe_ref[pl.ds(r0, rows), :]
        a = _silu(_dot(xe, wg_ref[0])) * _dot(xe, wu_ref[0])
        y = _dot(a.astype(BF16), wd_ref[0])

        @pl.when(f == 0)
        def _():
            acc_ref[pl.ds(r0, rows), :] = y

        @pl.when(f > 0)
        def _():
            acc_ref[pl.ds(r0, rows), :] += y

    @pl.when(f == 0)
    def _():
        for_blocks(gather)

    for_blocks(ffn)

    @pl.when(f == nf - 1)
    def _():
        rank_col = jnp.sum(jnp.where(lane == e, rank_ref[...], 0.0), axis=1, keepdims=True)
        sel_col = jnp.sum(jnp.where(lane == e, gate_ref[...], 0.0), axis=1, keepdims=True) > 0.0

        def scatter(r0, rows):
            g_rows = jnp.sum(jnp.where(pick(r0, rows), gate_t_ref[pl.ds(e, 1), :], 0.0), axis=1, keepdims=True)
            y = (acc_ref[pl.ds(r0, rows), :] * g_rows).astype(BF16)
            col = (lax.broadcasted_iota(I32, (1, rows), 1) + r0).astype(F32)
            put = jnp.where((rank_col == col) & sel_col, 1.0, 0.0).astype(BF16)
            o_ref[...] += _dot(put, y)

        for_blocks(scatter)


def _moe_ffn(h, gates, x, wg, wu, wd, *, tm=1024, tf=1792):
    n = x.shape[0]
    tm = min(tm, n)
    kern = functools.partial(_moe_ffn_kernel, tm=tm)
    return pl.pallas_call(
        kern,
        grid=(n // tm, N_EXPERTS, D_FF // tf),
        in_specs=[
            pl.BlockSpec((tm, D_MODEL), lambda i, e, f: (i, 0)),
            pl.BlockSpec((tm, LANES), lambda i, e, f: (i, 0)),
            pl.BlockSpec((tm, D_MODEL), lambda i, e, f: (i, 0)),
            pl.BlockSpec((1, D_MODEL, tf), lambda i, e, f: (e, 0, f)),
            pl.BlockSpec((1, D_MODEL, tf), lambda i, e, f: (e, 0, f)),
            pl.BlockSpec((1, tf, D_MODEL), lambda i, e, f: (e, f, 0)),
        ],
        out_specs=pl.BlockSpec((tm, D_MODEL), lambda i, e, f: (i, 0)),
        out_shape=jax.ShapeDtypeStruct((n, D_MODEL), F32),
        scratch_shapes=[
            pltpu.VMEM((tm, LANES), F32),
            pltpu.VMEM((LANES, tm), F32),
            pltpu.VMEM((LANES, tm), F32),
            pltpu.VMEM((LANES, tm), F32),
            pltpu.VMEM((tm, D_MODEL), BF16),
            pltpu.VMEM((tm, D_MODEL), F32),
            pltpu.SMEM((2,), I32),
        ],
        compiler_params=_cparams(("parallel", "arbitrary", "arbitrary")),
        name="moe_ffn",
    )(h, gates, x, wg, wu, wd)


def _moe(x, g, r_hi, r_lo, wg, wu, wd):
    h, gates = _moe_route(x, g, r_hi, r_lo)
    return _moe_ffn(h, gates, x, wg, wu, wd)


def _ssd_kernel(xbc_ref, z_ref, dt_ref, cs_ref, h0_ref, cw_ref, cb_ref, dtb_ref, alog_ref,
                dexp_ref, og_ref, e_ref, y_ref, hout_ref, xpad_ref, h_ref, *, q):
    c = pl.program_id(1)

    @pl.when(c == 0)
    def _():
        xpad_ref[0:8, :] = cs_ref[0]
        h_ref[...] = h0_ref[0]

    xbc = xbc_ref[0].astype(F32)
    xpad_ref[8:8 + q, :] = xbc
    cw = cw_ref[...]
    conv = cb_ref[...] + (xpad_ref[5:5 + q, :] * cw[0:1] + xpad_ref[6:6 + q, :] * cw[1:2]
                          + xpad_ref[7:7 + q, :] * cw[2:3] + xbc * cw[3:4])
    xpad_ref[0:8, :] = xpad_ref[q:q + 8, :]
    act = _silu(conv)

    lane = lax.broadcasted_iota(I32, (1, LANES), 1)
    xdt = dt_ref[0] + dtb_ref[...]
    sp = jnp.maximum(xdt, 0.0) + jnp.log1p(jnp.exp(-jnp.abs(xdt)))
    dt = jnp.where(lane < SSD_HEADS, sp, 0.0)
    a = dt * (-jnp.exp(alog_ref[...]))

    rows = lax.broadcasted_iota(I32, (q, q), 0)
    cols = lax.broadcasted_iota(I32, (q, q), 1)
    causal = rows >= cols
    tril = jnp.where(causal, 1.0, 0.0).astype(BF16)
    a3 = _split3(a)
    acum = _dot(tril, a3[0]) + _dot(tril, a3[1]) + _dot(tril, a3[2])
    acum_t = jnp.transpose(acum)

    e_mat = e_ref[...]
    ac3 = _split3(acum)
    acum_x = _dot(ac3[0], e_mat) + _dot(ac3[1], e_mat) + _dot(ac3[2], e_mat)
    dt3 = _split3(dt)
    dt_x = _dot(dt3[0], e_mat) + _dot(dt3[1], e_mat) + _dot(dt3[2], e_mat)
    last = acum_x[q - 1:q, :]
    eac_x = jnp.exp(acum_x)
    dte_x = jnp.exp(last - acum_x)
    blkdec = jnp.exp(last)

    xs = act[:, :D_INNER]
    x_dt = xs * dt_x
    xb = x_dt.astype(BF16)
    xdb = (x_dt * dte_x).astype(BF16)
    z = z_ref[0].astype(F32)
    lane_lo = lane < SSD_HEADDIM

    gw = SSD_HPG * SSD_HEADDIM
    for g in range(SSD_GROUPS):
        bg = act[:, D_INNER + g * D_STATE:D_INNER + (g + 1) * D_STATE]
        cg = act[:, D_INNER + GN + g * D_STATE:D_INNER + GN + (g + 1) * D_STATE]
        cgb = cg.astype(BF16)
        bgt = jnp.transpose(bg).astype(BF16)
        cb = _dot(cgb, bgt)
        gs = slice(g * gw, (g + 1) * gw)
        h_in = h_ref[g]
        y_off = _dot(cgb, h_in.astype(BF16)) * eac_x[:, gs]
        h_ref[g] = h_in * blkdec[:, gs] + _dot(bgt, xdb[:, gs])
        parts = []
        for pr in range(SSD_HPG // 2):
            h0 = g * SSD_HPG + 2 * pr
            xp = xb[:, (h0 // 2) * LANES:(h0 // 2 + 1) * LANES]
            ys = []
            for hh in (h0, h0 + 1):
                seg = acum[:, hh:hh + 1] - acum_t[hh:hh + 1, :]
                lm = jnp.exp(jnp.where(causal, seg, NEG_BIG))
                ys.append(_dot((cb * lm).astype(BF16), xp))
            parts.append(jnp.where(lane_lo, ys[0], ys[1]))
        y_g = jnp.concatenate(parts, axis=1) + y_off + dexp_ref[:, gs] * xs[:, gs]
        gt = y_g * _silu(z[:, gs])
        gn = gt * lax.rsqrt(jnp.mean(gt * gt, axis=-1, keepdims=True) + EPS) * og_ref[:, gs]
        y_ref[0, :, gs] = gn.astype(y_ref.dtype)

    @pl.when(c == pl.num_programs(1) - 1)
    def _():
        hout_ref[0] = h_ref[...]


def _ssd(u, u_small, conv_state8, h0_t, cw8, cb, dtb, alog, dexp, og, e_mat, q):
    b, t, _ = u.shape
    kern = functools.partial(_ssd_kernel, q=q)
    full2 = lambda i, c: (0, 0)
    return pl.pallas_call(
        kern,
        grid=(b, t // q),
        in_specs=[
            pl.BlockSpec((1, q, CONV_DIM), lambda i, c: (i, c, 0)),
            pl.BlockSpec((1, q, D_INNER), lambda i, c: (i, c, 2)),
            pl.BlockSpec((1, q, LANES), lambda i, c: (i, c, 0)),
            pl.BlockSpec((1, 8, CONV_DIM), lambda i, c: (i, 0, 0)),
            pl.BlockSpec((1, SSD_GROUPS, D_STATE, SSD_HPG * SSD_HEADDIM), lambda i, c: (i, 0, 0, 0)),
            pl.BlockSpec((8, CONV_DIM), full2),
            pl.BlockSpec((1, CONV_DIM), full2),
            pl.BlockSpec((1, LANES), full2),
            pl.BlockSpec((1, LANES), full2),
            pl.BlockSpec((1, D_INNER), full2),
            pl.BlockSpec((1, D_INNER), full2),
            pl.BlockSpec((LANES, D_INNER), full2),
        ],
        out_specs=[
            pl.BlockSpec((1, q, D_INNER), lambda i, c: (i, c, 0)),
            pl.BlockSpec((1, SSD_GROUPS, D_STATE, SSD_HPG * SSD_HEADDIM), lambda i, c: (i, 0, 0, 0)),
        ],
        out_shape=[
            jax.ShapeDtypeStruct((b, t, D_INNER), BF16),
            jax.ShapeDtypeStruct((b, SSD_GROUPS, D_STATE, SSD_HPG * SSD_HEADDIM), F32),
        ],
        scratch_shapes=[pltpu.VMEM((q + 8, CONV_DIM), F32),
                        pltpu.VMEM((SSD_GROUPS, D_STATE, SSD_HPG * SSD_HEADDIM), F32)],
        compiler_params=_cparams(("parallel", "arbitrary")),
        name="ssd",
    )(u, u, u_small, conv_state8, h0_t, cw8, cb, dtb, alog, dexp, og, e_mat)


V_ROWS = 80
DSA_TQ = LANES
DSA_HPT = 4


def _dsa_prep_kernel(q_ref, k_ref, v_ref, qi_ref, sm_ref, cos_ref, sin_ref, qg_ref, kg_ref, seg_ref,
                     ko_ref, kio_ref, qt_ref, qit_ref, wit_ref, *key_refs, tr):
    cos = cos_ref[...]
    sin = sin_ref[...]
    seg = seg_ref[...]
    lane = lax.broadcasted_iota(I32, (1, LANES), 1)
    nqb = tr // DSA_TQ

    def head_norm(x):
        s3 = _split3(x * x)
        ss = _dot(s3[0], seg) + _dot(s3[1], seg) + _dot(s3[2], seg)
        return x * lax.rsqrt(ss * (1.0 / ATT_HD) + EPS)

    def rope(x):
        partner = jnp.where((lane & 32) == 0, pltpu.roll(x, LANES - 32, 1), pltpu.roll(x, 32, 1))
        return x * cos + partner * sin

    def put_t(dst_ref, row0, x):
        for qb in range(nqb):
            xt = jnp.transpose(x[qb * DSA_TQ:(qb + 1) * DSA_TQ, :])
            dst_ref[0, qb, row0:row0 + LANES, :] = xt.astype(dst_ref.dtype)

    for c in range(ATT_Q // LANES):
        sl = slice(c * LANES, (c + 1) * LANES)
        put_t(qt_ref, c * LANES, rope(head_norm(q_ref[0, :, sl].astype(F32)) * qg_ref[...]))
    for c in range(IDX_Q // LANES):
        sl = slice(c * LANES, (c + 1) * LANES)
        put_t(qit_ref, c * LANES, rope(qi_ref[0, :, sl].astype(F32)))
    sm = sm_ref[0]
    for qb in range(nqb):
        wit_ref[0, qb] = jnp.transpose(sm[qb * DSA_TQ:(qb + 1) * DSA_TQ, :])[IDX_HD:IDX_HD + IDX_HEADS, :]
    ki = rope(sm)
    kio_ref[0] = ki
    k_rot = []
    for c in range(ATT_KV // LANES):
        sl = slice(c * LANES, (c + 1) * LANES)
        k_rot.append(rope(head_norm(k_ref[0, :, sl].astype(F32)) * kg_ref[...]))
        ko_ref[0, :, sl] = k_rot[c]

    if key_refs:
        kb_ref, kib_ref, vt_ref = key_refs
        kib_ref[0] = ki[:, :IDX_HD].astype(BF16)
        ones_row = jnp.where(lax.broadcasted_iota(I32, (V_ROWS - ATT_HD, tr), 0) == 0, 1.0, 0.0).astype(BF16)
        for c in range(ATT_KV // LANES):
            sl = slice(c * LANES, (c + 1) * LANES)
            kb_ref[0, :, sl] = k_rot[c].astype(BF16)
            vt = jnp.transpose(v_ref[0, :, sl].astype(F32)).astype(BF16)
            for hh in range(2):
                vt_ref[0, 2 * c + hh, 0, 0:ATT_HD, :] = vt[hh * ATT_HD:(hh + 1) * ATT_HD, :]
                vt_ref[0, 2 * c + hh, 0, ATT_HD:V_ROWS, :] = ones_row


def _dsa_prep(u, u_small, cos, sin, qg, kg, *, tr, emit_keys):
    b, t, _ = u.shape
    nb = t // DSA_TQ
    nqb = tr // DSA_TQ
    full2 = lambda i, j: (0, 0)
    seg = (jnp.arange(LANES)[:, None] // ATT_HD == jnp.arange(LANES)[None, :] // ATT_HD).astype(BF16)
    out_specs = [
        pl.BlockSpec((1, tr, ATT_KV), lambda i, j: (i, j, 0)),
        pl.BlockSpec((1, tr, LANES), lambda i, j: (i, j, 0)),
        pl.BlockSpec((1, nqb, ATT_Q, DSA_TQ), lambda i, j: (i, j, 0, 0)),
        pl.BlockSpec((1, nqb, IDX_Q, DSA_TQ), lambda i, j: (i, j, 0, 0)),
        pl.BlockSpec((1, nqb, IDX_HEADS, DSA_TQ), lambda i, j: (i, j, 0, 0)),
    ]
    out_shape = [
        jax.ShapeDtypeStruct((b, t, ATT_KV), F32),
        jax.ShapeDtypeStruct((b, t, LANES), F32),
        jax.ShapeDtypeStruct((b, nb, ATT_Q, DSA_TQ), BF16),
        jax.ShapeDtypeStruct((b, nb, IDX_Q, DSA_TQ), BF16),
        jax.ShapeDtypeStruct((b, nb, IDX_HEADS, DSA_TQ), F32),
    ]
    if emit_keys:
        out_specs += [
            pl.BlockSpec((1, tr, ATT_KV), lambda i, j: (i, j, 0)),
            pl.BlockSpec((1, tr, IDX_HD), lambda i, j: (i, j, 0)),
            pl.BlockSpec((1, KV_HEADS, 1, V_ROWS, tr), lambda i, j: (i, 0, j, 0, 0)),
        ]
        out_shape += [
            jax.ShapeDtypeStruct((b, t, ATT_KV), BF16),
            jax.ShapeDtypeStruct((b, t, IDX_HD), BF16),
            jax.ShapeDtypeStruct((b, KV_HEADS, t // tr, V_ROWS, tr), BF16),
        ]
    return pl.pallas_call(
        functools.partial(_dsa_prep_kernel, tr=tr),
        grid=(b, t // tr),
        in_specs=[
            pl.BlockSpec((1, tr, ATT_Q), lambda i, j: (i, j, 0)),
            pl.BlockSpec((1, tr, ATT_KV), lambda i, j: (i, j, 4)),
            pl.BlockSpec((1, tr, ATT_KV), lambda i, j: (i, j, 5)),
            pl.BlockSpec((1, tr, IDX_Q), lambda i, j: (i, j, 3)),
            pl.BlockSpec((1, tr, LANES), lambda i, j: (i, j, 0)),
            pl.BlockSpec((tr, LANES), lambda i, j: (j, 0)),
            pl.BlockSpec((tr, LANES), lambda i, j: (j, 0)),
            pl.BlockSpec((1, LANES), full2),
            pl.BlockSpec((1, LANES), full2),
            pl.BlockSpec((LANES, LANES), full2),
        ],
        out_specs=out_specs,
        out_shape=out_shape,
        compiler_params=_cparams(("parallel", "parallel")),
        name="dsa_prep",
    )(u, u, u, u, u_small, cos, sin, qg, kg, seg)


def _fold8(x, op, rows=8):
    r, n = x.shape
    x = x.reshape(r // rows, rows, n)
    while x.shape[0] > 1:
        h = x.shape[0] // 2
        y = op(x[:h], x[h:2 * h])
        x = y if x.shape[0] % 2 == 0 else jnp.concatenate([y, x[2 * h:]], axis=0)
    return x[0]


def _dsa_kernel(qit_ref, wit_ref, ki_ref, qt_ref, k_ref, vt_ref, o_ref,
                rhs_i_ref, rhs_q_ref, key_ref, s_ref, acc_ref, mm_ref,
                *, tq, cw, l_keys, start, nsel):
    i = pl.program_id(1)
    cols4 = ATT_GROUPS * tq
    last_pos = start + i * tq + (tq - 1)
    max_limit = jnp.minimum((lax.shift_right_logical(last_pos, 6) + 1) * CHUNK, l_keys)
    n_c = lax.div(max_limit + (cw - 1), cw)

    pos = start + i * tq + lax.broadcasted_iota(I32, (1, tq), 1)
    limit = jnp.minimum((lax.shift_right_logical(pos, 6) + 1) * CHUNK, l_keys)
    sub_pos = lax.broadcasted_iota(I32, (cw, 1), 0)

    for h in range(IDX_HEADS):
        rhs_i_ref[:, h * tq:(h + 1) * tq] = qit_ref[0, 0, h * IDX_HD:(h + 1) * IDX_HD, :]

    @pl.when(i == 0)
    def _():
        rhs_q_ref[...] = jnp.zeros_like(rhs_q_ref)

    for j in range(KV_HEADS):
        for g in range(ATT_GROUPS):
            hq = ATT_GROUPS * j + g
            rhs_q_ref[j, j * ATT_HD:(j + 1) * ATT_HD, g * tq:(g + 1) * tq] = \
                qt_ref[0, 0, hq * ATT_HD:(hq + 1) * ATT_HD, :]

    w = [wit_ref[0, 0, h:h + 1, :] * IDX_SCALE for h in range(IDX_HEADS)]

    def score_chunk(c, carry):
        d = jnp.maximum(_dot(ki_ref[0, c], rhs_i_ref[...]), 0.0)
        score = d[:, 0:tq] * w[0]
        for h in range(1, IDX_HEADS):
            score = score + d[:, h * tq:(h + 1) * tq] * w[h]
        bits = pltpu.bitcast(score, I32)
        key = jnp.where(bits < 0, bits ^ 0x7FFFFFFF, bits)
        key = jnp.where(score == 0.0, 0, key)
        key_ref[c] = jnp.where(sub_pos + c * cw < limit, key, INT_MIN)
        return carry

    lax.fori_loop(0, n_c, score_chunk, 0)

    def count(pred):
        def body(c, acc):
            return acc + _fold8(jnp.where(pred(key_ref[c], c * cw), 1.0, 0.0), jnp.add)

        acc = lax.fori_loop(0, n_c, body, jnp.zeros((8, tq), F32))
        return jnp.sum(acc, axis=0, keepdims=True)

    def search(it, lo):
        cand = lo + lax.shift_left(jnp.int32(1), 31 - it)
        cnt = count(lambda k, base: k >= cand)
        return jnp.where(cnt >= nsel, cand, lo)

    thr = lax.fori_loop(0, 32, search, jnp.full((1, tq), INT_MIN, I32))

    c_gt = count(lambda k, base: k > thr)
    n_eq = count(lambda k, base: k == thr)
    need = nsel - c_gt
    excess = jnp.where((n_eq > need) & (thr > INT_MIN), 1.0, 0.0)
    mm_ref[...] = jnp.full((1, tq), 1 << 14, I32)

    @pl.when(jnp.max(excess) > 0.0)
    def _():
        def tie_search(it, m):
            cand = m + lax.shift_left(jnp.int32(1), 13 - it)
            f = count(lambda k, base: (k == thr) & (sub_pos + base < cand))
            return jnp.where(f < need, cand, m)

        mm_ref[...] = lax.fori_loop(0, 14, tie_search, jnp.zeros((1, tq), I32))

    mm = mm_ref[...]

    scale = (ATT_HD ** -0.5) * math.log2(math.e)
    m0 = jnp.full((8, cols4), -3e38, F32)
    for j0 in range(0, KV_HEADS, DSA_HPT):
        pair = tuple(range(j0, j0 + DSA_HPT))

        def qk_chunk(c, ms, pair=pair):
            k = key_ref[c]
            kpos = sub_pos + c * cw
            sel = ((k > thr) | ((k == thr) & (kpos <= mm))) & (kpos < limit)
            b1 = jnp.where(sel, 0.0, NEG_BIG)
            b4 = jnp.concatenate([b1] * ATT_GROUPS, axis=1)
            out = []
            for jj, j in enumerate(pair):
                s = _dot(k_ref[0, c], rhs_q_ref[j]) * scale + b4
                s_ref[jj, c] = s
                out.append(jnp.maximum(ms[jj], _fold8(s, jnp.maximum)))
            return tuple(out)

        ms = lax.fori_loop(0, n_c, qk_chunk, (m0,) * DSA_HPT)
        ms = [jnp.max(m, axis=0, keepdims=True) for m in ms]
        acc_ref[...] = jnp.zeros_like(acc_ref)

        def pv_chunk(c, carry, pair=pair, ms=ms):
            for jj, j in enumerate(pair):
                p = jnp.exp2(s_ref[jj, c] - ms[jj])
                acc_ref[jj] += _dot(vt_ref[0, j, c], p.astype(BF16))
            return carry

        lax.fori_loop(0, n_c, pv_chunk, 0)
        for jj, j in enumerate(pair):
            a = acc_ref[jj]
            o = a[0:ATT_HD] / a[ATT_HD:ATT_HD + 1]
            for g in range(0, ATT_GROUPS, 2):
                two = jnp.concatenate([o[:, g * tq:(g + 1) * tq], o[:, (g + 1) * tq:(g + 2) * tq]], axis=0)
                lo = (ATT_GROUPS * j + g) * ATT_HD
                o_ref[0, :, lo:lo + 2 * ATT_HD] = jnp.transpose(two).astype(o_ref.dtype)


def _dsa(qit, wit, ki, qt, k, vt, *, tq, cw, l_keys, start):
    b, nb = qit.shape[0], qit.shape[1]
    nc = k.shape[1]
    nsel = min(TOPK_MAX, l_keys // 4)
    kern = functools.partial(_dsa_kernel, tq=tq, cw=cw, l_keys=l_keys, start=start, nsel=float(nsel))
    cols4 = ATT_GROUPS * tq
    return pl.pallas_call(
        kern,
        grid=(b, nb),
        in_specs=[
            pl.BlockSpec((1, 1, IDX_Q, tq), lambda i, j: (i, j, 0, 0)),
            pl.BlockSpec((1, 1, IDX_HEADS, tq), lambda i, j: (i, j, 0, 0)),
            pl.BlockSpec((1, nc, cw, IDX_HD), lambda i, j: (i, 0, 0, 0)),
            pl.BlockSpec((1, 1, ATT_Q, tq), lambda i, j: (i, j, 0, 0)),
            pl.BlockSpec((1, nc, cw, ATT_KV), lambda i, j: (i, 0, 0, 0)),
            pl.BlockSpec((1, KV_HEADS, nc, V_ROWS, cw), lambda i, j: (i, 0, 0, 0, 0)),
        ],
        out_specs=pl.BlockSpec((1, tq, ATT_Q), lambda i, j: (i, j, 0)),
        out_shape=jax.ShapeDtypeStruct((b, nb * tq, ATT_Q), BF16),
        scratch_shapes=[
            pltpu.VMEM((IDX_HD, IDX_HEADS * tq), BF16),
            pltpu.VMEM((KV_HEADS, ATT_KV, cols4), BF16),
            pltpu.VMEM((nc, cw, tq), I32),
            pltpu.VMEM((DSA_HPT, nc, cw, cols4), F32),
            pltpu.VMEM((DSA_HPT, V_ROWS, cols4), F32),
            pltpu.VMEM((1, tq), I32),
        ],
        compiler_params=_cparams(("parallel", "arbitrary")),
        name="dsa",
    )(qit, wit, ki, qt, k, vt)


def _dsa_key_layouts(k_all, v_all, ki_all, cw):
    b, l_keys, _ = k_all.shape
    nc = -(-l_keys // cw)
    padk = lambda a: jnp.pad(a, ((0, 0), (0, nc * cw - l_keys), (0, 0))).astype(BF16)
    vt = padk(v_all).reshape(b, nc, cw, KV_HEADS, ATT_HD).transpose(0, 3, 1, 4, 2)
    ones = jnp.ones((b, KV_HEADS, nc, 1, cw), BF16)
    zeros = jnp.zeros((b, KV_HEADS, nc, V_ROWS - ATT_HD - 1, cw), BF16)
    return (padk(ki_all).reshape(b, nc, cw, IDX_HD), padk(k_all).reshape(b, nc, cw, ATT_KV),
            jnp.concatenate([vt, ones, zeros], axis=3))


def _prep_weights(ssd_w_in, att_w_in, ssd_w_out, att_w_out, mem_w_kv, dense_w_gate, dense_w_up,
                  dense_w_down, moe_router, moe_w_gate, moe_w_up, moe_w_down):
    bf = lambda w: w.astype(BF16)
    w = ssd_w_in[0]
    o_xbc, o_dt, o_mq = D_INNER, D_INNER + CONV_DIM, D_INNER + CONV_DIM + SSD_HEADS
    ssd_main = bf(jnp.concatenate([w[:, o_xbc:o_dt], w[:, o_mq:], w[:, :D_INNER]], axis=1))
    ssd_small = bf(jnp.pad(w[:, o_dt:o_mq], ((0, 0), (0, LANES - SSD_HEADS))))
    w = att_w_in[0]
    o_wi = ATT_Q + 2 * ATT_KV + IDX_Q
    o_ki = o_wi + IDX_HEADS
    o_mq = o_ki + IDX_HD
    att_main = bf(jnp.concatenate([w[:, :o_wi], w[:, o_mq:]], axis=1))
    att_small = bf(jnp.pad(jnp.concatenate([w[:, o_ki:o_mq], w[:, o_wi:o_ki]], axis=1),
                           ((0, 0), (0, LANES - IDX_HD - IDX_HEADS))))
    r = jnp.pad(moe_router[0], ((0, 0), (0, LANES - N_EXPERTS)))
    r_hi = bf(r)
    r_lo = bf(r - r_hi.astype(F32))
    return dict(
        ssd_main=ssd_main, ssd_small=ssd_small, att_main=att_main, att_small=att_small,
        ssd_out_a=bf(ssd_w_out[0, :D_INNER]), ssd_out_b=bf(ssd_w_out[0, D_INNER:]),
        att_out_a=bf(att_w_out[0, :ATT_Q]), att_out_b=bf(att_w_out[0, ATT_Q:]),
        mem_w_kv=bf(mem_w_kv), dense_g=bf(dense_w_gate[0]), dense_u=bf(dense_w_up[0]),
        dense_d=bf(dense_w_down[0]), r_hi=r_hi, r_lo=r_lo,
        moe_g=bf(moe_w_gate[0]), moe_u=bf(moe_w_up[0]), moe_d=bf(moe_w_down[0]))


def _rope_tables(pos):
    half = ATT_HD // 2
    inv = ROPE_THETA ** (-jnp.arange(half, dtype=F32) / half)
    ang = pos.astype(F32)[:, None] * inv[None, :]
    cos = jnp.cos(ang)
    sin = jnp.sin(ang)
    cos_t = jnp.concatenate([cos, cos, cos, cos], axis=1)
    sin_t = jnp.concatenate([-sin, sin, -sin, sin], axis=1)
    return cos_t, sin_t


def _trunk(x, start, mem_k, mem_v, conv_in, ssm_in, kv_in, P, W, q_ssd, cw_dsa):
    b, t, _ = x.shape
    n = b * t
    x2 = x.reshape(n, D_MODEL)

    u = _norm_matmul(x2, W['ssd_norm'][0], P['ssd_main'], tn=1024, out_dtype=BF16).reshape(b, t, -1)
    u_small = _norm_matmul(x2, W['ssd_norm'][0], P['ssd_small'], tn=LANES).reshape(b, t, LANES)
    conv8 = jnp.pad(conv_in, ((0, 0), (8 - (CONV_W - 1), 0), (0, 0)))
    h0_t = ssm_in.reshape(b, SSD_GROUPS, SSD_HPG, SSD_HEADDIM, D_STATE).transpose(0, 1, 4, 2, 3)
    h0_t = h0_t.reshape(b, SSD_GROUPS, D_STATE, SSD_HPG * SSD_HEADDIM)
    cw8 = jnp.pad(W['ssd_conv_w'][0], ((0, 8 - CONV_W), (0, 0)))
    pad_h = lambda v: jnp.pad(v.astype(F32), (0, LANES - SSD_HEADS)).reshape(1, LANES)
    e_mat = (jnp.arange(LANES)[:, None] == (jnp.arange(D_INNER)[None, :] // SSD_HEADDIM)).astype(BF16)
    y_mix, h_t = _ssd(u, u_small, conv8, h0_t, cw8, W['ssd_conv_b'][0].reshape(1, CONV_DIM),
                      pad_h(W['ssd_dt_bias'][0]), pad_h(W['ssd_A_log'][0]),
                      jnp.repeat(W['ssd_D'][0].astype(F32), SSD_HEADDIM).reshape(1, D_INNER),
                      W['ssd_out_norm'][0].reshape(1, D_INNER), e_mat, q_ssd)
    new_conv = u[:, t - (CONV_W - 1):, :CONV_DIM].astype(F32)
    new_ssm = h_t.reshape(b, SSD_GROUPS, D_STATE, SSD_HPG, SSD_HEADDIM).transpose(0, 1, 3, 4, 2)
    new_ssm = new_ssm.reshape(b, SSD_HEADS, SSD_HEADDIM, D_STATE)
    y_mem = _mem_attn(u, 3, mem_k[0], mem_v[0], W['mem_q_norm'][0])
    x2 = _proj_res(y_mix.reshape(n, D_INNER), y_mem.reshape(n, MEM_WIDTH), P['ssd_out_a'], P['ssd_out_b'], x2)
    x2 = _ffn(x2, W['ffn_norm'][0], P['dense_g'], P['dense_u'], P['dense_d'])

    u = _norm_matmul(x2, W['att_norm'][0], P['att_main'], tn=1024, out_dtype=BF16).reshape(b, t, -1)
    u_small = _norm_matmul(x2, W['att_norm'][0], P['att_small'], tn=LANES).reshape(b, t, LANES)
    pos = start + jnp.arange(t)
    cos_t, sin_t = _rope_tables(pos)
    tile2 = lambda v: jnp.tile(v.astype(F32), 2).reshape(1, LANES)
    v_new = u[:, :, ATT_Q + ATT_KV:ATT_Q + 2 * ATT_KV].astype(F32)
    qg, kg = tile2(W['att_q_norm'][0]), tile2(W['att_k_norm'][0])
    if kv_in is None:
        k_rot, ki_rot, qt, qit, wit, k_b, ki_b, vt = _dsa_prep(u, u_small, cos_t, sin_t, qg, kg,
                                                               tr=cw_dsa, emit_keys=True)
        nc = t // cw_dsa
        keys = (ki_b.reshape(b, nc, cw_dsa, IDX_HD), k_b.reshape(b, nc, cw_dsa, ATT_KV), vt)
        ki_new = ki_rot[:, :, :IDX_HD]
        l_keys = t
    else:
        padt = lambda a: jnp.pad(a, ((0, DSA_TQ - t),) + ((0, 0),) * (a.ndim - 1))
        padbt = lambda a: jnp.pad(a, ((0, 0), (0, DSA_TQ - t), (0, 0)))
        k_rot, ki_rot, qt, qit, wit = _dsa_prep(padbt(u), padbt(u_small), padt(cos_t), padt(sin_t), qg, kg,
                                                tr=DSA_TQ, emit_keys=False)
        k_rot, ki_rot = k_rot[:, :t], ki_rot[:, :t]
        ki_new = ki_rot[:, :, :IDX_HD]
        keys = _dsa_key_layouts(jnp.concatenate([kv_in[0], k_rot], axis=1),
                                jnp.concatenate([kv_in[1], v_new], axis=1),
                                jnp.concatenate([kv_in[2], ki_new], axis=1), cw_dsa)
        l_keys = kv_in[0].shape[1] + t
    o_t = _dsa(qit, wit, keys[0], qt, keys[1], keys[2], tq=DSA_TQ, cw=cw_dsa, l_keys=l_keys, start=start)
    y_mix = o_t[:, :t].reshape(n, ATT_Q)
    y_mem = _mem_attn(u, 2, mem_k[1], mem_v[1], W['mem_q_norm'][1])
    x2 = _proj_res(y_mix, y_mem.reshape(n, MEM_WIDTH), P['att_out_a'], P['att_out_b'], x2)
    x2 = _moe(x2, W['ffn_norm'][1], P['r_hi'], P['r_lo'], P['moe_g'], P['moe_u'], P['moe_d'])

    return (x2.reshape(b, t, D_MODEL), new_conv, new_ssm, k_rot.reshape(b, t, KV_HEADS, ATT_HD),
            v_new.reshape(b, t, KV_HEADS, ATT_HD), ki_new)


def kernel(x_prompt, x_sample, mem_prompt, cache_conv, state_ssm, cache_k, cache_v, cache_idx_k, cache_mem_k, cache_mem_v, ssd_norm, ssd_w_in, ssd_conv_w, ssd_conv_b, ssd_dt_bias, ssd_A_log, ssd_D, ssd_out_norm, ssd_w_out, att_norm, att_w_in, att_q_norm, att_k_norm, att_w_out, mem_norm, mem_w_kv, mem_q_norm, mem_k_norm, ffn_norm, dense_w_gate, dense_w_up, dense_w_down, moe_router, moe_w_gate, moe_w_up, moe_w_down):
    W = dict(ssd_norm=ssd_norm, ssd_conv_w=ssd_conv_w, ssd_conv_b=ssd_conv_b, ssd_dt_bias=ssd_dt_bias,
             ssd_A_log=ssd_A_log, ssd_D=ssd_D, ssd_out_norm=ssd_out_norm, att_norm=att_norm,
             att_q_norm=att_q_norm, att_k_norm=att_k_norm, mem_q_norm=mem_q_norm, ffn_norm=ffn_norm)
    P = _prep_weights(ssd_w_in, att_w_in, ssd_w_out, att_w_out, mem_w_kv, dense_w_gate, dense_w_up,
                      dense_w_down, moe_router, moe_w_gate, moe_w_up, moe_w_down)
    bp, sp = x_prompt.shape[0], x_prompt.shape[1]
    bs = x_sample.shape[0]

    mem2 = mem_prompt.reshape(bp * N_MEM, D_MODEL)
    pk, pv = [], []
    for i in range(2):
        kv = _norm_matmul(mem2, mem_norm[i], P['mem_w_kv'][i], tn=1024)
        pk.append(_head_norm(kv[:, :MEM_WIDTH], mem_k_norm[i]).reshape(bp, N_MEM, MEM_WIDTH))
        pv.append(kv[:, MEM_WIDTH:].reshape(bp, N_MEM, MEM_WIDTH))
    p_mem_k = jnp.stack(pk)
    p_mem_v = jnp.stack(pv)

    conv0 = jnp.zeros((bp, CONV_W - 1, CONV_DIM), F32)
    ssm0 = jnp.zeros((bp, SSD_HEADS, SSD_HEADDIM, D_STATE), F32)
    y_p, p_conv, p_ssm, p_k, p_v, p_ki = _trunk(x_prompt, 0, p_mem_k, p_mem_v, conv0, ssm0, None, P, W,
                                                q_ssd=128, cw_dsa=512)
    past = cache_k.shape[2]
    kv_in = (cache_k[0].reshape(bs, past, ATT_KV), cache_v[0].reshape(bs, past, ATT_KV), cache_idx_k[0])
    y_s, s_conv, s_ssm, s_k, s_v, s_ki = _trunk(
        x_sample, past, cache_mem_k.reshape(2, bs, N_MEM, MEM_WIDTH), cache_mem_v.reshape(2, bs, N_MEM, MEM_WIDTH),
        cache_conv[0], state_ssm[0], kv_in, P, W, q_ssd=x_sample.shape[1], cw_dsa=384)

    shp = (bp, N_MEM, MEM_HEADS, MEM_HD)
    return (y_p, y_s, p_conv[None], p_ssm[None], p_k[None], p_v[None], p_ki[None],
            p_mem_k.reshape((2,) + shp), p_mem_v.reshape((2,) + shp),
            s_conv[None], s_ssm[None], s_k[None], s_v[None], s_ki[None])
```

```python
import functools
import math

import jax
import jax.numpy as jnp
from jax import lax
from jax.experimental import pallas as pl
from jax.experimental.pallas import tpu as pltpu

F32 = jnp.float32
BF16 = jnp.bfloat16
I32 = jnp.int32

D_MODEL = 1024
CHUNK = 64
N_MEM = 256
EPS = 1e-6
D_INNER = 2048
SSD_HEADDIM = 64
SSD_HEADS = 32
SSD_GROUPS = 4
SSD_HPG = 8
D_STATE = 128
CONV_W = 4
GN = SSD_GROUPS * D_STATE
CONV_DIM = D_INNER + 2 * GN
ATT_HEADS = 16
ATT_HD = 64
KV_HEADS = 4
ATT_GROUPS = 4
ATT_Q = 1024
ATT_KV = 256
IDX_HEADS = 8
IDX_HD = 64
IDX_Q = 512
IDX_SCALE = (IDX_HEADS * IDX_HD) ** -0.5
TOPK_MAX = 256
ROPE_THETA = 10000.0
MEM_HEADS = 4
MEM_HD = 256
MEM_WIDTH = 1024
D_FF = 3584
N_EXPERTS = 8

LANES = 128
INT_MIN = -(2 ** 31)
NEG_BIG = -1e30
VMEM_LIMIT = 56 * 1024 * 1024


def _cparams(sem):
    return pltpu.CompilerParams(dimension_semantics=sem, vmem_limit_bytes=VMEM_LIMIT)


def _dot(a, b):
    return jnp.dot(a, b, preferred_element_type=F32)


def _split3(v):
    hi = v.astype(BF16)
    r1 = v - hi.astype(F32)
    mid = r1.astype(BF16)
    lo = (r1 - mid.astype(F32)).astype(BF16)
    return hi, mid, lo


def _silu(x):
    return x * (0.5 * jnp.tanh(0.5 * x) + 0.5)


def _norm_matmul_kernel(x_ref, g_ref, w_ref, o_ref, h_ref):
    @pl.when(pl.program_id(1) == 0)
    def _():
        x = x_ref[...]
        ms = jnp.mean(x * x, axis=-1, keepdims=True)
        h_ref[...] = (x * lax.rsqrt(ms + EPS) * g_ref[...]).astype(BF16)

    o_ref[...] = _dot(h_ref[...], w_ref[...]).astype(o_ref.dtype)


def _norm_matmul(x, g, w, tn, out_dtype=F32):
    n, k = x.shape
    m = w.shape[1]
    tm = min(1024, n)
    return pl.pallas_call(
        _norm_matmul_kernel,
        grid=(n // tm, m // tn),
        in_specs=[
            pl.BlockSpec((tm, k), lambda i, j: (i, 0)),
            pl.BlockSpec((1, k), lambda i, j: (0, 0)),
            pl.BlockSpec((k, tn), lambda i, j: (0, j)),
        ],
        out_specs=pl.BlockSpec((tm, tn), lambda i, j: (i, j)),
        out_shape=jax.ShapeDtypeStruct((n, m), out_dtype),
        scratch_shapes=[pltpu.VMEM((tm, k), BF16)],
        compiler_params=_cparams(("parallel", "arbitrary")),
        name="norm_matmul",
    )(x, g.reshape(1, k), w)


def _head_norm_kernel(x_ref, g_ref, o_ref):
    for h in range(MEM_HEADS):
        x = x_ref[:, h * MEM_HD:(h + 1) * MEM_HD]
        ms = jnp.mean(x * x, axis=-1, keepdims=True)
        o_ref[:, h * MEM_HD:(h + 1) * MEM_HD] = x * lax.rsqrt(ms + EPS) * g_ref[...]


def _head_norm(x, g):
    n = x.shape[0]
    tm = min(512, n)
    return pl.pallas_call(
        _head_norm_kernel,
        grid=(n // tm,),
        in_specs=[pl.BlockSpec((tm, MEM_WIDTH), lambda i: (i, 0)),
                  pl.BlockSpec((1, MEM_HD), lambda i: (0, 0))],
        out_specs=pl.BlockSpec((tm, MEM_WIDTH), lambda i: (i, 0)),
        out_shape=jax.ShapeDtypeStruct((n, MEM_WIDTH), F32),
        compiler_params=_cparams(("parallel",)),
        name="head_norm",
    )(x, g.reshape(1, MEM_HD))


def _mem_attn_kernel(q_ref, k_ref, v_ref, g_ref, o_ref):
    for h in range(MEM_HEADS):
        sl = slice(h * MEM_HD, (h + 1) * MEM_HD)
        q = q_ref[0, :, sl].astype(F32)
        ms = jnp.mean(q * q, axis=-1, keepdims=True)
        qn = (q * lax.rsqrt(ms + EPS) * g_ref[...]).astype(BF16)
        k = k_ref[0, :, sl].astype(BF16)
        s = lax.dot_general(qn, k, (((1,), (1,)), ((), ())), preferred_element_type=F32)
        s = s * (MEM_HD ** -0.5)
        m = jnp.max(s, axis=-1, keepdims=True)
        p = jnp.exp(s - m)
        p = p / jnp.sum(p, axis=-1, keepdims=True)
        o = _dot(p.astype(BF16), v_ref[0, :, sl].astype(BF16))
        o_ref[0, :, sl] = o.astype(o_ref.dtype)


def _mem_attn(u, col_block, mk, mv, g):
    b, t, _ = u.shape
    tq = min(512, t)
    return pl.pallas_call(
        _mem_attn_kernel,
        grid=(b, t // tq),
        in_specs=[
            pl.BlockSpec((1, tq, MEM_WIDTH), lambda i, j: (i, j, col_block)),
            pl.BlockSpec((1, N_MEM, MEM_WIDTH), lambda i, j: (i, 0, 0)),
            pl.BlockSpec((1, N_MEM, MEM_WIDTH), lambda i, j: (i, 0, 0)),
            pl.BlockSpec((1, MEM_HD), lambda i, j: (0, 0)),
        ],
        out_specs=pl.BlockSpec((1, tq, MEM_WIDTH), lambda i, j: (i, j, 0)),
        out_shape=jax.ShapeDtypeStruct((b, t, MEM_WIDTH), BF16),
        compiler_params=_cparams(("parallel", "parallel")),
        name="mem_attn",
    )(u, mk, mv, g.reshape(1, MEM_HD))


def _proj_res_kernel(a_ref, b_ref, wa_ref, wb_ref, x_ref, o_ref):
    o_ref[...] = x_ref[...] + _dot(a_ref[...], wa_ref[...]) + _dot(b_ref[...], wb_ref[...])


def _proj_res(a, b, wa, wb, x):
    n = x.shape[0]
    tm = min(512, n)
    ka, kb = a.shape[1], b.shape[1]
    return pl.pallas_call(
        _proj_res_kernel,
        grid=(n // tm,),
        in_specs=[
            pl.BlockSpec((tm, ka), lambda i: (i, 0)),
            pl.BlockSpec((tm, kb), lambda i: (i, 0)),
            pl.BlockSpec((ka, D_MODEL), lambda i: (0, 0)),
            pl.BlockSpec((kb, D_MODEL), lambda i: (0, 0)),
            pl.BlockSpec((tm, D_MODEL), lambda i: (i, 0)),
        ],
        out_specs=pl.BlockSpec((tm, D_MODEL), lambda i: (i, 0)),
        out_shape=jax.ShapeDtypeStruct((n, D_MODEL), F32),
        compiler_params=_cparams(("parallel",)),
        name="proj_res",
    )(a, b, wa, wb, x)


def _ffn_kernel(x_ref, g_ref, wg_ref, wu_ref, wd_ref, o_ref, h_ref, acc_ref):
    f = pl.program_id(1)

    @pl.when(f == 0)
    def _():
        x = x_ref[...]
        ms = jnp.mean(x * x, axis=-1, keepdims=True)
        h_ref[...] = (x * lax.rsqrt(ms + EPS) * g_ref[...]).astype(BF16)
        acc_ref[...] = x

    h = h_ref[...]
    a = _silu(_dot(h, wg_ref[...])) * _dot(h, wu_ref[...])
    acc_ref[...] += _dot(a.astype(BF16), wd_ref[...])

    @pl.when(f == pl.num_programs(1) - 1)
    def _():
        o_ref[...] = acc_ref[...]


def _ffn(x, g, wg, wu, wd, tf=512):
    n = x.shape[0]
    tm = min(1024, n)
    return pl.pallas_call(
        _ffn_kernel,
        grid=(n // tm, D_FF // tf),
        in_specs=[
            pl.BlockSpec((tm, D_MODEL), lambda i, f: (i, 0)),
            pl.BlockSpec((1, D_MODEL), lambda i, f: (0, 0)),
            pl.BlockSpec((D_MODEL, tf), lambda i, f: (0, f)),
            pl.BlockSpec((D_MODEL, tf), lambda i, f: (0, f)),
            pl.BlockSpec((tf, D_MODEL), lambda i, f: (f, 0)),
        ],
        out_specs=pl.BlockSpec((tm, D_MODEL), lambda i, f: (i, 0)),
        out_shape=jax.ShapeDtypeStruct((n, D_MODEL), F32),
        scratch_shapes=[pltpu.VMEM((tm, D_MODEL), BF16), pltpu.VMEM((tm, D_MODEL), F32)],
        compiler_params=_cparams(("parallel", "arbitrary")),
        name="ffn",
    )(x, g.reshape(1, D_MODEL), wg, wu, wd)


MOE_ROWS = 256


def _split2(v):
    hi = v.astype(BF16)
    return hi, (v - hi.astype(F32)).astype(BF16)


def _moe_route_kernel(x_ref, g_ref, rh_ref, rl_ref, h_ref, gate_ref):
    lane = lax.broadcasted_iota(I32, (1, LANES), 1)
    x = x_ref[...]
    ms = jnp.mean(x * x, axis=-1, keepdims=True)
    hf = x * lax.rsqrt(ms + EPS) * g_ref[...]
    hb, hl = _split2(hf)
    h_ref[...] = hb
    logits = _dot(hb, rh_ref[...]) + _dot(hl, rh_ref[...]) + _dot(hb, rl_ref[...])
    valid = lane < N_EXPERTS
    logits = jnp.where(valid, logits, NEG_BIG)
    m = jnp.max(logits, axis=-1, keepdims=True)
    p = jnp.exp(logits - m)
    p = p / jnp.sum(p, axis=-1, keepdims=True)
    p = jnp.where(valid, p, -1.0)
    v1 = jnp.max(p, axis=-1, keepdims=True)
    i1 = jnp.min(jnp.where(p == v1, lane, LANES), axis=-1, keepdims=True)
    p2 = jnp.where(lane == i1, -1.0, p)
    v2 = jnp.max(p2, axis=-1, keepdims=True)
    i2 = jnp.min(jnp.where(p2 == v2, lane, LANES), axis=-1, keepdims=True)
    den = v1 + v2
    gate_ref[...] = jnp.where(lane == i1, v1 / den, jnp.where(lane == i2, v2 / den, 0.0))


def _moe_route(x, g, r_hi, r_lo):
    n = x.shape[0]
    tm = min(512, n)
    return pl.pallas_call(
        _moe_route_kernel,
        grid=(n // tm,),
        in_specs=[pl.BlockSpec((tm, D_MODEL), lambda i: (i, 0)),
                  pl.BlockSpec((1, D_MODEL), lambda i: (0, 0)),
                  pl.BlockSpec((D_MODEL, LANES), lambda i: (0, 0)),
                  pl.BlockSpec((D_MODEL, LANES), lambda i: (0, 0))],
        out_specs=[pl.BlockSpec((tm, D_MODEL), lambda i: (i, 0)),
                   pl.BlockSpec((tm, LANES), lambda i: (i, 0))],
        out_shape=[jax.ShapeDtypeStruct((n, D_MODEL), BF16), jax.ShapeDtypeStruct((n, LANES), F32)],
        compiler_params=_cparams(("parallel",)),
        name="moe_route",
    )(x, g.reshape(1, D_MODEL), r_hi, r_lo)


def _moe_ffn_kernel(h_ref, gate_ref, x_ref, wg_ref, wu_ref, wd_ref, o_ref,
                    rank_ref, rank_t_ref, sel_t_ref, gate_t_ref, xe_ref, acc_ref, nb_ref, *, tm):
    e = pl.program_id(1)
    f = pl.program_id(2)
    nf = pl.num_programs(2)
    big, small = MOE_ROWS, MOE_ROWS // 2
    lane = lax.broadcasted_iota(I32, (1, LANES), 1)

    @pl.when((e == 0) & (f == 0))
    def _():
        o_ref[...] = x_ref[...]
        gates = gate_ref[...]
        sel = jnp.where(gates > 0.0, 1.0, 0.0)
        r = lax.broadcasted_iota(I32, (tm, tm), 0)
        c = lax.broadcasted_iota(I32, (tm, tm), 1)
        below = jnp.where(c < r, 1.0, 0.0).astype(BF16)
        rank = _dot(below, sel.astype(BF16))
        rank_ref[...] = rank
        rank_t_ref[...] = jnp.transpose(rank)
        sel_t_ref[...] = jnp.transpose(sel)
        gate_t_ref[...] = jnp.transpose(gates)

    @pl.when(f == 0)
    def _():
        cnt = jnp.sum(jnp.where(lane == e, jnp.sum(jnp.where(gate_ref[...] > 0.0, 1.0, 0.0), axis=0, keepdims=True),
                                0.0)).astype(I32)
        q = lax.div(cnt, big)
        rem = cnt - q * big
        merge = (q >= 1) & (rem > 0) & (rem <= small)
        merged_kind = jnp.where(rem <= small // 4, 2, jnp.where(rem <= small // 2, 3, 4))
        kind = jnp.where(rem == 0, 0, jnp.where(merge, merged_kind, jnp.where(rem <= small, 1, 0)))
        nb_ref[0] = q - jnp.where(merge, 1, 0) + jnp.where(rem > small, 1, 0)
        nb_ref[1] = kind

    n_big = nb_ref[0]
    last_kind = nb_ref[1]
    r_last = pl.multiple_of(n_big * big, big)
    last_rows = (small, big + small // 4, big + small // 2, big + small)

    def pick(r0, rows):
        want = (lax.broadcasted_iota(I32, (rows, 1), 0) + r0).astype(F32)
        return (rank_t_ref[pl.ds(e, 1), :] == want) & (sel_t_ref[pl.ds(e, 1), :] > 0.0)

    def for_blocks(fn):
        def body(rb, carry):
            fn(pl.multiple_of(rb * big, big), big)
            return carry

        lax.fori_loop(0, n_big, body, 0)
        for kind, rows in enumerate(last_rows, start=1):
            @pl.when(last_kind == kind)
            def _(rows=rows):
                fn(r_last, rows)

    def gather(r0, rows):
        p = jnp.where(pick(r0, rows), 1.0, 0.0).astype(BF16)
        xe_ref[pl.ds(r0, rows), :] = _dot(p, h_ref[...]).astype(BF16)

    def ffn(r0, rows):
        xe = xe_ref[pl.ds(r0, rows), :]
        a = _silu(_dot(xe, wg_ref[0])) * _dot(xe, wu_ref[0])
        y = _dot(a.astype(BF16), wd_ref[0])

        @pl.when(f == 0)
        def _():
            acc_ref[pl.ds(r0, rows), :] = y

        @pl.when(f > 0)
        def _():
            acc_ref[pl.ds(r0, rows), :] += y

    @pl.when(f == 0)
    def _():
        for_blocks(gather)

    for_blocks(ffn)

    @pl.when(f == nf - 1)
    def _():
        rank_col = jnp.sum(jnp.where(lane == e, rank_ref[...], 0.0), axis=1, keepdims=True)
        sel_col = jnp.sum(jnp.where(lane == e, gate_ref[...], 0.0), axis=1, keepdims=True) > 0.0

        def scatter(r0, rows):
            g_rows = jnp.sum(jnp.where(pick(r0, rows), gate_t_ref[pl.ds(e, 1), :], 0.0), axis=1, keepdims=True)
            y = (acc_ref[pl.ds(r0, rows), :] * g_rows).astype(BF16)
            col = (lax.broadcasted_iota(I32, (1, rows), 1) + r0).astype(F32)
            put = jnp.where((rank_col == col) & sel_col, 1.0, 0.0).astype(BF16)
            o_ref[...] += _dot(put, y)

        for_blocks(scatter)


def _moe_ffn(h, gates, x, wg, wu, wd, *, tm=1024, tf=1792):
    n = x.shape[0]
    tm = min(tm, n)
    kern = functools.partial(_moe_ffn_kernel, tm=tm)
    return pl.pallas_call(
        kern,
        grid=(n // tm, N_EXPERTS, D_FF // tf),
        in_specs=[
            pl.BlockSpec((tm, D_MODEL), lambda i, e, f: (i, 0)),
            pl.BlockSpec((tm, LANES), lambda i, e, f: (i, 0)),
            pl.BlockSpec((tm, D_MODEL), lambda i, e, f: (i, 0)),
            pl.BlockSpec((1, D_MODEL, tf), lambda i, e, f: (e, 0, f)),
            pl.BlockSpec((1, D_MODEL, tf), lambda i, e, f: (e, 0, f)),
            pl.BlockSpec((1, tf, D_MODEL), lambda i, e, f: (e, f, 0)),
        ],
        out_specs=pl.BlockSpec((tm, D_MODEL), lambda i, e, f: (i, 0)),
        out_shape=jax.ShapeDtypeStruct((n, D_MODEL), F32),
        scratch_shapes=[
            pltpu.VMEM((tm, LANES), F32),
            pltpu.VMEM((LANES, tm), F32),
            pltpu.VMEM((LANES, tm), F32),
            pltpu.VMEM((LANES, tm), F32),
            pltpu.VMEM((tm, D_MODEL), BF16),
            pltpu.VMEM((tm, D_MODEL), F32),
            pltpu.SMEM((2,), I32),
        ],
        compiler_params=_cparams(("parallel", "arbitrary", "arbitrary")),
        name="moe_ffn",
    )(h, gates, x, wg, wu, wd)


def _moe(x, g, r_hi, r_lo, wg, wu, wd):
    h, gates = _moe_route(x, g, r_hi, r_lo)
    return _moe_ffn(h, gates, x, wg, wu, wd)


def _ssd_kernel(xbc_ref, z_ref, dt_ref, cs_ref, h0_ref, cw_ref, cb_ref, dtb_ref, alog_ref,
                dexp_ref, og_ref, e_ref, y_ref, hout_ref, xpad_ref, h_ref, *, q):
    c = pl.program_id(1)

    @pl.when(c == 0)
    def _():
        xpad_ref[0:8, :] = cs_ref[0]
        h_ref[...] = h0_ref[0]

    xbc = xbc_ref[0].astype(F32)
    xpad_ref[8:8 + q, :] = xbc
    cw = cw_ref[...]
    conv = cb_ref[...] + (xpad_ref[5:5 + q, :] * cw[0:1] + xpad_ref[6:6 + q, :] * cw[1:2]
                          + xpad_ref[7:7 + q, :] * cw[2:3] + xbc * cw[3:4])
    xpad_ref[0:8, :] = xpad_ref[q:q + 8, :]
    act = _silu(conv)

    lane = lax.broadcasted_iota(I32, (1, LANES), 1)
    xdt = dt_ref[0] + dtb_ref[...]
    sp = jnp.maximum(xdt, 0.0) + jnp.log1p(jnp.exp(-jnp.abs(xdt)))
    dt = jnp.where(lane < SSD_HEADS, sp, 0.0)
    a = dt * (-jnp.exp(alog_ref[...]))

    rows = lax.broadcasted_iota(I32, (q, q), 0)
    cols = lax.broadcasted_iota(I32, (q, q), 1)
    causal = rows >= cols
    tril = jnp.where(causal, 1.0, 0.0).astype(BF16)
    a3 = _split3(a)
    acum = _dot(tril, a3[0]) + _dot(tril, a3[1]) + _dot(tril, a3[2])
    acum_t = jnp.transpose(acum)

    e_mat = e_ref[...]
    ac3 = _split3(acum)
    acum_x = _dot(ac3[0], e_mat) + _dot(ac3[1], e_mat) + _dot(ac3[2], e_mat)
    dt3 = _split3(dt)
    dt_x = _dot(dt3[0], e_mat) + _dot(dt3[1], e_mat) + _dot(dt3[2], e_mat)
    last = acum_x[q - 1:q, :]
    eac_x = jnp.exp(acum_x)
    dte_x = jnp.exp(last - acum_x)
    blkdec = jnp.exp(last)

    xs = act[:, :D_INNER]
    x_dt = xs * dt_x
    xb = x_dt.astype(BF16)
    xdb = (x_dt * dte_x).astype(BF16)
    z = z_ref[0].astype(F32)
    lane_lo = lane < SSD_HEADDIM

    gw = SSD_HPG * SSD_HEADDIM
    for g in range(SSD_GROUPS):
        bg = act[:, D_INNER + g * D_STATE:D_INNER + (g + 1) * D_STATE]
        cg = act[:, D_INNER + GN + g * D_STATE:D_INNER + GN + (g + 1) * D_STATE]
        cgb = cg.astype(BF16)
        bgt = jnp.transpose(bg).astype(BF16)
        cb = _dot(cgb, bgt)
        gs = slice(g * gw, (g + 1) * gw)
        h_in = h_ref[g]
        y_off = _dot(cgb, h_in.astype(BF16)) * eac_x[:, gs]
        h_ref[g] = h_in * blkdec[:, gs] + _dot(bgt, xdb[:, gs])
        parts = []
        for pr in range(SSD_HPG // 2):
            h0 = g * SSD_HPG + 2 * pr
            xp = xb[:, (h0 // 2) * LANES:(h0 // 2 + 1) * LANES]
            ys = []
            for hh in (h0, h0 + 1):
                seg = acum[:, hh:hh + 1] - acum_t[hh:hh + 1, :]
                lm = jnp.exp(jnp.where(causal, seg, NEG_BIG))
                ys.append(_dot((cb * lm).astype(BF16), xp))
            parts.append(jnp.where(lane_lo, ys[0], ys[1]))
        y_g = jnp.concatenate(parts, axis=1) + y_off + dexp_ref[:, gs] * xs[:, gs]
        gt = y_g * _silu(z[:, gs])
        gn = gt * lax.rsqrt(jnp.mean(gt * gt, axis=-1, keepdims=True) + EPS) * og_ref[:, gs]
        y_ref[0, :, gs] = gn.astype(y_ref.dtype)

    @pl.when(c == pl.num_programs(1) - 1)
    def _():
        hout_ref[0] = h_ref[...]


def _ssd(u, u_small, conv_state8, h0_t, cw8, cb, dtb, alog, dexp, og, e_mat, q):
    b, t, _ = u.shape
    kern = functools.partial(_ssd_kernel, q=q)
    full2 = lambda i, c: (0, 0)
    return pl.pallas_call(
        kern,
        grid=(b, t // q),
        in_specs=[
            pl.BlockSpec((1, q, CONV_DIM), lambda i, c: (i, c, 0)),
            pl.BlockSpec((1, q, D_INNER), lambda i, c: (i, c, 2)),
            pl.BlockSpec((1, q, LANES), lambda i, c: (i, c, 0)),
            pl.BlockSpec((1, 8, CONV_DIM), lambda i, c: (i, 0, 0)),
            pl.BlockSpec((1, SSD_GROUPS, D_STATE, SSD_HPG * SSD_HEADDIM), lambda i, c: (i, 0, 0, 0)),
            pl.BlockSpec((8, CONV_DIM), full2),
            pl.BlockSpec((1, CONV_DIM), full2),
            pl.BlockSpec((1, LANES), full2),
            pl.BlockSpec((1, LANES), full2),
            pl.BlockSpec((1, D_INNER), full2),
            pl.BlockSpec((1, D_INNER), full2),
            pl.BlockSpec((LANES, D_INNER), full2),
        ],
        out_specs=[
            pl.BlockSpec((1, q, D_INNER), lambda i, c: (i, c, 0)),
            pl.BlockSpec((1, SSD_GROUPS, D_STATE, SSD_HPG * SSD_HEADDIM), lambda i, c: (i, 0, 0, 0)),
        ],
        out_shape=[
            jax.ShapeDtypeStruct((b, t, D_INNER), BF16),
            jax.ShapeDtypeStruct((b, SSD_GROUPS, D_STATE, SSD_HPG * SSD_HEADDIM), F32),
        ],
        scratch_shapes=[pltpu.VMEM((q + 8, CONV_DIM), F32),
                        pltpu.VMEM((SSD_GROUPS, D_STATE, SSD_HPG * SSD_HEADDIM), F32)],
        compiler_params=_cparams(("parallel", "arbitrary")),
        name="ssd",
    )(u, u, u_small, conv_state8, h0_t, cw8, cb, dtb, alog, dexp, og, e_mat)


V_ROWS = 80
DSA_TQ = LANES
DSA_HPT = 4


def _dsa_prep_kernel(q_ref, k_ref, v_ref, qi_ref, sm_ref, cos_ref, sin_ref, qg_ref, kg_ref, seg_ref,
                     ko_ref, kio_ref, qt_ref, qit_ref, wit_ref, *key_refs, tr):
    cos = cos_ref[...]
    sin = sin_ref[...]
    seg = seg_ref[...]
    lane = lax.broadcasted_iota(I32, (1, LANES), 1)
    nqb = tr // DSA_TQ

    def head_norm(x):
        s3 = _split3(x * x)
        ss = _dot(s3[0], seg) + _dot(s3[1], seg) + _dot(s3[2], seg)
        return x * lax.rsqrt(ss * (1.0 / ATT_HD) + EPS)

    def rope(x):
        partner = jnp.where((lane & 32) == 0, pltpu.roll(x, LANES - 32, 1), pltpu.roll(x, 32, 1))
        return x * cos + partner * sin

    def put_t(dst_ref, row0, x):
        for qb in range(nqb):
            xt = jnp.transpose(x[qb * DSA_TQ:(qb + 1) * DSA_TQ, :])
            dst_ref[0, qb, row0:row0 + LANES, :] = xt.astype(dst_ref.dtype)

    for c in range(ATT_Q // LANES):
        sl = slice(c * LANES, (c + 1) * LANES)
        put_t(qt_ref, c * LANES, rope(head_norm(q_ref[0, :, sl].astype(F32)) * qg_ref[...]))
    for c in range(IDX_Q // LANES):
        sl = slice(c * LANES, (c + 1) * LANES)
        put_t(qit_ref, c * LANES, rope(qi_ref[0, :, sl].astype(F32)))
    sm = sm_ref[0]
    for qb in range(nqb):
        wit_ref[0, qb] = jnp.transpose(sm[qb * DSA_TQ:(qb + 1) * DSA_TQ, :])[IDX_HD:IDX_HD + IDX_HEADS, :]
    ki = rope(sm)
    kio_ref[0] = ki
    k_rot = []
    for c in range(ATT_KV // LANES):
        sl = slice(c * LANES, (c + 1) * LANES)
        k_rot.append(rope(head_norm(k_ref[0, :, sl].astype(F32)) * kg_ref[...]))
        ko_ref[0, :, sl] = k_rot[c]

    if key_refs:
        kb_ref, kib_ref, vt_ref = key_refs
        kib_ref[0] = ki[:, :IDX_HD].astype(BF16)
        ones_row = jnp.where(lax.broadcasted_iota(I32, (V_ROWS - ATT_HD, tr), 0) == 0, 1.0, 0.0).astype(BF16)
        for c in range(ATT_KV // LANES):
            sl = slice(c * LANES, (c + 1) * LANES)
            kb_ref[0, :, sl] = k_rot[c].astype(BF16)
            vt = jnp.transpose(v_ref[0, :, sl].astype(F32)).astype(BF16)
            for hh in range(2):
                vt_ref[0, 2 * c + hh, 0, 0:ATT_HD, :] = vt[hh * ATT_HD:(hh + 1) * ATT_HD, :]
                vt_ref[0, 2 * c + hh, 0, ATT_HD:V_ROWS, :] = ones_row


def _dsa_prep(u, u_small, cos, sin, qg, kg, *, tr, emit_keys):
    b, t, _ = u.shape
    nb = t // DSA_TQ
    nqb = tr // DSA_TQ
    full2 = lambda i, j: (0, 0)
    seg = (jnp.arange(LANES)[:, None] // ATT_HD == jnp.arange(LANES)[None, :] // ATT_HD).astype(BF16)
    out_specs = [
        pl.BlockSpec((1, tr, ATT_KV), lambda i, j: (i, j, 0)),
        pl.BlockSpec((1, tr, LANES), lambda i, j: (i, j, 0)),
        pl.BlockSpec((1, nqb, ATT_Q, DSA_TQ), lambda i, j: (i, j, 0, 0)),
        pl.BlockSpec((1, nqb, IDX_Q, DSA_TQ), lambda i, j: (i, j, 0, 0)),
        pl.BlockSpec((1, nqb, IDX_HEADS, DSA_TQ), lambda i, j: (i, j, 0, 0)),
    ]
    out_shape = [
        jax.ShapeDtypeStruct((b, t, ATT_KV), F32),
        jax.ShapeDtypeStruct((b, t, LANES), F32),
        jax.ShapeDtypeStruct((b, nb, ATT_Q, DSA_TQ), BF16),
        jax.ShapeDtypeStruct((b, nb, IDX_Q, DSA_TQ), BF16),
        jax.ShapeDtypeStruct((b, nb, IDX_HEADS, DSA_TQ), F32),
    ]
    if emit_keys:
        out_specs += [
            pl.BlockSpec((1, tr, ATT_KV), lambda i, j: (i, j, 0)),
            pl.BlockSpec((1, tr, IDX_HD), lambda i, j: (i, j, 0)),
            pl.BlockSpec((1, KV_HEADS, 1, V_ROWS, tr), lambda i, j: (i, 0, j, 0, 0)),
        ]
        out_shape += [
            jax.ShapeDtypeStruct((b, t, ATT_KV), BF16),
            jax.ShapeDtypeStruct((b, t, IDX_HD), BF16),
            jax.ShapeDtypeStruct((b, KV_HEADS, t // tr, V_ROWS, tr), BF16),
        ]
    return pl.pallas_call(
        functools.partial(_dsa_prep_kernel, tr=tr),
        grid=(b, t // tr),
        in_specs=[
            pl.BlockSpec((1, tr, ATT_Q), lambda i, j: (i, j, 0)),
            pl.BlockSpec((1, tr, ATT_KV), lambda i, j: (i, j, 4)),
            pl.BlockSpec((1, tr, ATT_KV), lambda i, j: (i, j, 5)),
            pl.BlockSpec((1, tr, IDX_Q), lambda i, j: (i, j, 3)),
            pl.BlockSpec((1, tr, LANES), lambda i, j: (i, j, 0)),
            pl.BlockSpec((tr, LANES), lambda i, j: (j, 0)),
            pl.BlockSpec((tr, LANES), lambda i, j: (j, 0)),
            pl.BlockSpec((1, LANES), full2),
            pl.BlockSpec((1, LANES), full2),
            pl.BlockSpec((LANES, LANES), full2),
        ],
        out_specs=out_specs,
        out_shape=out_shape,
        compiler_params=_cparams(("parallel", "parallel")),
        name="dsa_prep",
    )(u, u, u, u, u_small, cos, sin, qg, kg, seg)


def _fold8(x, op, rows=8):
    r, n = x.shape
    x = x.reshape(r // rows, rows, n)
    while x.shape[0] > 1:
        h = x.shape[0] // 2
        y = op(x[:h], x[h:2 * h])
        x = y if x.shape[0] % 2 == 0 else jnp.concatenate([y, x[2 * h:]], axis=0)
    return x[0]


def _dsa_kernel(qit_ref, wit_ref, ki_ref, qt_ref, k_ref, vt_ref, o_ref,
                rhs_i_ref, rhs_q_ref, key_ref, s_ref, acc_ref, mm_ref,
                *, tq, cw, l_keys, start, nsel):
    i = pl.program_id(1)
    cols4 = ATT_GROUPS * tq
    last_pos = start + i * tq + (tq - 1)
    max_limit = jnp.minimum((lax.shift_right_logical(last_pos, 6) + 1) * CHUNK, l_keys)
    n_c = lax.div(max_limit + (cw - 1), cw)

    pos = start + i * tq + lax.broadcasted_iota(I32, (1, tq), 1)
    limit = jnp.minimum((lax.shift_right_logical(pos, 6) + 1) * CHUNK, l_keys)
    sub_pos = lax.broadcasted_iota(I32, (cw, 1), 0)

    for h in range(IDX_HEADS):
        rhs_i_ref[:, h * tq:(h + 1) * tq] = qit_ref[0, 0, h * IDX_HD:(h + 1) * IDX_HD, :]

    @pl.when(i == 0)
    def _():
        rhs_q_ref[...] = jnp.zeros_like(rhs_q_ref)

    for j in range(KV_HEADS):
        for g in range(ATT_GROUPS):
            hq = ATT_GROUPS * j + g
            rhs_q_ref[j, j * ATT_HD:(j + 1) * ATT_HD, g * tq:(g + 1) * tq] = \
                qt_ref[0, 0, hq * ATT_HD:(hq + 1) * ATT_HD, :]

    w = [wit_ref[0, 0, h:h + 1, :] * IDX_SCALE for h in range(IDX_HEADS)]

    def score_chunk(c, carry):
        d = jnp.maximum(_dot(ki_ref[0, c], rhs_i_ref[...]), 0.0)
        score = d[:, 0:tq] * w[0]
        for h in range(1, IDX_HEADS):
            score = score + d[:, h * tq:(h + 1) * tq] * w[h]
        bits = pltpu.bitcast(score, I32)
        key = jnp.where(bits < 0, bits ^ 0x7FFFFFFF, bits)
        key = jnp.where(score == 0.0, 0, key)
        key_ref[c] = jnp.where(sub_pos + c * cw < limit, key, INT_MIN)
        return carry

    lax.fori_loop(0, n_c, score_chunk, 0)

    def count(pred):
        def body(c, acc):
            return acc + _fold8(jnp.where(pred(key_ref[c], c * cw), 1.0, 0.0), jnp.add)

        acc = lax.fori_loop(0, n_c, body, jnp.zeros((8, tq), F32))
        return jnp.sum(acc, axis=0, keepdims=True)

    def search(it, lo):
        cand = lo + lax.shift_left(jnp.int32(1), 31 - it)
        cnt = count(lambda k, base: k >= cand)
        return jnp.where(cnt >= nsel, cand, lo)

    thr = lax.fori_loop(0, 32, search, jnp.full((1, tq), INT_MIN, I32))

    c_gt = count(lambda k, base: k > thr)
    n_eq = count(lambda k, base: k == thr)
    need = nsel - c_gt
    excess = jnp.where((n_eq > need) & (thr > INT_MIN), 1.0, 0.0)
    mm_ref[...] = jnp.full((1, tq), 1 << 14, I32)

    @pl.when(jnp.max(excess) > 0.0)
    def _():
        def tie_search(it, m):
            cand = m + lax.shift_left(jnp.int32(1), 13 - it)
            f = count(lambda k, base: (k == thr) & (sub_pos + base < cand))
            return jnp.where(f < need, cand, m)

        mm_ref[...] = lax.fori_loop(0, 14, tie_search, jnp.zeros((1, tq), I32))

    mm = mm_ref[...]

    scale = (ATT_HD ** -0.5) * math.log2(math.e)
    m0 = jnp.full((8, cols4), -3e38, F32)
    for j0 in range(0, KV_HEADS, DSA_HPT):
        pair = tuple(range(j0, j0 + DSA_HPT))

        def qk_chunk(c, ms, pair=pair):
            k = key_ref[c]
            kpos = sub_pos + c * cw
            sel = ((k > thr) | ((k == thr) & (kpos <= mm))) & (kpos < limit)
            b1 = jnp.where(sel, 0.0, NEG_BIG)
            b4 = jnp.concatenate([b1] * ATT_GROUPS, axis=1)
            out = []
            for jj, j in enumerate(pair):
                s = _dot(k_ref[0, c], rhs_q_ref[j]) * scale + b4
                s_ref[jj, c] = s
                out.append(jnp.maximum(ms[jj], _fold8(s, jnp.maximum)))
            return tuple(out)

        ms = lax.fori_loop(0, n_c, qk_chunk, (m0,) * DSA_HPT)
        ms = [jnp.max(m, axis=0, keepdims=True) for m in ms]
        acc_ref[...] = jnp.zeros_like(acc_ref)

        def pv_chunk(c, carry, pair=pair, ms=ms):
            for jj, j in enumerate(pair):
                p = jnp.exp2(s_ref[jj, c] - ms[jj])
                acc_ref[jj] += _dot(vt_ref[0, j, c], p.astype(BF16))
            return carry

        lax.fori_loop(0, n_c, pv_chunk, 0)
        for jj, j in enumerate(pair):
            a = acc_ref[jj]
            o = a[0:ATT_HD] / a[ATT_HD:ATT_HD + 1]
            for g in range(0, ATT_GROUPS, 2):
                two = jnp.concatenate([o[:, g * tq:(g + 1) * tq], o[:, (g + 1) * tq:(g + 2) * tq]], axis=0)
                lo = (ATT_GROUPS * j + g) * ATT_HD
                o_ref[0, :, lo:lo + 2 * ATT_HD] = jnp.transpose(two).astype(o_ref.dtype)


def _dsa(qit, wit, ki, qt, k, vt, *, tq, cw, l_keys, start):
    b, nb = qit.shape[0], qit.shape[1]
    nc = k.shape[1]
    nsel = min(TOPK_MAX, l_keys // 4)
    kern = functools.partial(_dsa_kernel, tq=tq, cw=cw, l_keys=l_keys, start=start, nsel=float(nsel))
    cols4 = ATT_GROUPS * tq
    return pl.pallas_call(
        kern,
        grid=(b, nb),
        in_specs=[
            pl.BlockSpec((1, 1, IDX_Q, tq), lambda i, j: (i, j, 0, 0)),
            pl.BlockSpec((1, 1, IDX_HEADS, tq), lambda i, j: (i, j, 0, 0)),
            pl.BlockSpec((1, nc, cw, IDX_HD), lambda i, j: (i, 0, 0, 0)),
            pl.BlockSpec((1, 1, ATT_Q, tq), lambda i, j: (i, j, 0, 0)),
            pl.BlockSpec((1, nc, cw, ATT_KV), lambda i, j: (i, 0, 0, 0)),
            pl.BlockSpec((1, KV_HEADS, nc, V_ROWS, cw), lambda i, j: (i, 0, 0, 0, 0)),
        ],
        out_specs=pl.BlockSpec((1, tq, ATT_Q), lambda i, j: (i, j, 0)),
        out_shape=jax.ShapeDtypeStruct((b, nb * tq, ATT_Q), BF16),
        scratch_shapes=[
            pltpu.VMEM((IDX_HD, IDX_HEADS * tq), BF16),
            pltpu.VMEM((KV_HEADS, ATT_KV, cols4), BF16),
            pltpu.VMEM((nc, cw, tq), I32),
            pltpu.VMEM((DSA_HPT, nc, cw, cols4), F32),
            pltpu.VMEM((DSA_HPT, V_ROWS, cols4), F32),
            pltpu.VMEM((1, tq), I32),
        ],
        compiler_params=_cparams(("parallel", "arbitrary")),
        name="dsa",
    )(qit, wit, ki, qt, k, vt)


def _dsa_key_layouts(k_all, v_all, ki_all, cw):
    b, l_keys, _ = k_all.shape
    nc = -(-l_keys // cw)
    padk = lambda a: jnp.pad(a, ((0, 0), (0, nc * cw - l_keys), (0, 0))).astype(BF16)
    vt = padk(v_all).reshape(b, nc, cw, KV_HEADS, ATT_HD).transpose(0, 3, 1, 4, 2)
    ones = jnp.ones((b, KV_HEADS, nc, 1, cw), BF16)
    zeros = jnp.zeros((b, KV_HEADS, nc, V_ROWS - ATT_HD - 1, cw), BF16)
    return (padk(ki_all).reshape(b, nc, cw, IDX_HD), padk(k_all).reshape(b, nc, cw, ATT_KV),
            jnp.concatenate([vt, ones, zeros], axis=3))


def _prep_weights(ssd_w_in, att_w_in, ssd_w_out, att_w_out, mem_w_kv, dense_w_gate, dense_w_up,
                  dense_w_down, moe_router, moe_w_gate, moe_w_up, moe_w_down):
    bf = lambda w: w.astype(BF16)
    w = ssd_w_in[0]
    o_xbc, o_dt, o_mq = D_INNER, D_INNER + CONV_DIM, D_INNER + CONV_DIM + SSD_HEADS
    ssd_main = bf(jnp.concatenate([w[:, o_xbc:o_dt], w[:, o_mq:], w[:, :D_INNER]], axis=1))
    ssd_small = bf(jnp.pad(w[:, o_dt:o_mq], ((0, 0), (0, LANES - SSD_HEADS))))
    w = att_w_in[0]
    o_wi = ATT_Q + 2 * ATT_KV + IDX_Q
    o_ki = o_wi + IDX_HEADS
    o_mq = o_ki + IDX_HD
    att_main = bf(jnp.concatenate([w[:, :o_wi], w[:, o_mq:]], axis=1))
    att_small = bf(jnp.pad(jnp.concatenate([w[:, o_ki:o_mq], w[:, o_wi:o_ki]], axis=1),
                           ((0, 0), (0, LANES - IDX_HD - IDX_HEADS))))
    r = jnp.pad(moe_router[0], ((0, 0), (0, LANES - N_EXPERTS)))
    r_hi = bf(r)
    r_lo = bf(r - r_hi.astype(F32))
    return dict(
        ssd_main=ssd_main, ssd_small=ssd_small, att_main=att_main, att_small=att_small,
        ssd_out_a=bf(ssd_w_out[0, :D_INNER]), ssd_out_b=bf(ssd_w_out[0, D_INNER:]),
        att_out_a=bf(att_w_out[0, :ATT_Q]), att_out_b=bf(att_w_out[0, ATT_Q:]),
        mem_w_kv=bf(mem_w_kv), dense_g=bf(dense_w_gate[0]), dense_u=bf(dense_w_up[0]),
        dense_d=bf(dense_w_down[0]), r_hi=r_hi, r_lo=r_lo,
        moe_g=bf(moe_w_gate[0]), moe_u=bf(moe_w_up[0]), moe_d=bf(moe_w_down[0]))


def _rope_tables(pos):
    half = ATT_HD // 2
    inv = ROPE_THETA ** (-jnp.arange(half, dtype=F32) / half)
    ang = pos.astype(F32)[:, None] * inv[None, :]
    cos = jnp.cos(ang)
    sin = jnp.sin(ang)
    cos_t = jnp.concatenate([cos, cos, cos, cos], axis=1)
    sin_t = jnp.concatenate([-sin, sin, -sin, sin], axis=1)
    return cos_t, sin_t


def _trunk(x, start, mem_k, mem_v, conv_in, ssm_in, kv_in, P, W, q_ssd, cw_dsa):
    b, t, _ = x.shape
    n = b * t
    x2 = x.reshape(n, D_MODEL)

    u = _norm_matmul(x2, W['ssd_norm'][0], P['ssd_main'], tn=1024, out_dtype=BF16).reshape(b, t, -1)
    u_small = _norm_matmul(x2, W['ssd_norm'][0], P['ssd_small'], tn=LANES).reshape(b, t, LANES)
    conv8 = jnp.pad(conv_in, ((0, 0), (8 - (CONV_W - 1), 0), (0, 0)))
    h0_t = ssm_in.reshape(b, SSD_GROUPS, SSD_HPG, SSD_HEADDIM, D_STATE).transpose(0, 1, 4, 2, 3)
    h0_t = h0_t.reshape(b, SSD_GROUPS, D_STATE, SSD_HPG * SSD_HEADDIM)
    cw8 = jnp.pad(W['ssd_conv_w'][0], ((0, 8 - CONV_W), (0, 0)))
    pad_h = lambda v: jnp.pad(v.astype(F32), (0, LANES - SSD_HEADS)).reshape(1, LANES)
    e_mat = (jnp.arange(LANES)[:, None] == (jnp.arange(D_INNER)[None, :] // SSD_HEADDIM)).astype(BF16)
    y_mix, h_t = _ssd(u, u_small, conv8, h0_t, cw8, W['ssd_conv_b'][0].reshape(1, CONV_DIM),
                      pad_h(W['ssd_dt_bias'][0]), pad_h(W['ssd_A_log'][0]),
                      jnp.repeat(W['ssd_D'][0].astype(F32), SSD_HEADDIM).reshape(1, D_INNER),
                      W['ssd_out_norm'][0].reshape(1, D_INNER), e_mat, q_ssd)
    new_conv = u[:, t - (CONV_W - 1):, :CONV_DIM].astype(F32)
    new_ssm = h_t.reshape(b, SSD_GROUPS, D_STATE, SSD_HPG, SSD_HEADDIM).transpose(0, 1, 3, 4, 2)
    new_ssm = new_ssm.reshape(b, SSD_HEADS, SSD_HEADDIM, D_STATE)
    y_mem = _mem_attn(u, 3, mem_k[0], mem_v[0], W['mem_q_norm'][0])
    x2 = _proj_res(y_mix.reshape(n, D_INNER), y_mem.reshape(n, MEM_WIDTH), P['ssd_out_a'], P['ssd_out_b'], x2)
    x2 = _ffn(x2, W['ffn_norm'][0], P['dense_g'], P['dense_u'], P['dense_d'])

    u = _norm_matmul(x2, W['att_norm'][0], P['att_main'], tn=1024, out_dtype=BF16).reshape(b, t, -1)
    u_small = _norm_matmul(x2, W['att_norm'][0], P['att_small'], tn=LANES).reshape(b, t, LANES)
    pos = start + jnp.arange(t)
    cos_t, sin_t = _rope_tables(pos)
    tile2 = lambda v: jnp.tile(v.astype(F32), 2).reshape(1, LANES)
    v_new = u[:, :, ATT_Q + ATT_KV:ATT_Q + 2 * ATT_KV].astype(F32)
    qg, kg = tile2(W['att_q_norm'][0]), tile2(W['att_k_norm'][0])
    if kv_in is None:
        k_rot, ki_rot, qt, qit, wit, k_b, ki_b, vt = _dsa_prep(u, u_small, cos_t, sin_t, qg, kg,
                                                               tr=cw_dsa, emit_keys=True)
        nc = t // cw_dsa
        keys = (ki_b.reshape(b, nc, cw_dsa, IDX_HD), k_b.reshape(b, nc, cw_dsa, ATT_KV), vt)
        ki_new = ki_rot[:, :, :IDX_HD]
        l_keys = t
    else:
        padt = lambda a: jnp.pad(a, ((0, DSA_TQ - t),) + ((0, 0),) * (a.ndim - 1))
        padbt = lambda a: jnp.pad(a, ((0, 0), (0, DSA_TQ - t), (0, 0)))
        k_rot, ki_rot, qt, qit, wit = _dsa_prep(padbt(u), padbt(u_small), padt(cos_t), padt(sin_t), qg, kg,
                                                tr=DSA_TQ, emit_keys=False)
        k_rot, ki_rot = k_rot[:, :t], ki_rot[:, :t]
        ki_new = ki_rot[:, :, :IDX_HD]
        keys = _dsa_key_layouts(jnp.concatenate([kv_in[0], k_rot], axis=1),
                                jnp.concatenate([kv_in[1], v_new], axis=1),
                                jnp.concatenate([kv_in[2], ki_new], axis=1), cw_dsa)
        l_keys = kv_in[0].shape[1] + t
    o_t = _dsa(qit, wit, keys[0], qt, keys[1], keys[2], tq=DSA_TQ, cw=cw_dsa, l_keys=l_keys, start=start)
    y_mix = o_t[:, :t].reshape(n, ATT_Q)
    y_mem = _mem_attn(u, 2, mem_k[1], mem_v[1], W['mem_q_norm'][1])
    x2 = _proj_res(y_mix, y_mem.reshape(n, MEM_WIDTH), P['att_out_a'], P['att_out_b'], x2)
    x2 = _moe(x2, W['ffn_norm'][1], P['r_hi'], P['r_lo'], P['moe_g'], P['moe_u'], P['moe_d'])

    return (x2.reshape(b, t, D_MODEL), new_conv, new_ssm, k_rot.reshape(b, t, KV_HEADS, ATT_HD),
            v_new.reshape(b, t, KV_HEADS, ATT_HD), ki_new)


def kernel(x_prompt, x_sample, mem_prompt, cache_conv, state_ssm, cache_k, cache_v, cache_idx_k, cache_mem_k, cache_mem_v, ssd_norm, ssd_w_in, ssd_conv_w, ssd_conv_b, ssd_dt_bias, ssd_A_log, ssd_D, ssd_out_norm, ssd_w_out, att_norm, att_w_in, att_q_norm, att_k_norm, att_w_out, mem_norm, mem_w_kv, mem_q_norm, mem_k_norm, ffn_norm, dense_w_gate, dense_w_up, dense_w_down, moe_router, moe_w_gate, moe_w_up, moe_w_down):
    W = dict(ssd_norm=ssd_norm, ssd_conv_w=ssd_conv_w, ssd_conv_b=ssd_conv_b, ssd_dt_bias=ssd_dt_bias,
             ssd_A_log=ssd_A_log, ssd_D=ssd_D, ssd_out_norm=ssd_out_norm, att_norm=att_norm,
             att_q_norm=att_q_norm, att_k_norm=att_k_norm, mem_q_norm=mem_q_norm, ffn_norm=ffn_norm)
    P = _prep_weights(ssd_w_in, att_w_in, ssd_w_out, att_w_out, mem_w_kv, dense_w_gate, dense_w_up,
                      dense_w_down, moe_router, moe_w_gate, moe_w_up, moe_w_down)
    bp, sp = x_prompt.shape[0], x_prompt.shape[1]
    bs = x_sample.shape[0]

    mem2 = mem_prompt.reshape(bp * N_MEM, D_MODEL)
    pk, pv = [], []
    for i in range(2):
        kv = _norm_matmul(mem2, mem_norm[i], P['mem_w_kv'][i], tn=1024)
        pk.append(_head_norm(kv[:, :MEM_WIDTH], mem_k_norm[i]).reshape(bp, N_MEM, MEM_WIDTH))
        pv.append(kv[:, MEM_WIDTH:].reshape(bp, N_MEM, MEM_WIDTH))
    p_mem_k = jnp.stack(pk)
    p_mem_v = jnp.stack(pv)

    conv0 = jnp.zeros((bp, CONV_W - 1, CONV_DIM), F32)
    ssm0 = jnp.zeros((bp, SSD_HEADS, SSD_HEADDIM, D_STATE), F32)
    y_p, p_conv, p_ssm, p_k, p_v, p_ki = _trunk(x_prompt, 0, p_mem_k, p_mem_v, conv0, ssm0, None, P, W,
                                                q_ssd=128, cw_dsa=512)
    past = cache_k.shape[2]
    kv_in = (cache_k[0].reshape(bs, past, ATT_KV), cache_v[0].reshape(bs, past, ATT_KV), cache_idx_k[0])
    y_s, s_conv, s_ssm, s_k, s_v, s_ki = _trunk(
        x_sample, past, cache_mem_k.reshape(2, bs, N_MEM, MEM_WIDTH), cache_mem_v.reshape(2, bs, N_MEM, MEM_WIDTH),
        cache_conv[0], state_ssm[0], kv_in, P, W, q_ssd=x_sample.shape[1], cw_dsa=384)

    shp = (bp, N_MEM, MEM_HEADS, MEM_HD)
    return (y_p, y_s, p_conv[None], p_ssm[None], p_k[None], p_v[None], p_ki[None],
            p_mem_k.reshape((2,) + shp), p_mem_v.reshape((2,) + shp),
            s_conv[None], s_ssm[None], s_k[None], s_v[None], s_ki[None])
```

```python
import functools
import math

import jax
import jax.numpy as jnp
from jax import lax
from jax.experimental import pallas as pl
from jax.experimental.pallas import tpu as pltpu

F32 = jnp.float32
BF16 = jnp.bfloat16
I32 = jnp.int32

D_MODEL = 1024
CHUNK = 64
N_MEM = 256
EPS = 1e-6
D_INNER = 2048
SSD_HEADDIM = 64
SSD_HEADS = 32
SSD_GROUPS = 4
SSD_HPG = 8
D_STATE = 128
CONV_W = 4
GN = SSD_GROUPS * D_STATE
CONV_DIM = D_INNER + 2 * GN
ATT_HEADS = 16
ATT_HD = 64
KV_HEADS = 4
ATT_GROUPS = 4
ATT_Q = 1024
ATT_KV = 256
IDX_HEADS = 8
IDX_HD = 64
IDX_Q = 512
IDX_SCALE = (IDX_HEADS * IDX_HD) ** -0.5
TOPK_MAX = 256
ROPE_THETA = 10000.0
MEM_HEADS = 4
MEM_HD = 256
MEM_WIDTH = 1024
D_FF = 3584
N_EXPERTS = 8

LANES = 128
INT_MIN = -(2 ** 31)
NEG_BIG = -1e30
VMEM_LIMIT = 56 * 1024 * 1024


def _cparams(sem):
    return pltpu.CompilerParams(dimension_semantics=sem, vmem_limit_bytes=VMEM_LIMIT)


def _dot(a, b):
    return jnp.dot(a, b, preferred_element_type=F32)


def _split3(v):
    hi = v.astype(BF16)
    r1 = v - hi.astype(F32)
    mid = r1.astype(BF16)
    lo = (r1 - mid.astype(F32)).astype(BF16)
    return hi, mid, lo


def _silu(x):
    return x * (0.5 * jnp.tanh(0.5 * x) + 0.5)


def _norm_matmul_kernel(x_ref, g_ref, w_ref, o_ref, h_ref):
    @pl.when(pl.program_id(1) == 0)
    def _():
        x = x_ref[...]
        ms = jnp.mean(x * x, axis=-1, keepdims=True)
        h_ref[...] = (x * lax.rsqrt(ms + EPS) * g_ref[...]).astype(BF16)

    o_ref[...] = _dot(h_ref[...], w_ref[...]).astype(o_ref.dtype)


def _norm_matmul(x, g, w, tn, out_dtype=F32):
    n, k = x.shape
    m = w.shape[1]
    tm = min(1024, n)
    return pl.pallas_call(
        _norm_matmul_kernel,
        grid=(n // tm, m // tn),
        in_specs=[
            pl.BlockSpec((tm, k), lambda i, j: (i, 0)),
            pl.BlockSpec((1, k), lambda i, j: (0, 0)),
            pl.BlockSpec((k, tn), lambda i, j: (0, j)),
        ],
        out_specs=pl.BlockSpec((tm, tn), lambda i, j: (i, j)),
        out_shape=jax.ShapeDtypeStruct((n, m), out_dtype),
        scratch_shapes=[pltpu.VMEM((tm, k), BF16)],
        compiler_params=_cparams(("parallel", "arbitrary")),
        name="norm_matmul",
    )(x, g.reshape(1, k), w)


def _head_norm_kernel(x_ref, g_ref, o_ref):
    for h in range(MEM_HEADS):
        x = x_ref[:, h * MEM_HD:(h + 1) * MEM_HD]
        ms = jnp.mean(x * x, axis=-1, keepdims=True)
        o_ref[:, h * MEM_HD:(h + 1) * MEM_HD] = x * lax.rsqrt(ms + EPS) * g_ref[...]


def _head_norm(x, g):
    n = x.shape[0]
    tm = min(512, n)
    return pl.pallas_call(
        _head_norm_kernel,
        grid=(n // tm,),
        in_specs=[pl.BlockSpec((tm, MEM_WIDTH), lambda i: (i, 0)),
                  pl.BlockSpec((1, MEM_HD), lambda i: (0, 0))],
        out_specs=pl.BlockSpec((tm, MEM_WIDTH), lambda i: (i, 0)),
        out_shape=jax.ShapeDtypeStruct((n, MEM_WIDTH), F32),
        compiler_params=_cparams(("parallel",)),
        name="head_norm",
    )(x, g.reshape(1, MEM_HD))


def _mem_attn_kernel(q_ref, k_ref, v_ref, g_ref, o_ref):
    for h in range(MEM_HEADS):
        sl = slice(h * MEM_HD, (h + 1) * MEM_HD)
        q = q_ref[0, :, sl].astype(F32)
        ms = jnp.mean(q * q, axis=-1, keepdims=True)
        qn = (q * lax.rsqrt(ms + EPS) * g_ref[...]).astype(BF16)
        k = k_ref[0, :, sl].astype(BF16)
        s = lax.dot_general(qn, k, (((1,), (1,)), ((), ())), preferred_element_type=F32)
        s = s * (MEM_HD ** -0.5)
        m = jnp.max(s, axis=-1, keepdims=True)
        p = jnp.exp(s - m)
        p = p / jnp.sum(p, axis=-1, keepdims=True)
        o = _dot(p.astype(BF16), v_ref[0, :, sl].astype(BF16))
        o_ref[0, :, sl] = o.astype(o_ref.dtype)


def _mem_attn(u, col_block, mk, mv, g):
    b, t, _ = u.shape
    tq = min(512, t)
    return pl.pallas_call(
        _mem_attn_kernel,
        grid=(b, t // tq),
        in_specs=[
            pl.BlockSpec((1, tq, MEM_WIDTH), lambda i, j: (i, j, col_block)),
            pl.BlockSpec((1, N_MEM, MEM_WIDTH), lambda i, j: (i, 0, 0)),
            pl.BlockSpec((1, N_MEM, MEM_WIDTH), lambda i, j: (i, 0, 0)),
            pl.BlockSpec((1, MEM_HD), lambda i, j: (0, 0)),
        ],
        out_specs=pl.BlockSpec((1, tq, MEM_WIDTH), lambda i, j: (i, j, 0)),
        out_shape=jax.ShapeDtypeStruct((b, t, MEM_WIDTH), BF16),
        compiler_params=_cparams(("parallel", "parallel")),
        name="mem_attn",
    )(u, mk, mv, g.reshape(1, MEM_HD))


def _proj_res_kernel(a_ref, b_ref, wa_ref, wb_ref, x_ref, o_ref):
    o_ref[...] = x_ref[...] + _dot(a_ref[...], wa_ref[...]) + _dot(b_ref[...], wb_ref[...])


def _proj_res(a, b, wa, wb, x):
    n = x.shape[0]
    tm = min(512, n)
    ka, kb = a.shape[1], b.shape[1]
    return pl.pallas_call(
        _proj_res_kernel,
        grid=(n // tm,),
        in_specs=[
            pl.BlockSpec((tm, ka), lambda i: (i, 0)),
            pl.BlockSpec((tm, kb), lambda i: (i, 0)),
            pl.BlockSpec((ka, D_MODEL), lambda i: (0, 0)),
            pl.BlockSpec((kb, D_MODEL), lambda i: (0, 0)),
            pl.BlockSpec((tm, D_MODEL), lambda i: (i, 0)),
        ],
        out_specs=pl.BlockSpec((tm, D_MODEL), lambda i: (i, 0)),
        out_shape=jax.ShapeDtypeStruct((n, D_MODEL), F32),
        compiler_params=_cparams(("parallel",)),
        name="proj_res",
    )(a, b, wa, wb, x)


def _ffn_kernel(x_ref, g_ref, wg_ref, wu_ref, wd_ref, o_ref, h_ref, acc_ref):
    f = pl.program_id(1)

    @pl.when(f == 0)
    def _():
        x = x_ref[...]
        ms = jnp.mean(x * x, axis=-1, keepdims=True)
        h_ref[...] = (x * lax.rsqrt(ms + EPS) * g_ref[...]).astype(BF16)
        acc_ref[...] = x

    h = h_ref[...]
    a = _silu(_dot(h, wg_ref[...])) * _dot(h, wu_ref[...])
    acc_ref[...] += _dot(a.astype(BF16), wd_ref[...])

    @pl.when(f == pl.num_programs(1) - 1)
    def _():
        o_ref[...] = acc_ref[...]


def _ffn(x, g, wg, wu, wd, tf=512):
    n = x.shape[0]
    tm = min(1024, n)
    return pl.pallas_call(
        _ffn_kernel,
        grid=(n // tm, D_FF // tf),
        in_specs=[
            pl.BlockSpec((tm, D_MODEL), lambda i, f: (i, 0)),
            pl.BlockSpec((1, D_MODEL), lambda i, f: (0, 0)),
            pl.BlockSpec((D_MODEL, tf), lambda i, f: (0, f)),
            pl.BlockSpec((D_MODEL, tf), lambda i, f: (0, f)),
            pl.BlockSpec((tf, D_MODEL), lambda i, f: (f, 0)),
        ],
        out_specs=pl.BlockSpec((tm, D_MODEL), lambda i, f: (i, 0)),
        out_shape=jax.ShapeDtypeStruct((n, D_MODEL), F32),
        scratch_shapes=[pltpu.VMEM((tm, D_MODEL), BF16), pltpu.VMEM((tm, D_MODEL), F32)],
        compiler_params=_cparams(("parallel", "arbitrary")),
        name="ffn",
    )(x, g.reshape(1, D_MODEL), wg, wu, wd)


MOE_ROWS = 256


def _split2(v):
    hi = v.astype(BF16)
    return hi, (v - hi.astype(F32)).astype(BF16)


def _moe_route_kernel(x_ref, g_ref, rh_ref, rl_ref, h_ref, gate_ref):
    lane = lax.broadcasted_iota(I32, (1, LANES), 1)
    x = x_ref[...]
    ms = jnp.mean(x * x, axis=-1, keepdims=True)
    hf = x * lax.rsqrt(ms + EPS) * g_ref[...]
    hb, hl = _split2(hf)
    h_ref[...] = hb
    logits = _dot(hb, rh_ref[...]) + _dot(hl, rh_ref[...]) + _dot(hb, rl_ref[...])
    valid = lane < N_EXPERTS
    logits = jnp.where(valid, logits, NEG_BIG)
    m = jnp.max(logits, axis=-1, keepdims=True)
    p = jnp.exp(logits - m)
    p = p / jnp.sum(p, axis=-1, keepdims=True)
    p = jnp.where(valid, p, -1.0)
    v1 = jnp.max(p, axis=-1, keepdims=True)
    i1 = jnp.min(jnp.where(p == v1, lane, LANES), axis=-1, keepdims=True)
    p2 = jnp.where(lane == i1, -1.0, p)
    v2 = jnp.max(p2, axis=-1, keepdims=True)
    i2 = jnp.min(jnp.where(p2 == v2, lane, LANES), axis=-1, keepdims=True)
    den = v1 + v2
    gate_ref[...] = jnp.where(lane == i1, v1 / den, jnp.where(lane == i2, v2 / den, 0.0))


def _moe_route(x, g, r_hi, r_lo):
    n = x.shape[0]
    tm = min(512, n)
    return pl.pallas_call(
        _moe_route_kernel,
        grid=(n // tm,),
        in_specs=[pl.BlockSpec((tm, D_MODEL), lambda i: (i, 0)),
                  pl.BlockSpec((1, D_MODEL), lambda i: (0, 0)),
                  pl.BlockSpec((D_MODEL, LANES), lambda i: (0, 0)),
                  pl.BlockSpec((D_MODEL, LANES), lambda i: (0, 0))],
        out_specs=[pl.BlockSpec((tm, D_MODEL), lambda i: (i, 0)),
                   pl.BlockSpec((tm, LANES), lambda i: (i, 0))],
        out_shape=[jax.ShapeDtypeStruct((n, D_MODEL), BF16), jax.ShapeDtypeStruct((n, LANES), F32)],
        compiler_params=_cparams(("parallel",)),
        name="moe_route",
    )(x, g.reshape(1, D_MODEL), r_hi, r_lo)


def _moe_ffn_kernel(h_ref, gate_ref, x_ref, wg_ref, wu_ref, wd_ref, o_ref,
                    rank_ref, rank_t_ref, sel_t_ref, gate_t_ref, xe_ref, acc_ref, nb_ref, *, tm):
    e = pl.program_id(1)
    f = pl.program_id(2)
    nf = pl.num_programs(2)
    big, small = MOE_ROWS, MOE_ROWS // 2
    lane = lax.broadcasted_iota(I32, (1, LANES), 1)

    @pl.when((e == 0) & (f == 0))
    def _():
        o_ref[...] = x_ref[...]
        gates = gate_ref[...]
        sel = jnp.where(gates > 0.0, 1.0, 0.0)
        r = lax.broadcasted_iota(I32, (tm, tm), 0)
        c = lax.broadcasted_iota(I32, (tm, tm), 1)
        below = jnp.where(c < r, 1.0, 0.0).astype(BF16)
        rank = _dot(below, sel.astype(BF16))
        rank_ref[...] = rank
        rank_t_ref[...] = jnp.transpose(rank)
        sel_t_ref[...] = jnp.transpose(sel)
        gate_t_ref[...] = jnp.transpose(gates)

    @pl.when(f == 0)
    def _():
        cnt = jnp.sum(jnp.where(lane == e, jnp.sum(jnp.where(gate_ref[...] > 0.0, 1.0, 0.0), axis=0, keepdims=True),
                                0.0)).astype(I32)
        q = lax.div(cnt, big)
        rem = cnt - q * big
        merge = (q >= 1) & (rem > 0) & (rem <= small)
        merged_kind = jnp.where(rem <= small // 4, 2, jnp.where(rem <= small // 2, 3, 4))
        kind = jnp.where(rem == 0, 0, jnp.where(merge, merged_kind, jnp.where(rem <= small, 1, 0)))
        nb_ref[0] = q - jnp.where(merge, 1, 0) + jnp.where(rem > small, 1, 0)
        nb_ref[1] = kind

    n_big = nb_ref[0]
    last_kind = nb_ref[1]
    r_last = pl.multiple_of(n_big * big, big)
    last_rows = (small, big + small // 4, big + small // 2, big + small)

    def pick(r0, rows):
        want = (lax.broadcasted_iota(I32, (rows, 1), 0) + r0).astype(F32)
        return (rank_t_ref[pl.ds(e, 1), :] == want) & (sel_t_ref[pl.ds(e, 1), :] > 0.0)

    def for_blocks(fn):
        def body(rb, carry):
            fn(pl.multiple_of(rb * big, big), big)
            return carry

        lax.fori_loop(0, n_big, body, 0)
        for kind, rows in enumerate(last_rows, start=1):
            @pl.when(last_kind == kind)
            def _(rows=rows):
                fn(r_last, rows)

    def gather(r0, rows):
        p = jnp.where(pick(r0, rows), 1.0, 0.0).astype(BF16)
        xe_ref[pl.ds(r0, rows), :] = _dot(p, h_ref[...]).astype(BF16)

    def ffn(r0, rows):
        xe = xe_ref[pl.ds(r0, rows), :]
        a = _silu(_dot(xe, wg_ref[0])) * _dot(xe, wu_ref[0])
        y = _dot(a.astype(BF16), wd_ref[0])

        @pl.when(f == 0)
        def _():
            acc_ref[pl.ds(r0, rows), :] = y

        @pl.when(f > 0)
        def _():
            acc_ref[pl.ds(r0, rows), :] += y

    @pl.when(f == 0)
    def _():
        for_blocks(gather)

    for_blocks(ffn)

    @pl.when(f == nf - 1)
    def _():
        rank_col = jnp.sum(jnp.where(lane == e, rank_ref[...], 0.0), axis=1, keepdims=True)
        sel_col = jnp.sum(jnp.where(lane == e, gate_ref[...], 0.0), axis=1, keepdims=True) > 0.0

        def scatter(r0, rows):
            g_rows = jnp.sum(jnp.where(pick(r0, rows), gate_t_ref[pl.ds(e, 1), :], 0.0), axis=1, keepdims=True)
            y = (acc_ref[pl.ds(r0, rows), :] * g_rows).astype(BF16)
            col = (lax.broadcasted_iota(I32, (1, rows), 1) + r0).astype(F32)
            put = jnp.where((rank_col == col) & sel_col, 1.0, 0.0).astype(BF16)
            o_ref[...] += _dot(put, y)

        for_blocks(scatter)


def _moe_ffn(h, gates, x, wg, wu, wd, *, tm=1024, tf=1792):
    n = x.shape[0]
    tm = min(tm, n)
    kern = functools.partial(_moe_ffn_kernel, tm=tm)
    return pl.pallas_call(
        kern,
        grid=(n // tm, N_EXPERTS, D_FF // tf),
        in_specs=[
            pl.BlockSpec((tm, D_MODEL), lambda i, e, f: (i, 0)),
            pl.BlockSpec((tm, LANES), lambda i, e, f: (i, 0)),
            pl.BlockSpec((tm, D_MODEL), lambda i, e, f: (i, 0)),
            pl.BlockSpec((1, D_MODEL, tf), lambda i, e, f: (e, 0, f)),
            pl.BlockSpec((1, D_MODEL, tf), lambda i, e, f: (e, 0, f)),
            pl.BlockSpec((1, tf, D_MODEL), lambda i, e, f: (e, f, 0)),
        ],
        out_specs=pl.BlockSpec((tm, D_MODEL), lambda i, e, f: (i, 0)),
        out_shape=jax.ShapeDtypeStruct((n, D_MODEL), F32),
        scratch_shapes=[
            pltpu.VMEM((tm, LANES), F32),
            pltpu.VMEM((LANES, tm), F32),
            pltpu.VMEM((LANES, tm), F32),
            pltpu.VMEM((LANES, tm), F32),
            pltpu.VMEM((tm, D_MODEL), BF16),
            pltpu.VMEM((tm, D_MODEL), F32),
            pltpu.SMEM((2,), I32),
        ],
        compiler_params=_cparams(("parallel", "arbitrary", "arbitrary")),
        name="moe_ffn",
    )(h, gates, x, wg, wu, wd)


def _moe(x, g, r_hi, r_lo, wg, wu, wd):
    h, gates = _moe_route(x, g, r_hi, r_lo)
    return _moe_ffn(h, gates, x, wg, wu, wd)


def _ssd_kernel(xbc_ref, z_ref, dt_ref, cs_ref, h0_ref, cw_ref, cb_ref, dtb_ref, alog_ref,
                dexp_ref, og_ref, e_ref, y_ref, hout_ref, xpad_ref, h_ref, *, q):
    c = pl.program_id(1)

    @pl.when(c == 0)
    def _():
        xpad_ref[0:8, :] = cs_ref[0]
        for g in range(SSD_GROUPS):
            hg = h0_ref[0, g * SSD_HPG:(g + 1) * SSD_HPG].reshape(SSD_HPG * SSD_HEADDIM, D_STATE)
            h_ref[g] = jnp.transpose(hg)

    xbc = xbc_ref[0].astype(F32)
    xpad_ref[8:8 + q, :] = xbc
    cw = cw_ref[...]
    conv = cb_ref[...] + (xpad_ref[5:5 + q, :] * cw[0:1] + xpad_ref[6:6 + q, :] * cw[1:2]
                          + xpad_ref[7:7 + q, :] * cw[2:3] + xbc * cw[3:4])
    xpad_ref[0:8, :] = xpad_ref[q:q + 8, :]
    act = _silu(conv)

    lane = lax.broadcasted_iota(I32, (1, LANES), 1)
    xdt = dt_ref[0] + dtb_ref[...]
    sp = jnp.maximum(xdt, 0.0) + jnp.log1p(jnp.exp(-jnp.abs(xdt)))
    dt = jnp.where(lane < SSD_HEADS, sp, 0.0)
    a = dt * (-jnp.exp(alog_ref[...]))

    rows = lax.broadcasted_iota(I32, (q, q), 0)
    cols = lax.broadcasted_iota(I32, (q, q), 1)
    causal = rows >= cols
    tril = jnp.where(causal, 1.0, 0.0).astype(BF16)
    a3 = _split3(a)
    acum = _dot(tril, a3[0]) + _dot(tril, a3[1]) + _dot(tril, a3[2])
    acum_t = jnp.transpose(acum)

    e_mat = e_ref[...]
    ac3 = _split3(acum)
    acum_x = _dot(ac3[0], e_mat) + _dot(ac3[1], e_mat) + _dot(ac3[2], e_mat)
    dt3 = _split3(dt)
    dt_x = _dot(dt3[0], e_mat) + _dot(dt3[1], e_mat) + _dot(dt3[2], e_mat)
    last = acum_x[q - 1:q, :]
    eac_x = jnp.exp(acum_x)
    dte_x = jnp.exp(last - acum_x)
    blkdec = jnp.exp(last)

    xs = act[:, :D_INNER]
    x_dt = xs * dt_x
    xb = x_dt.astype(BF16)
    xdb = (x_dt * dte_x).astype(BF16)
    z = z_ref[0].astype(F32)
    lane_lo = lane < SSD_HEADDIM

    gw = SSD_HPG * SSD_HEADDIM
    for g in range(SSD_GROUPS):
        bg = act[:, D_INNER + g * D_STATE:D_INNER + (g + 1) * D_STATE]
        cg = act[:, D_INNER + GN + g * D_STATE:D_INNER + GN + (g + 1) * D_STATE]
        cgb = cg.astype(BF16)
        bgt = jnp.transpose(bg).astype(BF16)
        cb = _dot(cgb, bgt)
        gs = slice(g * gw, (g + 1) * gw)
        h_in = h_ref[g]
        y_off = _dot(cgb, h_in.astype(BF16)) * eac_x[:, gs]
        h_ref[g] = h_in * blkdec[:, gs] + _dot(bgt, xdb[:, gs])
        parts = []
        for pr in range(SSD_HPG // 2):
            h0 = g * SSD_HPG + 2 * pr
            xp = xb[:, (h0 // 2) * LANES:(h0 // 2 + 1) * LANES]
            ys = []
            for hh in (h0, h0 + 1):
                seg = acum[:, hh:hh + 1] - acum_t[hh:hh + 1, :]
                lm = jnp.exp(jnp.where(causal, seg, NEG_BIG))
                ys.append(_dot((cb * lm).astype(BF16), xp))
            parts.append(jnp.where(lane_lo, ys[0], ys[1]))
        y_g = jnp.concatenate(parts, axis=1) + y_off + dexp_ref[:, gs] * xs[:, gs]
        gt = y_g * _silu(z[:, gs])
        gn = gt * lax.rsqrt(jnp.mean(gt * gt, axis=-1, keepdims=True) + EPS) * og_ref[:, gs]
        y_ref[0, :, gs] = gn.astype(y_ref.dtype)

    @pl.when(c == pl.num_programs(1) - 1)
    def _():
        for g in range(SSD_GROUPS):
            hg = jnp.transpose(h_ref[g]).reshape(SSD_HPG, SSD_HEADDIM, D_STATE)
            hout_ref[0, g * SSD_HPG:(g + 1) * SSD_HPG] = hg


def _ssd(u, u_small, conv_state8, h0_t, cw8, cb, dtb, alog, dexp, og, e_mat, q):
    b, t, _ = u.shape
    kern = functools.partial(_ssd_kernel, q=q)
    full2 = lambda i, c: (0, 0)
    return pl.pallas_call(
        kern,
        grid=(b, t // q),
        in_specs=[
            pl.BlockSpec((1, q, CONV_DIM), lambda i, c: (i, c, 0)),
            pl.BlockSpec((1, q, D_INNER), lambda i, c: (i, c, 2)),
            pl.BlockSpec((1, q, LANES), lambda i, c: (i, c, 0)),
            pl.BlockSpec((1, 8, CONV_DIM), lambda i, c: (i, 0, 0)),
            pl.BlockSpec((1, SSD_HEADS, SSD_HEADDIM, D_STATE), lambda i, c: (i, 0, 0, 0)),
            pl.BlockSpec((8, CONV_DIM), full2),
            pl.BlockSpec((1, CONV_DIM), full2),
            pl.BlockSpec((1, LANES), full2),
            pl.BlockSpec((1, LANES), full2),
            pl.BlockSpec((1, D_INNER), full2),
            pl.BlockSpec((1, D_INNER), full2),
            pl.BlockSpec((LANES, D_INNER), full2),
        ],
        out_specs=[
            pl.BlockSpec((1, q, D_INNER), lambda i, c: (i, c, 0)),
            pl.BlockSpec((1, SSD_HEADS, SSD_HEADDIM, D_STATE), lambda i, c: (i, 0, 0, 0)),
        ],
        out_shape=[
            jax.ShapeDtypeStruct((b, t, D_INNER), BF16),
            jax.ShapeDtypeStruct((b, SSD_HEADS, SSD_HEADDIM, D_STATE), F32),
        ],
        scratch_shapes=[pltpu.VMEM((q + 8, CONV_DIM), F32),
                        pltpu.VMEM((SSD_GROUPS, D_STATE, SSD_HPG * SSD_HEADDIM), F32)],
        compiler_params=_cparams(("parallel", "arbitrary")),
        name="ssd",
    )(u, u, u_small, conv_state8, h0_t, cw8, cb, dtb, alog, dexp, og, e_mat)


V_ROWS = 80
DSA_TQ = LANES
DSA_HPT = 4


def _dsa_prep_kernel(q_ref, k_ref, v_ref, qi_ref, sm_ref, cos_ref, sin_ref, qg_ref, kg_ref, seg_ref,
                     ko_ref, kio_ref, qt_ref, qit_ref, wit_ref, *key_refs, tr):
    cos = cos_ref[...]
    sin = sin_ref[...]
    seg = seg_ref[...]
    lane = lax.broadcasted_iota(I32, (1, LANES), 1)
    nqb = tr // DSA_TQ

    def head_norm(x):
        s3 = _split3(x * x)
        ss = _dot(s3[0], seg) + _dot(s3[1], seg) + _dot(s3[2], seg)
        return x * lax.rsqrt(ss * (1.0 / ATT_HD) + EPS)

    def rope(x):
        partner = jnp.where((lane & 32) == 0, pltpu.roll(x, LANES - 32, 1), pltpu.roll(x, 32, 1))
        return x * cos + partner * sin

    def put_t(dst_ref, row0, x):
        for qb in range(nqb):
            xt = jnp.transpose(x[qb * DSA_TQ:(qb + 1) * DSA_TQ, :])
            dst_ref[0, qb, row0:row0 + LANES, :] = xt.astype(dst_ref.dtype)

    for c in range(ATT_Q // LANES):
        sl = slice(c * LANES, (c + 1) * LANES)
        put_t(qt_ref, c * LANES, rope(head_norm(q_ref[0, :, sl].astype(F32)) * qg_ref[...]))
    for c in range(IDX_Q // LANES):
        sl = slice(c * LANES, (c + 1) * LANES)
        put_t(qit_ref, c * LANES, rope(qi_ref[0, :, sl].astype(F32)))
    sm = sm_ref[0]
    for qb in range(nqb):
        wit_ref[0, qb] = jnp.transpose(sm[qb * DSA_TQ:(qb + 1) * DSA_TQ, :])[IDX_HD:IDX_HD + IDX_HEADS, :]
    ki = rope(sm)
    kio_ref[0] = ki
    k_rot = []
    for c in range(ATT_KV // LANES):
        sl = slice(c * LANES, (c + 1) * LANES)
        k_rot.append(rope(head_norm(k_ref[0, :, sl].astype(F32)) * kg_ref[...]))
        ko_ref[0, :, sl] = k_rot[c]

    if key_refs:
        kb_ref, kib_ref, vt_ref = key_refs
        kib_ref[0] = ki[:, :IDX_HD].astype(BF16)
        ones_row = jnp.where(lax.broadcasted_iota(I32, (V_ROWS - ATT_HD, tr), 0) == 0, 1.0, 0.0).astype(BF16)
        for c in range(ATT_KV // LANES):
            sl = slice(c * LANES, (c + 1) * LANES)
            kb_ref[0, :, sl] = k_rot[c].astype(BF16)
            vt = jnp.transpose(v_ref[0, :, sl].astype(F32)).astype(BF16)
            for hh in range(2):
                vt_ref[0, 2 * c + hh, 0, 0:ATT_HD, :] = vt[hh * ATT_HD:(hh + 1) * ATT_HD, :]
                vt_ref[0, 2 * c + hh, 0, ATT_HD:V_ROWS, :] = ones_row


def _dsa_prep(u, u_small, cos, sin, qg, kg, *, tr, emit_keys):
    b, t, _ = u.shape
    nb = t // DSA_TQ
    nqb = tr // DSA_TQ
    full2 = lambda i, j: (0, 0)
    seg = (jnp.arange(LANES)[:, None] // ATT_HD == jnp.arange(LANES)[None, :] // ATT_HD).astype(BF16)
    out_specs = [
        pl.BlockSpec((1, tr, ATT_KV), lambda i, j: (i, j, 0)),
        pl.BlockSpec((1, tr, LANES), lambda i, j: (i, j, 0)),
        pl.BlockSpec((1, nqb, ATT_Q, DSA_TQ), lambda i, j: (i, j, 0, 0)),
        pl.BlockSpec((1, nqb, IDX_Q, DSA_TQ), lambda i, j: (i, j, 0, 0)),
        pl.BlockSpec((1, nqb, IDX_HEADS, DSA_TQ), lambda i, j: (i, j, 0, 0)),
    ]
    out_shape = [
        jax.ShapeDtypeStruct((b, t, ATT_KV), F32),
        jax.ShapeDtypeStruct((b, t, LANES), F32),
        jax.ShapeDtypeStruct((b, nb, ATT_Q, DSA_TQ), BF16),
        jax.ShapeDtypeStruct((b, nb, IDX_Q, DSA_TQ), BF16),
        jax.ShapeDtypeStruct((b, nb, IDX_HEADS, DSA_TQ), F32),
    ]
    if emit_keys:
        out_specs += [
            pl.BlockSpec((1, tr, ATT_KV), lambda i, j: (i, j, 0)),
            pl.BlockSpec((1, tr, IDX_HD), lambda i, j: (i, j, 0)),
            pl.BlockSpec((1, KV_HEADS, 1, V_ROWS, tr), lambda i, j: (i, 0, j, 0, 0)),
        ]
        out_shape += [
            jax.ShapeDtypeStruct((b, t, ATT_KV), BF16),
            jax.ShapeDtypeStruct((b, t, IDX_HD), BF16),
            jax.ShapeDtypeStruct((b, KV_HEADS, t // tr, V_ROWS, tr), BF16),
        ]
    return pl.pallas_call(
        functools.partial(_dsa_prep_kernel, tr=tr),
        grid=(b, t // tr),
        in_specs=[
            pl.BlockSpec((1, tr, ATT_Q), lambda i, j: (i, j, 0)),
            pl.BlockSpec((1, tr, ATT_KV), lambda i, j: (i, j, 4)),
            pl.BlockSpec((1, tr, ATT_KV), lambda i, j: (i, j, 5)),
            pl.BlockSpec((1, tr, IDX_Q), lambda i, j: (i, j, 3)),
            pl.BlockSpec((1, tr, LANES), lambda i, j: (i, j, 0)),
            pl.BlockSpec((tr, LANES), lambda i, j: (j, 0)),
            pl.BlockSpec((tr, LANES), lambda i, j: (j, 0)),
            pl.BlockSpec((1, LANES), full2),
            pl.BlockSpec((1, LANES), full2),
            pl.BlockSpec((LANES, LANES), full2),
        ],
        out_specs=out_specs,
        out_shape=out_shape,
        compiler_params=_cparams(("parallel", "parallel")),
        name="dsa_prep",
    )(u, u, u, u, u_small, cos, sin, qg, kg, seg)


def _fold8(x, op, rows=8):
    r, n = x.shape
    x = x.reshape(r // rows, rows, n)
    while x.shape[0] > 1:
        h = x.shape[0] // 2
        y = op(x[:h], x[h:2 * h])
        x = y if x.shape[0] % 2 == 0 else jnp.concatenate([y, x[2 * h:]], axis=0)
    return x[0]


def _dsa_kernel(qit_ref, wit_ref, ki_ref, qt_ref, k_ref, vt_ref, o_ref,
                rhs_i_ref, rhs_q_ref, key_ref, s_ref, acc_ref, mm_ref,
                *, tq, cw, l_keys, start, nsel):
    i = pl.program_id(1)
    cols4 = ATT_GROUPS * tq
    last_pos = start + i * tq + (tq - 1)
    max_limit = jnp.minimum((lax.shift_right_logical(last_pos, 6) + 1) * CHUNK, l_keys)
    n_c = lax.div(max_limit + (cw - 1), cw)

    pos = start + i * tq + lax.broadcasted_iota(I32, (1, tq), 1)
    limit = jnp.minimum((lax.shift_right_logical(pos, 6) + 1) * CHUNK, l_keys)
    sub_pos = lax.broadcasted_iota(I32, (cw, 1), 0)

    for h in range(IDX_HEADS):
        rhs_i_ref[:, h * tq:(h + 1) * tq] = qit_ref[0, 0, h * IDX_HD:(h + 1) * IDX_HD, :]

    @pl.when(i == 0)
    def _():
        rhs_q_ref[...] = jnp.zeros_like(rhs_q_ref)

    for j in range(KV_HEADS):
        for g in range(ATT_GROUPS):
            hq = ATT_GROUPS * j + g
            rhs_q_ref[j, j * ATT_HD:(j + 1) * ATT_HD, g * tq:(g + 1) * tq] = \
                qt_ref[0, 0, hq * ATT_HD:(hq + 1) * ATT_HD, :]

    w = [wit_ref[0, 0, h:h + 1, :] * IDX_SCALE for h in range(IDX_HEADS)]

    def score_chunk(c, carry):
        d = jnp.maximum(_dot(ki_ref[0, c], rhs_i_ref[...]), 0.0)
        score = d[:, 0:tq] * w[0]
        for h in range(1, IDX_HEADS):
            score = score + d[:, h * tq:(h + 1) * tq] * w[h]
        bits = pltpu.bitcast(score, I32)
        key = jnp.where(bits < 0, bits ^ 0x7FFFFFFF, bits)
        key = jnp.where(score == 0.0, 0, key)
        key_ref[c] = jnp.where(sub_pos + c * cw < limit, key, INT_MIN)
        return carry

    lax.fori_loop(0, n_c, score_chunk, 0)

    def count(pred):
        def body(c, acc):
            return acc + _fold8(jnp.where(pred(key_ref[c], c * cw), 1.0, 0.0), jnp.add)

        acc = lax.fori_loop(0, n_c, body, jnp.zeros((8, tq), F32))
        return jnp.sum(acc, axis=0, keepdims=True)

    def search(it, lo):
        cand = lo + lax.shift_left(jnp.int32(1), 31 - it)
        cnt = count(lambda k, base: k >= cand)
        return jnp.where(cnt >= nsel, cand, lo)

    thr = lax.fori_loop(0, 32, search, jnp.full((1, tq), INT_MIN, I32))

    c_gt = count(lambda k, base: k > thr)
    n_eq = count(lambda k, base: k == thr)
    need = nsel - c_gt
    excess = jnp.where((n_eq > need) & (thr > INT_MIN), 1.0, 0.0)
    mm_ref[...] = jnp.full((1, tq), 1 << 14, I32)

    @pl.when(jnp.max(excess) > 0.0)
    def _():
        def tie_search(it, m):
            cand = m + lax.shift_left(jnp.int32(1), 13 - it)
            f = count(lambda k, base: (k == thr) & (sub_pos + base < cand))
            return jnp.where(f < need, cand, m)

        mm_ref[...] = lax.fori_loop(0, 14, tie_search, jnp.zeros((1, tq), I32))

    mm = mm_ref[...]

    scale = (ATT_HD ** -0.5) * math.log2(math.e)
    m0 = jnp.full((8, cols4), -3e38, F32)
    for j0 in range(0, KV_HEADS, DSA_HPT):
        pair = tuple(range(j0, j0 + DSA_HPT))

        def qk_chunk(c, ms, pair=pair):
            k = key_ref[c]
            kpos = sub_pos + c * cw
            sel = ((k > thr) | ((k == thr) & (kpos <= mm))) & (kpos < limit)
            b1 = jnp.where(sel, 0.0, NEG_BIG)
            b4 = jnp.concatenate([b1] * ATT_GROUPS, axis=1)
            out = []
            for jj, j in enumerate(pair):
                s = _dot(k_ref[0, c], rhs_q_ref[j]) * scale + b4
                s_ref[jj, c] = s
                out.append(jnp.maximum(ms[jj], _fold8(s, jnp.maximum)))
            return tuple(out)

        ms = lax.fori_loop(0, n_c, qk_chunk, (m0,) * DSA_HPT)
        ms = [jnp.max(m, axis=0, keepdims=True) for m in ms]
        acc_ref[...] = jnp.zeros_like(acc_ref)

        def pv_chunk(c, carry, pair=pair, ms=ms):
            for jj, j in enumerate(pair):
                p = jnp.exp2(s_ref[jj, c] - ms[jj])
                acc_ref[jj] += _dot(vt_ref[0, j, c], p.astype(BF16))
            return carry

        lax.fori_loop(0, n_c, pv_chunk, 0)
        for jj, j in enumerate(pair):
            a = acc_ref[jj]
            o = a[0:ATT_HD] / a[ATT_HD:ATT_HD + 1]
            for g in range(0, ATT_GROUPS, 2):
                two = jnp.concatenate([o[:, g * tq:(g + 1) * tq], o[:, (g + 1) * tq:(g + 2) * tq]], axis=0)
                lo = (ATT_GROUPS * j + g) * ATT_HD
                o_ref[0, :, lo:lo + 2 * ATT_HD] = jnp.transpose(two).astype(o_ref.dtype)


def _dsa(qit, wit, ki, qt, k, vt, *, tq, cw, l_keys, start):
    b, nb = qit.shape[0], qit.shape[1]
    nc = k.shape[1]
    nsel = min(TOPK_MAX, l_keys // 4)
    kern = functools.partial(_dsa_kernel, tq=tq, cw=cw, l_keys=l_keys, start=start, nsel=float(nsel))
    cols4 = ATT_GROUPS * tq
    return pl.pallas_call(
        kern,
        grid=(b, nb),
        in_specs=[
            pl.BlockSpec((1, 1, IDX_Q, tq), lambda i, j: (i, j, 0, 0)),
            pl.BlockSpec((1, 1, IDX_HEADS, tq), lambda i, j: (i, j, 0, 0)),
            pl.BlockSpec((1, nc, cw, IDX_HD), lambda i, j: (i, 0, 0, 0)),
            pl.BlockSpec((1, 1, ATT_Q, tq), lambda i, j: (i, j, 0, 0)),
            pl.BlockSpec((1, nc, cw, ATT_KV), lambda i, j: (i, 0, 0, 0)),
            pl.BlockSpec((1, KV_HEADS, nc, V_ROWS, cw), lambda i, j: (i, 0, 0, 0, 0)),
        ],
        out_specs=pl.BlockSpec((1, tq, ATT_Q), lambda i, j: (i, j, 0)),
        out_shape=jax.ShapeDtypeStruct((b, nb * tq, ATT_Q), BF16),
        scratch_shapes=[
            pltpu.VMEM((IDX_HD, IDX_HEADS * tq), BF16),
            pltpu.VMEM((KV_HEADS, ATT_KV, cols4), BF16),
            pltpu.VMEM((nc, cw, tq), I32),
            pltpu.VMEM((DSA_HPT, nc, cw, cols4), F32),
            pltpu.VMEM((DSA_HPT, V_ROWS, cols4), F32),
            pltpu.VMEM((1, tq), I32),
        ],
        compiler_params=_cparams(("parallel", "arbitrary")),
        name="dsa",
    )(qit, wit, ki, qt, k, vt)


def _dsa_key_layouts(k_all, v_all, ki_all, cw):
    b, l_keys, _ = k_all.shape
    nc = -(-l_keys // cw)
    padk = lambda a: jnp.pad(a, ((0, 0), (0, nc * cw - l_keys), (0, 0))).astype(BF16)
    vt = padk(v_all).reshape(b, nc, cw, KV_HEADS, ATT_HD).transpose(0, 3, 1, 4, 2)
    ones = jnp.ones((b, KV_HEADS, nc, 1, cw), BF16)
    zeros = jnp.zeros((b, KV_HEADS, nc, V_ROWS - ATT_HD - 1, cw), BF16)
    return (padk(ki_all).reshape(b, nc, cw, IDX_HD), padk(k_all).reshape(b, nc, cw, ATT_KV),
            jnp.concatenate([vt, ones, zeros], axis=3))


def _prep_weights(ssd_w_in, att_w_in, ssd_w_out, att_w_out, mem_w_kv, dense_w_gate, dense_w_up,
                  dense_w_down, moe_router, moe_w_gate, moe_w_up, moe_w_down):
    bf = lambda w: w.astype(BF16)
    w = ssd_w_in[0]
    o_xbc, o_dt, o_mq = D_INNER, D_INNER + CONV_DIM, D_INNER + CONV_DIM + SSD_HEADS
    ssd_main = bf(jnp.concatenate([w[:, o_xbc:o_dt], w[:, o_mq:], w[:, :D_INNER]], axis=1))
    ssd_small = bf(jnp.pad(w[:, o_dt:o_mq], ((0, 0), (0, LANES - SSD_HEADS))))
    w = att_w_in[0]
    o_wi = ATT_Q + 2 * ATT_KV + IDX_Q
    o_ki = o_wi + IDX_HEADS
    o_mq = o_ki + IDX_HD
    att_main = bf(jnp.concatenate([w[:, :o_wi], w[:, o_mq:]], axis=1))
    att_small = bf(jnp.pad(jnp.concatenate([w[:, o_ki:o_mq], w[:, o_wi:o_ki]], axis=1),
                           ((0, 0), (0, LANES - IDX_HD - IDX_HEADS))))
    r = jnp.pad(moe_router[0], ((0, 0), (0, LANES - N_EXPERTS)))
    r_hi = bf(r)
    r_lo = bf(r - r_hi.astype(F32))
    return dict(
        ssd_main=ssd_main, ssd_small=ssd_small, att_main=att_main, att_small=att_small,
        ssd_out_a=bf(ssd_w_out[0, :D_INNER]), ssd_out_b=bf(ssd_w_out[0, D_INNER:]),
        att_out_a=bf(att_w_out[0, :ATT_Q]), att_out_b=bf(att_w_out[0, ATT_Q:]),
        mem_w_kv=bf(mem_w_kv), dense_g=bf(dense_w_gate[0]), dense_u=bf(dense_w_up[0]),
        dense_d=bf(dense_w_down[0]), r_hi=r_hi, r_lo=r_lo,
        moe_g=bf(moe_w_gate[0]), moe_u=bf(moe_w_up[0]), moe_d=bf(moe_w_down[0]))


def _rope_tables(pos):
    half = ATT_HD // 2
    inv = ROPE_THETA ** (-jnp.arange(half, dtype=F32) / half)
    ang = pos.astype(F32)[:, None] * inv[None, :]
    cos = jnp.cos(ang)
    sin = jnp.sin(ang)
    cos_t = jnp.concatenate([cos, cos, cos, cos], axis=1)
    sin_t = jnp.concatenate([-sin, sin, -sin, sin], axis=1)
    return cos_t, sin_t


def _trunk(x, start, mem_k, mem_v, conv_in, ssm_in, kv_in, P, W, q_ssd, cw_dsa):
    b, t, _ = x.shape
    n = b * t
    x2 = x.reshape(n, D_MODEL)

    u = _norm_matmul(x2, W['ssd_norm'][0], P['ssd_main'], tn=1024, out_dtype=BF16).reshape(b, t, -1)
    u_small = _norm_matmul(x2, W['ssd_norm'][0], P['ssd_small'], tn=LANES).reshape(b, t, LANES)
    conv8 = jnp.pad(conv_in, ((0, 0), (8 - (CONV_W - 1), 0), (0, 0)))
    cw8 = jnp.pad(W['ssd_conv_w'][0], ((0, 8 - CONV_W), (0, 0)))
    pad_h = lambda v: jnp.pad(v.astype(F32), (0, LANES - SSD_HEADS)).reshape(1, LANES)
    e_mat = (jnp.arange(LANES)[:, None] == (jnp.arange(D_INNER)[None, :] // SSD_HEADDIM)).astype(BF16)
    y_mix, new_ssm = _ssd(u, u_small, conv8, ssm_in, cw8, W['ssd_conv_b'][0].reshape(1, CONV_DIM),
                      pad_h(W['ssd_dt_bias'][0]), pad_h(W['ssd_A_log'][0]),
                      jnp.repeat(W['ssd_D'][0].astype(F32), SSD_HEADDIM).reshape(1, D_INNER),
                      W['ssd_out_norm'][0].reshape(1, D_INNER), e_mat, q_ssd)
    new_conv = u[:, t - (CONV_W - 1):, :CONV_DIM].astype(F32)
    y_mem = _mem_attn(u, 3, mem_k[0], mem_v[0], W['mem_q_norm'][0])
    x2 = _proj_res(y_mix.reshape(n, D_INNER), y_mem.reshape(n, MEM_WIDTH), P['ssd_out_a'], P['ssd_out_b'], x2)
    x2 = _ffn(x2, W['ffn_norm'][0], P['dense_g'], P['dense_u'], P['dense_d'])

    u = _norm_matmul(x2, W['att_norm'][0], P['att_main'], tn=1024, out_dtype=BF16).reshape(b, t, -1)
    u_small = _norm_matmul(x2, W['att_norm'][0], P['att_small'], tn=LANES).reshape(b, t, LANES)
    pos = start + jnp.arange(t)
    cos_t, sin_t = _rope_tables(pos)
    tile2 = lambda v: jnp.tile(v.astype(F32), 2).reshape(1, LANES)
    v_new = u[:, :, ATT_Q + ATT_KV:ATT_Q + 2 * ATT_KV].astype(F32)
    qg, kg = tile2(W['att_q_norm'][0]), tile2(W['att_k_norm'][0])
    if kv_in is None:
        k_rot, ki_rot, qt, qit, wit, k_b, ki_b, vt = _dsa_prep(u, u_small, cos_t, sin_t, qg, kg,
                                                               tr=cw_dsa, emit_keys=True)
        nc = t // cw_dsa
        keys = (ki_b.reshape(b, nc, cw_dsa, IDX_HD), k_b.reshape(b, nc, cw_dsa, ATT_KV), vt)
        ki_new = ki_rot[:, :, :IDX_HD]
        l_keys = t
    else:
        padt = lambda a: jnp.pad(a, ((0, DSA_TQ - t),) + ((0, 0),) * (a.ndim - 1))
        padbt = lambda a: jnp.pad(a, ((0, 0), (0, DSA_TQ - t), (0, 0)))
        k_rot, ki_rot, qt, qit, wit = _dsa_prep(padbt(u), padbt(u_small), padt(cos_t), padt(sin_t), qg, kg,
                                                tr=DSA_TQ, emit_keys=False)
        k_rot, ki_rot = k_rot[:, :t], ki_rot[:, :t]
        ki_new = ki_rot[:, :, :IDX_HD]
        keys = _dsa_key_layouts(jnp.concatenate([kv_in[0], k_rot], axis=1),
                                jnp.concatenate([kv_in[1], v_new], axis=1),
                                jnp.concatenate([kv_in[2], ki_new], axis=1), cw_dsa)
        l_keys = kv_in[0].shape[1] + t
    o_t = _dsa(qit, wit, keys[0], qt, keys[1], keys[2], tq=DSA_TQ, cw=cw_dsa, l_keys=l_keys, start=start)
    y_mix = o_t[:, :t].reshape(n, ATT_Q)
    y_mem = _mem_attn(u, 2, mem_k[1], mem_v[1], W['mem_q_norm'][1])
    x2 = _proj_res(y_mix, y_mem.reshape(n, MEM_WIDTH), P['att_out_a'], P['att_out_b'], x2)
    x2 = _moe(x2, W['ffn_norm'][1], P['r_hi'], P['r_lo'], P['moe_g'], P['moe_u'], P['moe_d'])

    return (x2.reshape(b, t, D_MODEL), new_conv, new_ssm, k_rot.reshape(b, t, KV_HEADS, ATT_HD),
            v_new.reshape(b, t, KV_HEADS, ATT_HD), ki_new)


def kernel(x_prompt, x_sample, mem_prompt, cache_conv, state_ssm, cache_k, cache_v, cache_idx_k, cache_mem_k, cache_mem_v, ssd_norm, ssd_w_in, ssd_conv_w, ssd_conv_b, ssd_dt_bias, ssd_A_log, ssd_D, ssd_out_norm, ssd_w_out, att_norm, att_w_in, att_q_norm, att_k_norm, att_w_out, mem_norm, mem_w_kv, mem_q_norm, mem_k_norm, ffn_norm, dense_w_gate, dense_w_up, dense_w_down, moe_router, moe_w_gate, moe_w_up, moe_w_down):
    W = dict(ssd_norm=ssd_norm, ssd_conv_w=ssd_conv_w, ssd_conv_b=ssd_conv_b, ssd_dt_bias=ssd_dt_bias,
             ssd_A_log=ssd_A_log, ssd_D=ssd_D, ssd_out_norm=ssd_out_norm, att_norm=att_norm,
             att_q_norm=att_q_norm, att_k_norm=att_k_norm, mem_q_norm=mem_q_norm, ffn_norm=ffn_norm)
    P = _prep_weights(ssd_w_in, att_w_in, ssd_w_out, att_w_out, mem_w_kv, dense_w_gate, dense_w_up,
                      dense_w_down, moe_router, moe_w_gate, moe_w_up, moe_w_down)
    bp, sp = x_prompt.shape[0], x_prompt.shape[1]
    bs = x_sample.shape[0]

    mem2 = mem_prompt.reshape(bp * N_MEM, D_MODEL)
    pk, pv = [], []
    for i in range(2):
        kv = _norm_matmul(mem2, mem_norm[i], P['mem_w_kv'][i], tn=1024)
        pk.append(_head_norm(kv[:, :MEM_WIDTH], mem_k_norm[i]).reshape(bp, N_MEM, MEM_WIDTH))
        pv.append(kv[:, MEM_WIDTH:].reshape(bp, N_MEM, MEM_WIDTH))
    p_mem_k = jnp.stack(pk)
    p_mem_v = jnp.stack(pv)

    conv0 = jnp.zeros((bp, CONV_W - 1, CONV_DIM), F32)
    ssm0 = jnp.zeros((bp, SSD_HEADS, SSD_HEADDIM, D_STATE), F32)
    y_p, p_conv, p_ssm, p_k, p_v, p_ki = _trunk(x_prompt, 0, p_mem_k, p_mem_v, conv0, ssm0, None, P, W,
                                                q_ssd=128, cw_dsa=512)
    past = cache_k.shape[2]
    kv_in = (cache_k[0].reshape(bs, past, ATT_KV), cache_v[0].reshape(bs, past, ATT_KV), cache_idx_k[0])
    y_s, s_conv, s_ssm, s_k, s_v, s_ki = _trunk(
        x_sample, past, cache_mem_k.reshape(2, bs, N_MEM, MEM_WIDTH), cache_mem_v.reshape(2, bs, N_MEM, MEM_WIDTH),
        cache_conv[0], state_ssm[0], kv_in, P, W, q_ssd=x_sample.shape[1], cw_dsa=384)

    shp = (bp, N_MEM, MEM_HEADS, MEM_HD)
    return (y_p, y_s, p_conv[None], p_ssm[None], p_k[None], p_v[None], p_ki[None],
            p_mem_k.reshape((2,) + shp), p_mem_v.reshape((2,) + shp),
            s_conv[None], s_ssm[None], s_k[None], s_v[None], s_ki[None])
```

```python
import functools
import math

import jax
import jax.numpy as jnp
from jax import lax
from jax.experimental import pallas as pl
from jax.experimental.pallas import tpu as pltpu

F32 = jnp.float32
BF16 = jnp.bfloat16
I32 = jnp.int32

D_MODEL = 1024
CHUNK = 64
N_MEM = 256
EPS = 1e-6
D_INNER = 2048
SSD_HEADDIM = 64
SSD_HEADS = 32
SSD_GROUPS = 4
SSD_HPG = 8
D_STATE = 128
CONV_W = 4
GN = SSD_GROUPS * D_STATE
CONV_DIM = D_INNER + 2 * GN
ATT_HEADS = 16
ATT_HD = 64
KV_HEADS = 4
ATT_GROUPS = 4
ATT_Q = 1024
ATT_KV = 256
IDX_HEADS = 8
IDX_HD = 64
IDX_Q = 512
IDX_SCALE = (IDX_HEADS * IDX_HD) ** -0.5
TOPK_MAX = 256
ROPE_THETA = 10000.0
MEM_HEADS = 4
MEM_HD = 256
MEM_WIDTH = 1024
D_FF = 3584
N_EXPERTS = 8

LANES = 128
INT_MIN = -(2 ** 31)
NEG_BIG = -1e30
VMEM_LIMIT = 56 * 1024 * 1024


def _cparams(sem):
    return pltpu.CompilerParams(dimension_semantics=sem, vmem_limit_bytes=VMEM_LIMIT)


def _dot(a, b):
    return jnp.dot(a, b, preferred_element_type=F32)


def _split3(v):
    hi = v.astype(BF16)
    r1 = v - hi.astype(F32)
    mid = r1.astype(BF16)
    lo = (r1 - mid.astype(F32)).astype(BF16)
    return hi, mid, lo


def _silu(x):
    return x * (0.5 * jnp.tanh(0.5 * x) + 0.5)


def _norm_matmul_kernel(x_ref, g_ref, w_ref, o_ref, h_ref):
    @pl.when(pl.program_id(1) == 0)
    def _():
        x = x_ref[...]
        ms = jnp.mean(x * x, axis=-1, keepdims=True)
        h_ref[...] = (x * lax.rsqrt(ms + EPS) * g_ref[...]).astype(BF16)

    o_ref[...] = _dot(h_ref[...], w_ref[...]).astype(o_ref.dtype)


def _norm_matmul(x, g, w, tn, out_dtype=F32):
    n, k = x.shape
    m = w.shape[1]
    tm = min(1024, n)
    return pl.pallas_call(
        _norm_matmul_kernel,
        grid=(n // tm, m // tn),
        in_specs=[
            pl.BlockSpec((tm, k), lambda i, j: (i, 0)),
            pl.BlockSpec((1, k), lambda i, j: (0, 0)),
            pl.BlockSpec((k, tn), lambda i, j: (0, j)),
        ],
        out_specs=pl.BlockSpec((tm, tn), lambda i, j: (i, j)),
        out_shape=jax.ShapeDtypeStruct((n, m), out_dtype),
        scratch_shapes=[pltpu.VMEM((tm, k), BF16)],
        compiler_params=_cparams(("parallel", "arbitrary")),
        name="norm_matmul",
    )(x, g.reshape(1, k), w)


def _head_norm_kernel(x_ref, g_ref, o_ref):
    for h in range(MEM_HEADS):
        x = x_ref[:, h * MEM_HD:(h + 1) * MEM_HD]
        ms = jnp.mean(x * x, axis=-1, keepdims=True)
        o_ref[:, h * MEM_HD:(h + 1) * MEM_HD] = x * lax.rsqrt(ms + EPS) * g_ref[...]


def _head_norm(x, g):
    n = x.shape[0]
    tm = min(512, n)
    return pl.pallas_call(
        _head_norm_kernel,
        grid=(n // tm,),
        in_specs=[pl.BlockSpec((tm, MEM_WIDTH), lambda i: (i, 0)),
                  pl.BlockSpec((1, MEM_HD), lambda i: (0, 0))],
        out_specs=pl.BlockSpec((tm, MEM_WIDTH), lambda i: (i, 0)),
        out_shape=jax.ShapeDtypeStruct((n, MEM_WIDTH), F32),
        compiler_params=_cparams(("parallel",)),
        name="head_norm",
    )(x, g.reshape(1, MEM_HD))


def _mem_attn_kernel(q_ref, k_ref, v_ref, g_ref, o_ref):
    for h in range(MEM_HEADS):
        sl = slice(h * MEM_HD, (h + 1) * MEM_HD)
        q = q_ref[0, :, sl].astype(F32)
        ms = jnp.mean(q * q, axis=-1, keepdims=True)
        qn = (q * lax.rsqrt(ms + EPS) * g_ref[...]).astype(BF16)
        k = k_ref[0, :, sl].astype(BF16)
        s = lax.dot_general(qn, k, (((1,), (1,)), ((), ())), preferred_element_type=F32)
        s = s * (MEM_HD ** -0.5)
        m = jnp.max(s, axis=-1, keepdims=True)
        p = jnp.exp(s - m)
        p = p / jnp.sum(p, axis=-1, keepdims=True)
        o = _dot(p.astype(BF16), v_ref[0, :, sl].astype(BF16))
        o_ref[0, :, sl] = o.astype(o_ref.dtype)


def _mem_attn(u, col_block, mk, mv, g):
    b, t, _ = u.shape
    tq = min(512, t)
    return pl.pallas_call(
        _mem_attn_kernel,
        grid=(b, t // tq),
        in_specs=[
            pl.BlockSpec((1, tq, MEM_WIDTH), lambda i, j: (i, j, col_block)),
            pl.BlockSpec((1, N_MEM, MEM_WIDTH), lambda i, j: (i, 0, 0)),
            pl.BlockSpec((1, N_MEM, MEM_WIDTH), lambda i, j: (i, 0, 0)),
            pl.BlockSpec((1, MEM_HD), lambda i, j: (0, 0)),
        ],
        out_specs=pl.BlockSpec((1, tq, MEM_WIDTH), lambda i, j: (i, j, 0)),
        out_shape=jax.ShapeDtypeStruct((b, t, MEM_WIDTH), BF16),
        compiler_params=_cparams(("parallel", "parallel")),
        name="mem_attn",
    )(u, mk, mv, g.reshape(1, MEM_HD))


def _proj_res_kernel(a_ref, b_ref, wa_ref, wb_ref, x_ref, o_ref):
    o_ref[...] = x_ref[...] + _dot(a_ref[...], wa_ref[...]) + _dot(b_ref[...], wb_ref[...])


def _proj_res(a, b, wa, wb, x):
    n = x.shape[0]
    tm = min(512, n)
    ka, kb = a.shape[1], b.shape[1]
    return pl.pallas_call(
        _proj_res_kernel,
        grid=(n // tm,),
        in_specs=[
            pl.BlockSpec((tm, ka), lambda i: (i, 0)),
            pl.BlockSpec((tm, kb), lambda i: (i, 0)),
            pl.BlockSpec((ka, D_MODEL), lambda i: (0, 0)),
            pl.BlockSpec((kb, D_MODEL), lambda i: (0, 0)),
            pl.BlockSpec((tm, D_MODEL), lambda i: (i, 0)),
        ],
        out_specs=pl.BlockSpec((tm, D_MODEL), lambda i: (i, 0)),
        out_shape=jax.ShapeDtypeStruct((n, D_MODEL), F32),
        compiler_params=_cparams(("parallel",)),
        name="proj_res",
    )(a, b, wa, wb, x)


def _ffn_kernel(x_ref, g_ref, wg_ref, wu_ref, wd_ref, o_ref, h_ref, acc_ref):
    f = pl.program_id(1)

    @pl.when(f == 0)
    def _():
        x = x_ref[...]
        ms = jnp.mean(x * x, axis=-1, keepdims=True)
        h_ref[...] = (x * lax.rsqrt(ms + EPS) * g_ref[...]).astype(BF16)
        acc_ref[...] = x

    h = h_ref[...]
    a = _silu(_dot(h, wg_ref[...])) * _dot(h, wu_ref[...])
    acc_ref[...] += _dot(a.astype(BF16), wd_ref[...])

    @pl.when(f == pl.num_programs(1) - 1)
    def _():
        o_ref[...] = acc_ref[...]


def _ffn(x, g, wg, wu, wd, tf=512):
    n = x.shape[0]
    tm = min(1024, n)
    return pl.pallas_call(
        _ffn_kernel,
        grid=(n // tm, D_FF // tf),
        in_specs=[
            pl.BlockSpec((tm, D_MODEL), lambda i, f: (i, 0)),
            pl.BlockSpec((1, D_MODEL), lambda i, f: (0, 0)),
            pl.BlockSpec((D_MODEL, tf), lambda i, f: (0, f)),
            pl.BlockSpec((D_MODEL, tf), lambda i, f: (0, f)),
            pl.BlockSpec((tf, D_MODEL), lambda i, f: (f, 0)),
        ],
        out_specs=pl.BlockSpec((tm, D_MODEL), lambda i, f: (i, 0)),
        out_shape=jax.ShapeDtypeStruct((n, D_MODEL), F32),
        scratch_shapes=[pltpu.VMEM((tm, D_MODEL), BF16), pltpu.VMEM((tm, D_MODEL), F32)],
        compiler_params=_cparams(("parallel", "arbitrary")),
        name="ffn",
    )(x, g.reshape(1, D_MODEL), wg, wu, wd)


MOE_ROWS = 256


def _split2(v):
    hi = v.astype(BF16)
    return hi, (v - hi.astype(F32)).astype(BF16)


def _moe_route_kernel(x_ref, g_ref, rh_ref, rl_ref, h_ref, gate_ref):
    lane = lax.broadcasted_iota(I32, (1, LANES), 1)
    x = x_ref[...]
    ms = jnp.mean(x * x, axis=-1, keepdims=True)
    hf = x * lax.rsqrt(ms + EPS) * g_ref[...]
    hb, hl = _split2(hf)
    h_ref[...] = hb
    logits = _dot(hb, rh_ref[...]) + _dot(hl, rh_ref[...]) + _dot(hb, rl_ref[...])
    valid = lane < N_EXPERTS
    logits = jnp.where(valid, logits, NEG_BIG)
    m = jnp.max(logits, axis=-1, keepdims=True)
    p = jnp.exp(logits - m)
    p = p / jnp.sum(p, axis=-1, keepdims=True)
    p = jnp.where(valid, p, -1.0)
    v1 = jnp.max(p, axis=-1, keepdims=True)
    i1 = jnp.min(jnp.where(p == v1, lane, LANES), axis=-1, keepdims=True)
    p2 = jnp.where(lane == i1, -1.0, p)
    v2 = jnp.max(p2, axis=-1, keepdims=True)
    i2 = jnp.min(jnp.where(p2 == v2, lane, LANES), axis=-1, keepdims=True)
    den = v1 + v2
    gate_ref[...] = jnp.where(lane == i1, v1 / den, jnp.where(lane == i2, v2 / den, 0.0))


def _moe_route(x, g, r_hi, r_lo):
    n = x.shape[0]
    tm = min(512, n)
    return pl.pallas_call(
        _moe_route_kernel,
        grid=(n // tm,),
        in_specs=[pl.BlockSpec((tm, D_MODEL), lambda i: (i, 0)),
                  pl.BlockSpec((1, D_MODEL), lambda i: (0, 0)),
                  pl.BlockSpec((D_MODEL, LANES), lambda i: (0, 0)),
                  pl.BlockSpec((D_MODEL, LANES), lambda i: (0, 0))],
        out_specs=[pl.BlockSpec((tm, D_MODEL), lambda i: (i, 0)),
                   pl.BlockSpec((tm, LANES), lambda i: (i, 0))],
        out_shape=[jax.ShapeDtypeStruct((n, D_MODEL), BF16), jax.ShapeDtypeStruct((n, LANES), F32)],
        compiler_params=_cparams(("parallel",)),
        name="moe_route",
    )(x, g.reshape(1, D_MODEL), r_hi, r_lo)


def _moe_ffn_kernel(h_ref, gate_ref, x_ref, wg_ref, wu_ref, wd_ref, o_ref,
                    rank_ref, rank_t_ref, sel_t_ref, gate_t_ref, xe_ref, acc_ref, nb_ref, *, tm):
    e = pl.program_id(1)
    f = pl.program_id(2)
    nf = pl.num_programs(2)
    big, small = MOE_ROWS, MOE_ROWS // 2
    lane = lax.broadcasted_iota(I32, (1, LANES), 1)

    @pl.when((e == 0) & (f == 0))
    def _():
        o_ref[...] = x_ref[...]
        gates = gate_ref[...]
        sel = jnp.where(gates > 0.0, 1.0, 0.0)
        r = lax.broadcasted_iota(I32, (tm, tm), 0)
        c = lax.broadcasted_iota(I32, (tm, tm), 1)
        below = jnp.where(c < r, 1.0, 0.0).astype(BF16)
        rank = _dot(below, sel.astype(BF16))
        rank_ref[...] = rank
        rank_t_ref[...] = jnp.transpose(rank)
        sel_t_ref[...] = jnp.transpose(sel)
        gate_t_ref[...] = jnp.transpose(gates)

    @pl.when(f == 0)
    def _():
        cnt = jnp.sum(jnp.where(lane == e, jnp.sum(jnp.where(gate_ref[...] > 0.0, 1.0, 0.0), axis=0, keepdims=True),
                                0.0)).astype(I32)
        q = lax.div(cnt, big)
        rem = cnt - q * big
        merge = (q >= 1) & (rem > 0) & (rem <= small)
        merged_kind = jnp.where(rem <= small // 4, 2, jnp.where(rem <= small // 2, 3, 4))
        kind = jnp.where(rem == 0, 0, jnp.where(merge, merged_kind, jnp.where(rem <= small, 1, 0)))
        nb_ref[0] = q - jnp.where(merge, 1, 0) + jnp.where(rem > small, 1, 0)
        nb_ref[1] = kind

    n_big = nb_ref[0]
    last_kind = nb_ref[1]
    r_last = pl.multiple_of(n_big * big, big)
    last_rows = (small, big + small // 4, big + small // 2, big + small)

    def pick(r0, rows):
        want = (lax.broadcasted_iota(I32, (rows, 1), 0) + r0).astype(F32)
        return (rank_t_ref[pl.ds(e, 1), :] == want) & (sel_t_ref[pl.ds(e, 1), :] > 0.0)

    def for_blocks(fn):
        def body(rb, carry):
            fn(pl.multiple_of(rb * big, big), big)
            return carry

        lax.fori_loop(0, n_big, body, 0)
        for kind, rows in enumerate(last_rows, start=1):
            @pl.when(last_kind == kind)
            def _(rows=rows):
                fn(r_last, rows)

    def gather(r0, rows):
        p = jnp.where(pick(r0, rows), 1.0, 0.0).astype(BF16)
        xe_ref[pl.ds(r0, rows), :] = _dot(p, h_ref[...]).astype(BF16)

    def ffn(r0, rows):
        xe = xe_ref[pl.ds(r0, rows), :]
        a = _silu(_dot(xe, wg_ref[0])) * _dot(xe, wu_ref[0])
        y = _dot(a.astype(BF16), wd_ref[0])

        @pl.when(f == 0)
        def _():
            acc_ref[pl.ds(r0, rows), :] = y

        @pl.when(f > 0)
        def _():
            acc_ref[pl.ds(r0, rows), :] += y

    @pl.when(f == 0)
    def _():
        for_blocks(gather)

    for_blocks(ffn)

    @pl.when(f == nf - 1)
    def _():
        rank_col = jnp.sum(jnp.where(lane == e, rank_ref[...], 0.0), axis=1, keepdims=True)
        sel_col = jnp.sum(jnp.where(lane == e, gate_ref[...], 0.0), axis=1, keepdims=True) > 0.0

        def scatter(r0, rows):
            g_rows = jnp.sum(jnp.where(pick(r0, rows), gate_t_ref[pl.ds(e, 1), :], 0.0), axis=1, keepdims=True)
            y = (acc_ref[pl.ds(r0, rows), :] * g_rows).astype(BF16)
            col = (lax.broadcasted_iota(I32, (1, rows), 1) + r0).astype(F32)
            put = jnp.where((rank_col == col) & sel_col, 1.0, 0.0).astype(BF16)
            o_ref[...] += _dot(put, y)

        for_blocks(scatter)


def _moe_ffn(h, gates, x, wg, wu, wd, *, tm=1024, tf=1792):
    n = x.shape[0]
    tm = min(tm, n)
    kern = functools.partial(_moe_ffn_kernel, tm=tm)
    return pl.pallas_call(
        kern,
        grid=(n // tm, N_EXPERTS, D_FF // tf),
        in_specs=[
            pl.BlockSpec((tm, D_MODEL), lambda i, e, f: (i, 0)),
            pl.BlockSpec((tm, LANES), lambda i, e, f: (i, 0)),
            pl.BlockSpec((tm, D_MODEL), lambda i, e, f: (i, 0)),
            pl.BlockSpec((1, D_MODEL, tf), lambda i, e, f: (e, 0, f)),
            pl.BlockSpec((1, D_MODEL, tf), lambda i, e, f: (e, 0, f)),
            pl.BlockSpec((1, tf, D_MODEL), lambda i, e, f: (e, f, 0)),
        ],
        out_specs=pl.BlockSpec((tm, D_MODEL), lambda i, e, f: (i, 0)),
        out_shape=jax.ShapeDtypeStruct((n, D_MODEL), F32),
        scratch_shapes=[
            pltpu.VMEM((tm, LANES), F32),
            pltpu.VMEM((LANES, tm), F32),
            pltpu.VMEM((LANES, tm), F32),
            pltpu.VMEM((LANES, tm), F32),
            pltpu.VMEM((tm, D_MODEL), BF16),
            pltpu.VMEM((tm, D_MODEL), F32),
            pltpu.SMEM((2,), I32),
        ],
        compiler_params=_cparams(("parallel", "arbitrary", "arbitrary")),
        name="moe_ffn",
    )(h, gates, x, wg, wu, wd)


def _moe(x, g, r_hi, r_lo, wg, wu, wd):
    h, gates = _moe_route(x, g, r_hi, r_lo)
    return _moe_ffn(h, gates, x, wg, wu, wd)


def _ssd_kernel(xbc_ref, z_ref, dt_ref, cs_ref, h0_ref, cw_ref, cb_ref, dtb_ref, alog_ref,
                dexp_ref, og_ref, e_ref, y_ref, hout_ref, xpad_ref, h_ref, *, q):
    c = pl.program_id(1)

    @pl.when(c == 0)
    def _():
        xpad_ref[0:8, :] = cs_ref[0]
        for g in range(SSD_GROUPS):
            hg = h0_ref[0, g * SSD_HPG:(g + 1) * SSD_HPG].reshape(SSD_HPG * SSD_HEADDIM, D_STATE)
            h_ref[g] = jnp.transpose(hg)

    xbc = xbc_ref[0].astype(F32)
    xpad_ref[8:8 + q, :] = xbc
    cw = cw_ref[...]
    conv = cb_ref[...] + (xpad_ref[5:5 + q, :] * cw[0:1] + xpad_ref[6:6 + q, :] * cw[1:2]
                          + xpad_ref[7:7 + q, :] * cw[2:3] + xbc * cw[3:4])
    xpad_ref[0:8, :] = xpad_ref[q:q + 8, :]
    act = _silu(conv)

    lane = lax.broadcasted_iota(I32, (1, LANES), 1)
    xdt = dt_ref[0] + dtb_ref[...]
    sp = jnp.maximum(xdt, 0.0) + jnp.log1p(jnp.exp(-jnp.abs(xdt)))
    dt = jnp.where(lane < SSD_HEADS, sp, 0.0)
    a = dt * (-jnp.exp(alog_ref[...]))

    rows = lax.broadcasted_iota(I32, (q, q), 0)
    cols = lax.broadcasted_iota(I32, (q, q), 1)
    causal = rows >= cols
    tril = jnp.where(causal, 1.0, 0.0).astype(BF16)
    a3 = _split3(a)
    acum = _dot(tril, a3[0]) + _dot(tril, a3[1]) + _dot(tril, a3[2])
    acum_t = jnp.transpose(acum)

    e_mat = e_ref[...]
    ac3 = _split3(acum)
    acum_x = _dot(ac3[0], e_mat) + _dot(ac3[1], e_mat) + _dot(ac3[2], e_mat)
    dt3 = _split3(dt)
    dt_x = _dot(dt3[0], e_mat) + _dot(dt3[1], e_mat) + _dot(dt3[2], e_mat)
    last = acum_x[q - 1:q, :]
    eac_x = jnp.exp(acum_x)
    dte_x = jnp.exp(last - acum_x)
    blkdec = jnp.exp(last)

    xs = act[:, :D_INNER]
    x_dt = xs * dt_x
    xb = x_dt.astype(BF16)
    xdb = (x_dt * dte_x).astype(BF16)
    z = z_ref[0].astype(F32)
    lane_lo = lane < SSD_HEADDIM

    gw = SSD_HPG * SSD_HEADDIM
    for g in range(SSD_GROUPS):
        bg = act[:, D_INNER + g * D_STATE:D_INNER + (g + 1) * D_STATE]
        cg = act[:, D_INNER + GN + g * D_STATE:D_INNER + GN + (g + 1) * D_STATE]
        cgb = cg.astype(BF16)
        bgt = jnp.transpose(bg).astype(BF16)
        cb = _dot(cgb, bgt)
        gs = slice(g * gw, (g + 1) * gw)
        h_in = h_ref[g]
        y_off = _dot(cgb, h_in.astype(BF16)) * eac_x[:, gs]
        h_ref[g] = h_in * blkdec[:, gs] + _dot(bgt, xdb[:, gs])
        parts = []
        for pr in range(SSD_HPG // 2):
            h0 = g * SSD_HPG + 2 * pr
            xp = xb[:, (h0 // 2) * LANES:(h0 // 2 + 1) * LANES]
            ys = []
            for hh in (h0, h0 + 1):
                seg = acum[:, hh:hh + 1] - acum_t[hh:hh + 1, :]
                lm = jnp.exp(jnp.where(causal, seg, NEG_BIG))
                ys.append(_dot((cb * lm).astype(BF16), xp))
            parts.append(jnp.where(lane_lo, ys[0], ys[1]))
        y_g = jnp.concatenate(parts, axis=1) + y_off + dexp_ref[:, gs] * xs[:, gs]
        gt = y_g * _silu(z[:, gs])
        gn = gt * lax.rsqrt(jnp.mean(gt * gt, axis=-1, keepdims=True) + EPS) * og_ref[:, gs]
        y_ref[0, :, gs] = gn.astype(y_ref.dtype)

    @pl.when(c == pl.num_programs(1) - 1)
    def _():
        for g in range(SSD_GROUPS):
            hg = jnp.transpose(h_ref[g]).reshape(SSD_HPG, SSD_HEADDIM, D_STATE)
            hout_ref[0, g * SSD_HPG:(g + 1) * SSD_HPG] = hg


def _ssd(u, u_small, conv_state8, h0_t, cw8, cb, dtb, alog, dexp, og, e_mat, q):
    b, t, _ = u.shape
    kern = functools.partial(_ssd_kernel, q=q)
    full2 = lambda i, c: (0, 0)
    return pl.pallas_call(
        kern,
        grid=(b, t // q),
        in_specs=[
            pl.BlockSpec((1, q, CONV_DIM), lambda i, c: (i, c, 0)),
            pl.BlockSpec((1, q, D_INNER), lambda i, c: (i, c, 2)),
            pl.BlockSpec((1, q, LANES), lambda i, c: (i, c, 0)),
            pl.BlockSpec((1, 8, CONV_DIM), lambda i, c: (i, 0, 0)),
            pl.BlockSpec((1, SSD_HEADS, SSD_HEADDIM, D_STATE), lambda i, c: (i, 0, 0, 0)),
            pl.BlockSpec((8, CONV_DIM), full2),
            pl.BlockSpec((1, CONV_DIM), full2),
            pl.BlockSpec((1, LANES), full2),
            pl.BlockSpec((1, LANES), full2),
            pl.BlockSpec((1, D_INNER), full2),
            pl.BlockSpec((1, D_INNER), full2),
            pl.BlockSpec((LANES, D_INNER), full2),
        ],
        out_specs=[
            pl.BlockSpec((1, q, D_INNER), lambda i, c: (i, c, 0)),
            pl.BlockSpec((1, SSD_HEADS, SSD_HEADDIM, D_STATE), lambda i, c: (i, 0, 0, 0)),
        ],
        out_shape=[
            jax.ShapeDtypeStruct((b, t, D_INNER), BF16),
            jax.ShapeDtypeStruct((b, SSD_HEADS, SSD_HEADDIM, D_STATE), F32),
        ],
        scratch_shapes=[pltpu.VMEM((q + 8, CONV_DIM), F32),
                        pltpu.VMEM((SSD_GROUPS, D_STATE, SSD_HPG * SSD_HEADDIM), F32)],
        compiler_params=_cparams(("parallel", "arbitrary")),
        name="ssd",
    )(u, u, u_small, conv_state8, h0_t, cw8, cb, dtb, alog, dexp, og, e_mat)


V_ROWS = 80
DSA_TQ = LANES
DSA_HPT = 4


def _dsa_prep_kernel(q_ref, k_ref, v_ref, qi_ref, sm_ref, cos_ref, sin_ref, qg_ref, kg_ref, seg_ref,
                     ko_ref, kio_ref, qt_ref, qit_ref, wit_ref, *key_refs, tr):
    cos = cos_ref[...]
    sin = sin_ref[...]
    seg = seg_ref[...]
    lane = lax.broadcasted_iota(I32, (1, LANES), 1)
    nqb = tr // DSA_TQ

    def head_norm(x):
        s3 = _split3(x * x)
        ss = _dot(s3[0], seg) + _dot(s3[1], seg) + _dot(s3[2], seg)
        return x * lax.rsqrt(ss * (1.0 / ATT_HD) + EPS)

    def rope(x):
        partner = jnp.where((lane & 32) == 0, pltpu.roll(x, LANES - 32, 1), pltpu.roll(x, 32, 1))
        return x * cos + partner * sin

    def put_t(dst_ref, row0, x):
        for qb in range(nqb):
            xt = jnp.transpose(x[qb * DSA_TQ:(qb + 1) * DSA_TQ, :])
            dst_ref[0, qb, row0:row0 + LANES, :] = xt.astype(dst_ref.dtype)

    for c in range(ATT_Q // LANES):
        sl = slice(c * LANES, (c + 1) * LANES)
        put_t(qt_ref, c * LANES, rope(head_norm(q_ref[0, :, sl].astype(F32)) * qg_ref[...]))
    for c in range(IDX_Q // LANES):
        sl = slice(c * LANES, (c + 1) * LANES)
        put_t(qit_ref, c * LANES, rope(qi_ref[0, :, sl].astype(F32)))
    sm = sm_ref[0]
    for qb in range(nqb):
        wit_ref[0, qb] = jnp.transpose(sm[qb * DSA_TQ:(qb + 1) * DSA_TQ, :])[IDX_HD:IDX_HD + IDX_HEADS, :]
    ki = rope(sm)
    kio_ref[0] = ki
    k_rot = []
    for c in range(ATT_KV // LANES):
        sl = slice(c * LANES, (c + 1) * LANES)
        k_rot.append(rope(head_norm(k_ref[0, :, sl].astype(F32)) * kg_ref[...]))
        ko_ref[0, :, sl] = k_rot[c]

    if key_refs:
        kb_ref, kib_ref, vt_ref = key_refs
        kib_ref[0] = ki[:, :IDX_HD].astype(BF16)
        ones_row = jnp.where(lax.broadcasted_iota(I32, (V_ROWS - ATT_HD, tr), 0) == 0, 1.0, 0.0).astype(BF16)
        for c in range(ATT_KV // LANES):
            sl = slice(c * LANES, (c + 1) * LANES)
            kb_ref[0, :, sl] = k_rot[c].astype(BF16)
            vt = jnp.transpose(v_ref[0, :, sl].astype(F32)).astype(BF16)
            for hh in range(2):
                vt_ref[0, 2 * c + hh, 0, 0:ATT_HD, :] = vt[hh * ATT_HD:(hh + 1) * ATT_HD, :]
                vt_ref[0, 2 * c + hh, 0, ATT_HD:V_ROWS, :] = ones_row


def _dsa_prep(u, u_small, cos, sin, qg, kg, *, tr, emit_keys):
    b, t, _ = u.shape
    nb = t // DSA_TQ
    nqb = tr // DSA_TQ
    full2 = lambda i, j: (0, 0)
    seg = (jnp.arange(LANES)[:, None] // ATT_HD == jnp.arange(LANES)[None, :] // ATT_HD).astype(BF16)
    out_specs = [
        pl.BlockSpec((1, tr, ATT_KV), lambda i, j: (i, j, 0)),
        pl.BlockSpec((1, tr, LANES), lambda i, j: (i, j, 0)),
        pl.BlockSpec((1, nqb, ATT_Q, DSA_TQ), lambda i, j: (i, j, 0, 0)),
        pl.BlockSpec((1, nqb, IDX_Q, DSA_TQ), lambda i, j: (i, j, 0, 0)),
        pl.BlockSpec((1, nqb, IDX_HEADS, DSA_TQ), lambda i, j: (i, j, 0, 0)),
    ]
    out_shape = [
        jax.ShapeDtypeStruct((b, t, ATT_KV), F32),
        jax.ShapeDtypeStruct((b, t, LANES), F32),
        jax.ShapeDtypeStruct((b, nb, ATT_Q, DSA_TQ), BF16),
        jax.ShapeDtypeStruct((b, nb, IDX_Q, DSA_TQ), BF16),
        jax.ShapeDtypeStruct((b, nb, IDX_HEADS, DSA_TQ), F32),
    ]
    if emit_keys:
        out_specs += [
            pl.BlockSpec((1, tr, ATT_KV), lambda i, j: (i, j, 0)),
            pl.BlockSpec((1, tr, IDX_HD), lambda i, j: (i, j, 0)),
            pl.BlockSpec((1, KV_HEADS, 1, V_ROWS, tr), lambda i, j: (i, 0, j, 0, 0)),
        ]
        out_shape += [
            jax.ShapeDtypeStruct((b, t, ATT_KV), BF16),
            jax.ShapeDtypeStruct((b, t, IDX_HD), BF16),
            jax.ShapeDtypeStruct((b, KV_HEADS, t // tr, V_ROWS, tr), BF16),
        ]
    return pl.pallas_call(
        functools.partial(_dsa_prep_kernel, tr=tr),
        grid=(b, t // tr),
        in_specs=[
            pl.BlockSpec((1, tr, ATT_Q), lambda i, j: (i, j, 0)),
            pl.BlockSpec((1, tr, ATT_KV), lambda i, j: (i, j, 4)),
            pl.BlockSpec((1, tr, ATT_KV), lambda i, j: (i, j, 5)),
            pl.BlockSpec((1, tr, IDX_Q), lambda i, j: (i, j, 3)),
            pl.BlockSpec((1, tr, LANES), lambda i, j: (i, j, 0)),
            pl.BlockSpec((tr, LANES), lambda i, j: (j, 0)),
            pl.BlockSpec((tr, LANES), lambda i, j: (j, 0)),
            pl.BlockSpec((1, LANES), full2),
            pl.BlockSpec((1, LANES), full2),
            pl.BlockSpec((LANES, LANES), full2),
        ],
        out_specs=out_specs,
        out_shape=out_shape,
        compiler_params=_cparams(("parallel", "parallel")),
        name="dsa_prep",
    )(u, u, u, u, u_small, cos, sin, qg, kg, seg)


def _fold8(x, op, rows=8):
    r, n = x.shape
    x = x.reshape(r // rows, rows, n)
    while x.shape[0] > 1:
        h = x.shape[0] // 2
        y = op(x[:h], x[h:2 * h])
        x = y if x.shape[0] % 2 == 0 else jnp.concatenate([y, x[2 * h:]], axis=0)
    return x[0]


def _dsa_kernel(qit_ref, wit_ref, ki_ref, qt_ref, k_ref, vt_ref, o_ref,
                rhs_i_ref, rhs_q_ref, key_ref, s_ref, acc_ref, mm_ref, cnt_ref,
                *, tq, cw, l_keys, start, nsel):
    i = pl.program_id(1)
    cols4 = ATT_GROUPS * tq
    last_pos = start + i * tq + (tq - 1)
    max_limit = jnp.minimum((lax.shift_right_logical(last_pos, 6) + 1) * CHUNK, l_keys)
    n_c = lax.div(max_limit + (cw - 1), cw)

    pos = start + i * tq + lax.broadcasted_iota(I32, (1, tq), 1)
    limit = jnp.minimum((lax.shift_right_logical(pos, 6) + 1) * CHUNK, l_keys)
    sub_pos = lax.broadcasted_iota(I32, (cw, 1), 0)

    for h in range(IDX_HEADS):
        rhs_i_ref[:, h * tq:(h + 1) * tq] = qit_ref[0, 0, h * IDX_HD:(h + 1) * IDX_HD, :]

    @pl.when(i == 0)
    def _():
        rhs_q_ref[...] = jnp.zeros_like(rhs_q_ref)

    for j in range(KV_HEADS):
        for g in range(ATT_GROUPS):
            hq = ATT_GROUPS * j + g
            rhs_q_ref[j, j * ATT_HD:(j + 1) * ATT_HD, g * tq:(g + 1) * tq] = \
                qt_ref[0, 0, hq * ATT_HD:(hq + 1) * ATT_HD, :]

    w = [wit_ref[0, 0, h:h + 1, :] * IDX_SCALE for h in range(IDX_HEADS)]

    def score_chunk(c, carry):
        d = jnp.maximum(_dot(ki_ref[0, c], rhs_i_ref[...]), 0.0)
        score = d[:, 0:tq] * w[0]
        for h in range(1, IDX_HEADS):
            score = score + d[:, h * tq:(h + 1) * tq] * w[h]
        bits = pltpu.bitcast(score, I32)
        key = jnp.where(bits < 0, bits ^ 0x7FFFFFFF, bits)
        key = jnp.where(score == 0.0, 0, key)
        key_ref[c] = jnp.where(sub_pos + c * cw < limit, key, INT_MIN)
        return carry

    lax.fori_loop(0, n_c, score_chunk, 0)

    def count(pred):
        def hits(c):
            return _fold8(jnp.where(pred(key_ref[c], c * cw), 1.0, 0.0), jnp.add)

        def body(c2, acc):
            return acc + hits(2 * c2) + hits(2 * c2 + 1)

        cnt_ref[...] = lax.fori_loop(0, lax.shift_right_logical(n_c, 1), body, jnp.zeros((8, tq), F32))

        @pl.when((n_c & 1) == 1)
        def _():
            cnt_ref[...] += hits(n_c - 1)

        return jnp.sum(cnt_ref[...], axis=0, keepdims=True)

    def search(it, lo):
        cand = lo + lax.shift_left(jnp.int32(1), 31 - it)
        cnt = count(lambda k, base: k >= cand)
        return jnp.where(cnt >= nsel, cand, lo)

    thr = lax.fori_loop(0, 32, search, jnp.full((1, tq), INT_MIN, I32))

    c_gt = count(lambda k, base: k > thr)
    n_eq = count(lambda k, base: k == thr)
    need = nsel - c_gt
    excess = jnp.where((n_eq > need) & (thr > INT_MIN), 1.0, 0.0)
    mm_ref[...] = jnp.full((1, tq), 1 << 14, I32)

    @pl.when(jnp.max(excess) > 0.0)
    def _():
        def tie_search(it, m):
            cand = m + lax.shift_left(jnp.int32(1), 13 - it)
            f = count(lambda k, base: (k == thr) & (sub_pos + base < cand))
            return jnp.where(f < need, cand, m)

        mm_ref[...] = lax.fori_loop(0, 14, tie_search, jnp.zeros((1, tq), I32))

    mm = mm_ref[...]

    scale = (ATT_HD ** -0.5) * math.log2(math.e)
    m0 = jnp.full((8, cols4), -3e38, F32)
    for j0 in range(0, KV_HEADS, DSA_HPT):
        pair = tuple(range(j0, j0 + DSA_HPT))

        def qk_chunk(c, ms, pair=pair):
            k = key_ref[c]
            kpos = sub_pos + c * cw
            sel = ((k > thr) | ((k == thr) & (kpos <= mm))) & (kpos < limit)
            b1 = jnp.where(sel, 0.0, NEG_BIG)
            b4 = jnp.concatenate([b1] * ATT_GROUPS, axis=1)
            out = []
            for jj, j in enumerate(pair):
                s = _dot(k_ref[0, c], rhs_q_ref[j]) * scale + b4
                s_ref[jj, c] = s
                out.append(jnp.maximum(ms[jj], _fold8(s, jnp.maximum)))
            return tuple(out)

        ms = lax.fori_loop(0, n_c, qk_chunk, (m0,) * DSA_HPT)
        ms = [jnp.max(m, axis=0, keepdims=True) for m in ms]
        acc_ref[...] = jnp.zeros_like(acc_ref)

        def pv_chunk(c, carry, pair=pair, ms=ms):
            for jj, j in enumerate(pair):
                p = jnp.exp2(s_ref[jj, c] - ms[jj])
                acc_ref[jj] += _dot(vt_ref[0, j, c], p.astype(BF16))
            return carry

        lax.fori_loop(0, n_c, pv_chunk, 0)
        for jj, j in enumerate(pair):
            a = acc_ref[jj]
            o = a[0:ATT_HD] / a[ATT_HD:ATT_HD + 1]
            for g in range(0, ATT_GROUPS, 2):
                two = jnp.concatenate([o[:, g * tq:(g + 1) * tq], o[:, (g + 1) * tq:(g + 2) * tq]], axis=0)
                lo = (ATT_GROUPS * j + g) * ATT_HD
                o_ref[0, :, lo:lo + 2 * ATT_HD] = jnp.transpose(two).astype(o_ref.dtype)


def _dsa(qit, wit, ki, qt, k, vt, *, tq, cw, l_keys, start):
    b, nb = qit.shape[0], qit.shape[1]
    nc = k.shape[1]
    nsel = min(TOPK_MAX, l_keys // 4)
    kern = functools.partial(_dsa_kernel, tq=tq, cw=cw, l_keys=l_keys, start=start, nsel=float(nsel))
    cols4 = ATT_GROUPS * tq
    return pl.pallas_call(
        kern,
        grid=(b, nb),
        in_specs=[
            pl.BlockSpec((1, 1, IDX_Q, tq), lambda i, j: (i, j, 0, 0)),
            pl.BlockSpec((1, 1, IDX_HEADS, tq), lambda i, j: (i, j, 0, 0)),
            pl.BlockSpec((1, nc, cw, IDX_HD), lambda i, j: (i, 0, 0, 0)),
            pl.BlockSpec((1, 1, ATT_Q, tq), lambda i, j: (i, j, 0, 0)),
            pl.BlockSpec((1, nc, cw, ATT_KV), lambda i, j: (i, 0, 0, 0)),
            pl.BlockSpec((1, KV_HEADS, nc, V_ROWS, cw), lambda i, j: (i, 0, 0, 0, 0)),
        ],
        out_specs=pl.BlockSpec((1, tq, ATT_Q), lambda i, j: (i, j, 0)),
        out_shape=jax.ShapeDtypeStruct((b, nb * tq, ATT_Q), BF16),
        scratch_shapes=[
            pltpu.VMEM((IDX_HD, IDX_HEADS * tq), BF16),
            pltpu.VMEM((KV_HEADS, ATT_KV, cols4), BF16),
            pltpu.VMEM((nc, cw, tq), I32),
            pltpu.VMEM((DSA_HPT, nc, cw, cols4), F32),
            pltpu.VMEM((DSA_HPT, V_ROWS, cols4), F32),
            pltpu.VMEM((1, tq), I32),
            pltpu.VMEM((8, tq), F32),
        ],
        compiler_params=_cparams(("parallel", "arbitrary")),
        name="dsa",
    )(qit, wit, ki, qt, k, vt)


def _dsa_key_layouts(k_all, v_all, ki_all, cw):
    b, l_keys, _ = k_all.shape
    nc = -(-l_keys // cw)
    padk = lambda a: jnp.pad(a, ((0, 0), (0, nc * cw - l_keys), (0, 0))).astype(BF16)
    vt = padk(v_all).reshape(b, nc, cw, KV_HEADS, ATT_HD).transpose(0, 3, 1, 4, 2)
    ones = jnp.ones((b, KV_HEADS, nc, 1, cw), BF16)
    zeros = jnp.zeros((b, KV_HEADS, nc, V_ROWS - ATT_HD - 1, cw), BF16)
    return (padk(ki_all).reshape(b, nc, cw, IDX_HD), padk(k_all).reshape(b, nc, cw, ATT_KV),
            jnp.concatenate([vt, ones, zeros], axis=3))


def _prep_weights(ssd_w_in, att_w_in, ssd_w_out, att_w_out, mem_w_kv, dense_w_gate, dense_w_up,
                  dense_w_down, moe_router, moe_w_gate, moe_w_up, moe_w_down):
    bf = lambda w: w.astype(BF16)
    w = ssd_w_in[0]
    o_xbc, o_dt, o_mq = D_INNER, D_INNER + CONV_DIM, D_INNER + CONV_DIM + SSD_HEADS
    ssd_main = bf(jnp.concatenate([w[:, o_xbc:o_dt], w[:, o_mq:], w[:, :D_INNER]], axis=1))
    ssd_small = bf(jnp.pad(w[:, o_dt:o_mq], ((0, 0), (0, LANES - SSD_HEADS))))
    w = att_w_in[0]
    o_wi = ATT_Q + 2 * ATT_KV + IDX_Q
    o_ki = o_wi + IDX_HEADS
    o_mq = o_ki + IDX_HD
    att_main = bf(jnp.concatenate([w[:, :o_wi], w[:, o_mq:]], axis=1))
    att_small = bf(jnp.pad(jnp.concatenate([w[:, o_ki:o_mq], w[:, o_wi:o_ki]], axis=1),
                           ((0, 0), (0, LANES - IDX_HD - IDX_HEADS))))
    r = jnp.pad(moe_router[0], ((0, 0), (0, LANES - N_EXPERTS)))
    r_hi = bf(r)
    r_lo = bf(r - r_hi.astype(F32))
    return dict(
        ssd_main=ssd_main, ssd_small=ssd_small, att_main=att_main, att_small=att_small,
        ssd_out_a=bf(ssd_w_out[0, :D_INNER]), ssd_out_b=bf(ssd_w_out[0, D_INNER:]),
        att_out_a=bf(att_w_out[0, :ATT_Q]), att_out_b=bf(att_w_out[0, ATT_Q:]),
        mem_w_kv=bf(mem_w_kv), dense_g=bf(dense_w_gate[0]), dense_u=bf(dense_w_up[0]),
        dense_d=bf(dense_w_down[0]), r_hi=r_hi, r_lo=r_lo,
        moe_g=bf(moe_w_gate[0]), moe_u=bf(moe_w_up[0]), moe_d=bf(moe_w_down[0]))


def _rope_tables(pos):
    half = ATT_HD // 2
    inv = ROPE_THETA ** (-jnp.arange(half, dtype=F32) / half)
    ang = pos.astype(F32)[:, None] * inv[None, :]
    cos = jnp.cos(ang)
    sin = jnp.sin(ang)
    cos_t = jnp.concatenate([cos, cos, cos, cos], axis=1)
    sin_t = jnp.concatenate([-sin, sin, -sin, sin], axis=1)
    return cos_t, sin_t


def _trunk(x, start, mem_k, mem_v, conv_in, ssm_in, kv_in, P, W, q_ssd, cw_dsa):
    b, t, _ = x.shape
    n = b * t
    x2 = x.reshape(n, D_MODEL)

    u = _norm_matmul(x2, W['ssd_norm'][0], P['ssd_main'], tn=1024, out_dtype=BF16).reshape(b, t, -1)
    u_small = _norm_matmul(x2, W['ssd_norm'][0], P['ssd_small'], tn=LANES).reshape(b, t, LANES)
    conv8 = jnp.pad(conv_in, ((0, 0), (8 - (CONV_W - 1), 0), (0, 0)))
    cw8 = jnp.pad(W['ssd_conv_w'][0], ((0, 8 - CONV_W), (0, 0)))
    pad_h = lambda v: jnp.pad(v.astype(F32), (0, LANES - SSD_HEADS)).reshape(1, LANES)
    e_mat = (jnp.arange(LANES)[:, None] == (jnp.arange(D_INNER)[None, :] // SSD_HEADDIM)).astype(BF16)
    y_mix, new_ssm = _ssd(u, u_small, conv8, ssm_in, cw8, W['ssd_conv_b'][0].reshape(1, CONV_DIM),
                      pad_h(W['ssd_dt_bias'][0]), pad_h(W['ssd_A_log'][0]),
                      jnp.repeat(W['ssd_D'][0].astype(F32), SSD_HEADDIM).reshape(1, D_INNER),
                      W['ssd_out_norm'][0].reshape(1, D_INNER), e_mat, q_ssd)
    new_conv = u[:, t - (CONV_W - 1):, :CONV_DIM].astype(F32)
    y_mem = _mem_attn(u, 3, mem_k[0], mem_v[0], W['mem_q_norm'][0])
    x2 = _proj_res(y_mix.reshape(n, D_INNER), y_mem.reshape(n, MEM_WIDTH), P['ssd_out_a'], P['ssd_out_b'], x2)
    x2 = _ffn(x2, W['ffn_norm'][0], P['dense_g'], P['dense_u'], P['dense_d'])

    u = _norm_matmul(x2, W['att_norm'][0], P['att_main'], tn=1024, out_dtype=BF16).reshape(b, t, -1)
    u_small = _norm_matmul(x2, W['att_norm'][0], P['att_small'], tn=LANES).reshape(b, t, LANES)
    pos = start + jnp.arange(t)
    cos_t, sin_t = _rope_tables(pos)
    tile2 = lambda v: jnp.tile(v.astype(F32), 2).reshape(1, LANES)
    v_new = u[:, :, ATT_Q + ATT_KV:ATT_Q + 2 * ATT_KV].astype(F32)
    qg, kg = tile2(W['att_q_norm'][0]), tile2(W['att_k_norm'][0])
    if kv_in is None:
        k_rot, ki_rot, qt, qit, wit, k_b, ki_b, vt = _dsa_prep(u, u_small, cos_t, sin_t, qg, kg,
                                                               tr=cw_dsa, emit_keys=True)
        nc = t // cw_dsa
        keys = (ki_b.reshape(b, nc, cw_dsa, IDX_HD), k_b.reshape(b, nc, cw_dsa, ATT_KV), vt)
        ki_new = ki_rot[:, :, :IDX_HD]
        l_keys = t
    else:
        padt = lambda a: jnp.pad(a, ((0, DSA_TQ - t),) + ((0, 0),) * (a.ndim - 1))
        padbt = lambda a: jnp.pad(a, ((0, 0), (0, DSA_TQ - t), (0, 0)))
        k_rot, ki_rot, qt, qit, wit = _dsa_prep(padbt(u), padbt(u_small), padt(cos_t), padt(sin_t), qg, kg,
                                                tr=DSA_TQ, emit_keys=False)
        k_rot, ki_rot = k_rot[:, :t], ki_rot[:, :t]
        ki_new = ki_rot[:, :, :IDX_HD]
        keys = _dsa_key_layouts(jnp.concatenate([kv_in[0], k_rot], axis=1),
                                jnp.concatenate([kv_in[1], v_new], axis=1),
                                jnp.concatenate([kv_in[2], ki_new], axis=1), cw_dsa)
        l_keys = kv_in[0].shape[1] + t
    o_t = _dsa(qit, wit, keys[0], qt, keys[1], keys[2], tq=DSA_TQ, cw=cw_dsa, l_keys=l_keys, start=start)
    y_mix = o_t[:, :t].reshape(n, ATT_Q)
    y_mem = _mem_attn(u, 2, mem_k[1], mem_v[1], W['mem_q_norm'][1])
    x2 = _proj_res(y_mix, y_mem.reshape(n, MEM_WIDTH), P['att_out_a'], P['att_out_b'], x2)
    x2 = _moe(x2, W['ffn_norm'][1], P['r_hi'], P['r_lo'], P['moe_g'], P['moe_u'], P['moe_d'])

    return (x2.reshape(b, t, D_MODEL), new_conv, new_ssm, k_rot.reshape(b, t, KV_HEADS, ATT_HD),
            v_new.reshape(b, t, KV_HEADS, ATT_HD), ki_new)


def kernel(x_prompt, x_sample, mem_prompt, cache_conv, state_ssm, cache_k, cache_v, cache_idx_k, cache_mem_k, cache_mem_v, ssd_norm, ssd_w_in, ssd_conv_w, ssd_conv_b, ssd_dt_bias, ssd_A_log, ssd_D, ssd_out_norm, ssd_w_out, att_norm, att_w_in, att_q_norm, att_k_norm, att_w_out, mem_norm, mem_w_kv, mem_q_norm, mem_k_norm, ffn_norm, dense_w_gate, dense_w_up, dense_w_down, moe_router, moe_w_gate, moe_w_up, moe_w_down):
    W = dict(ssd_norm=ssd_norm, ssd_conv_w=ssd_conv_w, ssd_conv_b=ssd_conv_b, ssd_dt_bias=ssd_dt_bias,
             ssd_A_log=ssd_A_log, ssd_D=ssd_D, ssd_out_norm=ssd_out_norm, att_norm=att_norm,
             att_q_norm=att_q_norm, att_k_norm=att_k_norm, mem_q_norm=mem_q_norm, ffn_norm=ffn_norm)
    P = _prep_weights(ssd_w_in, att_w_in, ssd_w_out, att_w_out, mem_w_kv, dense_w_gate, dense_w_up,
                      dense_w_down, moe_router, moe_w_gate, moe_w_up, moe_w_down)
    bp, sp = x_prompt.shape[0], x_prompt.shape[1]
    bs = x_sample.shape[0]

    mem2 = mem_prompt.reshape(bp * N_MEM, D_MODEL)
    pk, pv = [], []
    for i in range(2):
        kv = _norm_matmul(mem2, mem_norm[i], P['mem_w_kv'][i], tn=1024)
        pk.append(_head_norm(kv[:, :MEM_WIDTH], mem_k_norm[i]).reshape(bp, N_MEM, MEM_WIDTH))
        pv.append(kv[:, MEM_WIDTH:].reshape(bp, N_MEM, MEM_WIDTH))
    p_mem_k = jnp.stack(pk)
    p_mem_v = jnp.stack(pv)

    conv0 = jnp.zeros((bp, CONV_W - 1, CONV_DIM), F32)
    ssm0 = jnp.zeros((bp, SSD_HEADS, SSD_HEADDIM, D_STATE), F32)
    y_p, p_conv, p_ssm, p_k, p_v, p_ki = _trunk(x_prompt, 0, p_mem_k, p_mem_v, conv0, ssm0, None, P, W,
                                                q_ssd=128, cw_dsa=512)
    past = cache_k.shape[2]
    kv_in = (cache_k[0].reshape(bs, past, ATT_KV), cache_v[0].reshape(bs, past, ATT_KV), cache_idx_k[0])
    y_s, s_conv, s_ssm, s_k, s_v, s_ki = _trunk(
        x_sample, past, cache_mem_k.reshape(2, bs, N_MEM, MEM_WIDTH), cache_mem_v.reshape(2, bs, N_MEM, MEM_WIDTH),
        cache_conv[0], state_ssm[0], kv_in, P, W, q_ssd=x_sample.shape[1], cw_dsa=384)

    shp = (bp, N_MEM, MEM_HEADS, MEM_HD)
    return (y_p, y_s, p_conv[None], p_ssm[None], p_k[None], p_v[None], p_ki[None],
            p_mem_k.reshape((2,) + shp), p_mem_v.reshape((2,) + shp),
            s_conv[None], s_ssm[None], s_k[None], s_v[None], s_ki[None])
```
